```python
import math
import jax, jax.numpy as jnp
from jax import lax
import numpy as np

D_MODEL = 1024
BATCH = 16
SEQ = 2048
DEPTH = 2
DEC_BATCH = 32
DEC_SEQ = 4
PAST_LEN = 16384
PAGE_SIZE = 128

N_MIXERS = 2
N_ATTN_LAYERS = (DEPTH + 1) // 2
N_LRU_LAYERS = DEPTH // 2

N_HEADS = 8
N_KV_HEADS = 2
HEAD_DIM = D_MODEL // N_HEADS
ATTN_WIDTH = N_HEADS * HEAD_DIM
KV_WIDTH = N_KV_HEADS * HEAD_DIM
N_IDX_HEADS = 16
IDX_DIM = 64
TOPK_MAX = 256
ROPE_THETA = 500000.0
ROT_FRAC = 4
Q_BLOCK = 128
ATTN_SPLITS = (ATTN_WIDTH, KV_WIDTH, KV_WIDTH, N_IDX_HEADS * IDX_DIM, IDX_DIM, N_IDX_HEADS, ATTN_WIDTH)
IN_A = sum(ATTN_SPLITS)

LRU_WIDTH = D_MODEL
N_LRU_BLOCKS = 4
LRU_BLOCK = LRU_WIDTH // N_LRU_BLOCKS
CONV_W = 4
LRU_C = 8.0

EPS = 1e-6
POOL_SLACK = 1.25

kernel_name = "hybrid_dsa_rglru_adaln_step"

F32 = jnp.float32


def rmsnorm(x, g):
    xf = x.astype(F32)
    y = xf * lax.rsqrt(jnp.mean(xf * xf, axis=-1, keepdims=True) + EPS)
    return (y * g.astype(F32)).astype(x.dtype)


def modulate(x, c, g, aw, ab):
    m = jax.nn.silu(c) @ aw + ab
    shift, scale, gate = jnp.split(m, 3, axis=-1)
    h = rmsnorm(x, g) * (1 + scale[:, None]) + shift[:, None]
    return h, gate[:, None]


def rope(x, pos):
    d = x.shape[-1]
    r = d // ROT_FRAC
    half = r // 2
    inv = jnp.exp(-math.log(ROPE_THETA) * jnp.arange(half, dtype=F32) * 2.0 / r)
    ang = pos.astype(F32)[:, None] * inv[None, :]
    cos = jnp.cos(ang)[:, None, :]
    sin = jnp.sin(ang)[:, None, :]
    xf = x.astype(F32)
    x1 = xf[..., :half]
    x2 = xf[..., half:r]
    out = jnp.concatenate([x1 * cos - x2 * sin, x2 * cos + x1 * sin, xf[..., r:]], axis=-1)
    return out.astype(x.dtype)


def attn_project(h, pos, w_in, q_norm, k_norm):
    B, T, _ = h.shape
    z = h @ w_in
    offs = [int(o) for o in np.cumsum(ATTN_SPLITS)[:-1]]
    q, k, v, qi, ki, wi, g = jnp.split(z, offs, axis=-1)
    q = rope(rmsnorm(q.reshape(B, T, N_HEADS, HEAD_DIM), q_norm), pos)
    k = rope(rmsnorm(k.reshape(B, T, N_KV_HEADS, HEAD_DIM), k_norm), pos)
    v = v.reshape(B, T, N_KV_HEADS, HEAD_DIM)
    qi = rope(qi.reshape(B, T, N_IDX_HEADS, IDX_DIM), pos)
    ki = rope(ki.reshape(B, T, 1, IDX_DIM), pos)[:, :, 0]
    wi = wi * (N_IDX_HEADS ** -0.5 * IDX_DIM ** -0.5)
    return q, k, v, qi, ki, wi, g


def indexer_select(qi, wi, ki, q_pos, k_sel):
    dots = jnp.einsum('thd,ld->thl', qi.astype(F32), ki.astype(F32))
    score = jnp.einsum('th,thl->tl', wi.astype(F32), jax.nn.relu(dots))
    admissible = jnp.arange(ki.shape[0])[None, :] <= q_pos[:, None]
    score = jnp.where(admissible, score, -jnp.inf)
    _, idx = lax.top_k(score, k_sel)
    valid = idx <= q_pos[:, None]
    return idx, valid


def sparse_attend(q, k_sel, v_sel, valid):
    T = q.shape[0]
    qg = q.reshape(T, N_KV_HEADS, N_HEADS // N_KV_HEADS, HEAD_DIM).astype(F32)
    s = jnp.einsum('tkgd,tskd->tkgs', qg, k_sel.astype(F32)) * (HEAD_DIM ** -0.5)
    s = jnp.where(valid[:, None, None, :], s, -jnp.inf)
    p = jax.nn.softmax(s, axis=-1)
    o = jnp.einsum('tkgs,tskd->tkgd', p, v_sel.astype(F32))
    return o.reshape(T, ATTN_WIDTH).astype(q.dtype)


def attn_prompt(q, k, v, qi, ki, wi):
    B, S = q.shape[:2]
    k_sel = min(TOPK_MAX, S // 4)
    nb = S // Q_BLOCK
    pos_blocks = jnp.arange(S).reshape(nb, Q_BLOCK)

    def per_seq(args):
        q_b, k_b, v_b, qi_b, ki_b, wi_b = args

        def per_block(blk):
            qb, qib, wib, pos = blk
            idx, valid = indexer_select(qib, wib, ki_b, pos, k_sel)
            return sparse_attend(qb, k_b[idx], v_b[idx], valid)

        blocks = (q_b.reshape(nb, Q_BLOCK, N_HEADS, HEAD_DIM),
                  qi_b.reshape(nb, Q_BLOCK, N_IDX_HEADS, IDX_DIM),
                  wi_b.reshape(nb, Q_BLOCK, N_IDX_HEADS),
                  pos_blocks)
        return lax.map(per_block, blocks).reshape(S, ATTN_WIDTH)

    return lax.map(per_seq, (q, k, v, qi, ki, wi))


def attn_sample(q, k_new, v_new, qi, ki_new, wi, cache_k, cache_v, cache_idx_k, page_table, li):
    Bd, T = q.shape[:2]
    past = page_table.shape[1] * PAGE_SIZE
    L = past + T
    k_sel = min(TOPK_MAX, L // 4)
    ki_past = cache_idx_k[li, page_table].reshape(Bd, past, IDX_DIM)
    ki_all = jnp.concatenate([ki_past, ki_new.astype(ki_past.dtype)], axis=1)
    pos = past + jnp.arange(T)
    idx, valid = jax.vmap(lambda a, b, c_: indexer_select(a, b, c_, pos, k_sel))(qi, wi, ki_all)
    bidx = jnp.arange(Bd)[:, None, None]
    pidx = jnp.minimum(idx, past - 1)
    phys = page_table[bidx, pidx // PAGE_SIZE]
    slot = pidx % PAGE_SIZE
    nidx = jnp.clip(idx - past, 0, T - 1)
    in_past = (idx < past)[..., None, None]
    ks = jnp.where(in_past, cache_k[li, phys, slot], k_new[bidx, nidx])
    vs = jnp.where(in_past, cache_v[li, phys, slot], v_new[bidx, nidx])
    return jax.vmap(sparse_attend)(q, ks, vs, valid)


def causal_conv(xb, buf, w, b):
    xp = jnp.concatenate([buf.astype(xb.dtype), xb], axis=1)
    out = lax.conv_general_dilated(xp, w[:, None, :].astype(xb.dtype), window_strides=(1,), padding='VALID',
                                   dimension_numbers=('NWC', 'WIO', 'NWC'), feature_group_count=xb.shape[-1])
    return out + b, xp[:, -(CONV_W - 1):]


def rglru(xc, h0, w_a, b_a, w_x, b_x, lam):
    B, T, W = xc.shape
    xf = xc.astype(F32)
    xblk = xf.reshape(B, T, N_LRU_BLOCKS, LRU_BLOCK)
    r = jax.nn.sigmoid(jnp.einsum('btni,nij->btnj', xblk, w_a.astype(F32)).reshape(B, T, W) + b_a.astype(F32))
    i = jax.nn.sigmoid(jnp.einsum('btni,nij->btnj', xblk, w_x.astype(F32)).reshape(B, T, W) + b_x.astype(F32))
    log_a = LRU_C * r * jax.nn.log_sigmoid(lam.astype(F32))
    a = jnp.exp(log_a)
    bterm = jnp.sqrt(-jnp.expm1(2.0 * log_a)) * (i * xf)
    bterm = bterm.at[:, 0].add(a[:, 0] * h0.astype(F32))

    def comb(left, right):
        a_l, b_l = left
        a_r, b_r = right
        return a_l * a_r, a_r * b_l + b_r

    _, hs = lax.associative_scan(comb, (a, bterm), axis=1)
    return hs, hs[:, -1]


def lru_branch(h, buf, h0, w_in, conv_w, conv_b, w_a, b_a, w_x, b_x, lam, w_out):
    xb, g = jnp.split(h @ w_in, 2, axis=-1)
    xc, new_buf = causal_conv(xb, buf, conv_w, conv_b)
    hs, h_last = rglru(xc, h0, w_a, b_a, w_x, b_x, lam)
    out = (hs.astype(h.dtype) * jax.nn.silu(g)) @ w_out
    return out, new_buf, h_last.astype(h.dtype)


def setup_inputs(seed: int = 0) -> dict:
    key = jax.random.key(seed)
    ks = jax.random.split(key, 32)
    n_pages = PAST_LEN // PAGE_SIZE
    n_pool = int(math.ceil(POOL_SLACK * DEC_BATCH * n_pages))
    nrm = lambda k, shape, s=1.0: jax.random.normal(k, shape, F32) * s
    perm = jax.random.permutation(ks[0], n_pool)[: DEC_BATCH * n_pages]
    u = jax.random.uniform(ks[1], (N_LRU_LAYERS, LRU_WIDTH), F32, 0.9, 0.999)
    return {
        "x_prompt": nrm(ks[2], (BATCH, SEQ, D_MODEL)),
        "x_sample": nrm(ks[3], (DEC_BATCH, DEC_SEQ, D_MODEL)),
        "cache_k": nrm(ks[4], (N_ATTN_LAYERS, n_pool, PAGE_SIZE, N_KV_HEADS, HEAD_DIM)),
        "cache_v": nrm(ks[5], (N_ATTN_LAYERS, n_pool, PAGE_SIZE, N_KV_HEADS, HEAD_DIM)),
        "cache_idx_k": nrm(ks[6], (N_ATTN_LAYERS, n_pool, PAGE_SIZE, IDX_DIM)),
        "state_conv": nrm(ks[7], (N_LRU_LAYERS, DEC_BATCH, CONV_W - 1, LRU_WIDTH)),
        "state_h": nrm(ks[8], (N_LRU_LAYERS, DEC_BATCH, LRU_WIDTH), 0.5),
        "page_table": perm.reshape(DEC_BATCH, n_pages).astype(jnp.int32),
        "c_prompt": nrm(ks[9], (BATCH, D_MODEL)),
        "c_sample": nrm(ks[10], (DEC_BATCH, D_MODEL)),
        "norm_g": 1.0 + nrm(ks[11], (DEPTH, D_MODEL), 0.02),
        "ada_w": nrm(ks[12], (DEPTH, D_MODEL, 3 * D_MODEL), D_MODEL ** -0.5),
        "ada_b": nrm(ks[13], (DEPTH, 3 * D_MODEL), 0.02),
        "attn_w_in": nrm(ks[14], (N_ATTN_LAYERS, D_MODEL, IN_A), D_MODEL ** -0.5),
        "attn_q_norm": 1.0 + nrm(ks[15], (N_ATTN_LAYERS, HEAD_DIM), 0.02),
        "attn_k_norm": 1.0 + nrm(ks[16], (N_ATTN_LAYERS, HEAD_DIM), 0.02),
        "attn_w_out": nrm(ks[17], (N_ATTN_LAYERS, ATTN_WIDTH, D_MODEL), ATTN_WIDTH ** -0.5),
        "lru_w_in": nrm(ks[18], (N_LRU_LAYERS, D_MODEL, 2 * LRU_WIDTH), D_MODEL ** -0.5),
        "lru_conv_w": nrm(ks[19], (N_LRU_LAYERS, CONV_W, LRU_WIDTH), CONV_W ** -0.5),
        "lru_conv_b": nrm(ks[20], (N_LRU_LAYERS, LRU_WIDTH), 0.02),
        "lru_w_a": nrm(ks[21], (N_LRU_LAYERS, N_LRU_BLOCKS, LRU_BLOCK, LRU_BLOCK), LRU_BLOCK ** -0.5),
        "lru_b_a": nrm(ks[22], (N_LRU_LAYERS, LRU_WIDTH), 0.02),
        "lru_w_x": nrm(ks[23], (N_LRU_LAYERS, N_LRU_BLOCKS, LRU_BLOCK, LRU_BLOCK), LRU_BLOCK ** -0.5),
        "lru_b_x": nrm(ks[24], (N_LRU_LAYERS, LRU_WIDTH), 0.02),
        "lru_lam": jnp.log(u) - jnp.log1p(-u),
        "lru_w_out": nrm(ks[25], (N_LRU_LAYERS, LRU_WIDTH, D_MODEL), LRU_WIDTH ** -0.5),
    }


def reference(x_prompt, x_sample, cache_k, cache_v, cache_idx_k, state_conv, state_h, page_table,
              c_prompt, c_sample, norm_g, ada_w, ada_b, attn_w_in, attn_q_norm, attn_k_norm, attn_w_out,
              lru_w_in, lru_conv_w, lru_conv_b, lru_w_a, lru_b_a, lru_w_x, lru_b_x, lru_lam, lru_w_out):
    xp, xs = x_prompt, x_sample
    B, S, _ = xp.shape
    Bd, T, _ = xs.shape
    past = page_table.shape[1] * PAGE_SIZE
    pos_p = jnp.arange(S)
    pos_s = past + jnp.arange(T)
    kp_l, vp_l, ikp_l, ksm_l, vsm_l, iks_l = [], [], [], [], [], []
    cp_l, hp_l, cs_l, hs_l = [], [], [], []
    for layer in range(DEPTH):
        li = layer // N_MIXERS
        hp, gp = modulate(xp, c_prompt, norm_g[layer], ada_w[layer], ada_b[layer])
        hs, gs = modulate(xs, c_sample, norm_g[layer], ada_w[layer], ada_b[layer])
        if layer % N_MIXERS == 0:
            q, k, v, qi, ki, wi, g = attn_project(hp, pos_p, attn_w_in[li], attn_q_norm[li], attn_k_norm[li])
            o = attn_prompt(q, k, v, qi, ki, wi)
            xp = xp + gp * ((o * jax.nn.silu(g)) @ attn_w_out[li])
            kp_l.append(k); vp_l.append(v); ikp_l.append(ki)
            q2, k2, v2, qi2, ki2, wi2, g2 = attn_project(hs, pos_s, attn_w_in[li], attn_q_norm[li], attn_k_norm[li])
            o2 = attn_sample(q2, k2, v2, qi2, ki2, wi2, cache_k, cache_v, cache_idx_k, page_table, li)
            xs = xs + gs * ((o2 * jax.nn.silu(g2)) @ attn_w_out[li])
            ksm_l.append(k2); vsm_l.append(v2); iks_l.append(ki2)
        else:
            w = (lru_w_in[li], lru_conv_w[li], lru_conv_b[li], lru_w_a[li], lru_b_a[li],
                 lru_w_x[li], lru_b_x[li], lru_lam[li], lru_w_out[li])
            buf0 = jnp.zeros((B, CONV_W - 1, LRU_WIDTH), xp.dtype)
            h00 = jnp.zeros((B, LRU_WIDTH), F32)
            o, nbuf, hl = lru_branch(hp, buf0, h00, *w)
            xp = xp + gp * o
            cp_l.append(nbuf); hp_l.append(hl)
            o2, nbuf2, hl2 = lru_branch(hs, state_conv[li], state_h[li], *w)
            xs = xs + gs * o2
            cs_l.append(nbuf2); hs_l.append(hl2)
    k_prompt = jnp.stack(kp_l)
    v_prompt = jnp.stack(vp_l)
    ik_prompt = jnp.stack(ikp_l)
    k_sample = jnp.stack(ksm_l)
    v_sample = jnp.stack(vsm_l)
    ik_sample = jnp.stack(iks_l)
    conv_prompt = jnp.stack(cp_l)
    h_prompt = jnp.stack(hp_l)
    conv_sample = jnp.stack(cs_l)
    h_sample = jnp.stack(hs_l)
    return (xp, xs, k_prompt, v_prompt, ik_prompt, k_sample, v_sample, ik_sample,
            conv_prompt, h_prompt, conv_sample, h_sample)
```

```python
import functools
import math

import jax
import jax.numpy as jnp
from jax import lax
from jax.experimental import pallas as pl
from jax.experimental.pallas import tpu as pltpu

F32 = jnp.float32
BF16 = jnp.bfloat16
I32 = jnp.int32

D_MODEL = 1024
N_HEADS = 8
N_KV_HEADS = 2
HEAD_DIM = 128
GROUP = N_HEADS // N_KV_HEADS
ATTN_WIDTH = N_HEADS * HEAD_DIM
KV_WIDTH = N_KV_HEADS * HEAD_DIM
N_IDX_HEADS = 16
IDX_DIM = 64
IDX_WIDTH = N_IDX_HEADS * IDX_DIM
TOPK_MAX = 256
ROPE_THETA = 500000.0
ROT_FRAC = 4
PAGE_SIZE = 128
LRU_WIDTH = D_MODEL
N_LRU_BLOCKS = 4
LRU_BLOCK = LRU_WIDTH // N_LRU_BLOCKS
CONV_W = 4
LRU_C = 8.0
EPS = 1e-6

LANES = 128
SUBLANES = 8
MXU_COLS = 256
VMEM_LIMIT = 56 * 1024 * 1024

OFF_Q = 0
OFF_K = OFF_Q + ATTN_WIDTH
OFF_V = OFF_K + KV_WIDTH
OFF_QI = OFF_V + KV_WIDTH
OFF_G = OFF_QI + IDX_WIDTH
OFF_KW = OFF_G + ATTN_WIDTH
PROJ_WIDTH = OFF_KW + LANES

INT_MIN = -(2 ** 31)
KEY_NEG_INF = 0x807FFFFF - 2 ** 32
RADIX_UNROLL = 2
COUNT_SLAB_ROWS = 64
LOG2E = 1.4426950408889634
NT_DIMS = (((1,), (1,)), ((), ()))


def _dot(a, b):
    return jnp.dot(a, b, preferred_element_type=F32)


def _dot_nt(a, b):
    return lax.dot_general(a, b, NT_DIMS, preferred_element_type=F32)


def _silu(x):
    return x / (1.0 + jnp.exp(-x))


def _sigmoid(x):
    return 1.0 / (1.0 + jnp.exp(-x))


def _rmsnorm(x, g):
    return x * lax.rsqrt(jnp.mean(x * x, axis=-1, keepdims=True) + EPS) * g


def _modulated_norm(x, g, mod):
    shift = mod[:, 0:D_MODEL]
    scale = mod[:, D_MODEL:2 * D_MODEL]
    return _rmsnorm(x, g) * (1.0 + scale) + shift


def _rope(y, tabs_ref, base, half):
    c = tabs_ref[base]
    s1 = tabs_ref[base + 1]
    s2 = tabs_ref[base + 2]
    return y * c + pltpu.roll(y, LANES - half, axis=1) * s1 + pltpu.roll(y, half, axis=1) * s2


def _key_to_f32(key):
    bits = jnp.where(key >= 0, key, key ^ 0x7FFFFFFF)
    return pltpu.bitcast(bits, F32)


def _kth_largest_key(sc_ref, n, k, axis):
    kf = float(k)
    shape = list(sc_ref.shape)
    shape[axis] = 1

    slab = COUNT_SLAB_ROWS if axis == 0 else LANES
    assert n % slab == 0

    def body(it, prefix):
        cand = prefix + lax.shift_left(jnp.int32(1), 31 - it)
        cand_f = _key_to_f32(cand)
        parts = []
        for j in range(n // slab):
            sc = sc_ref[j * slab:(j + 1) * slab, :] if axis == 0 else sc_ref[:, j * slab:(j + 1) * slab]
            parts.append(jnp.where(sc >= cand_f, 1.0, 0.0))
        while len(parts) > 1:
            nxt = [parts[a] + parts[a + 1] for a in range(0, len(parts) - 1, 2)]
            if len(parts) % 2:
                nxt.append(parts[-1])
            parts = nxt
        cnt = jnp.sum(parts[0], axis=axis, keepdims=True)
        return jnp.where(cnt >= kf, cand, prefix)

    return lax.fori_loop(0, 32, body, jnp.full(tuple(shape), INT_MIN, I32), unroll=RADIX_UNROLL)


def _ada_body(c_ref, w_ref, b_ref, o_ref):
    s = _silu(c_ref[...]).astype(BF16)
    o_ref[0] = _dot(s, w_ref[0].astype(BF16)) + b_ref[0]


def _ada(c_all, ada_w, ada_b):
    rows = c_all.shape[0]
    depth = ada_w.shape[0]
    nblk = 3
    return pl.pallas_call(
        _ada_body,
        grid=(depth, nblk),
        in_specs=[
            pl.BlockSpec((rows, D_MODEL), lambda l, j: (0, 0)),
            pl.BlockSpec((1, D_MODEL, D_MODEL), lambda l, j: (l, 0, j)),
            pl.BlockSpec((1, 1, D_MODEL), lambda l, j: (l, 0, j)),
        ],
        out_specs=pl.BlockSpec((1, rows, D_MODEL), lambda l, j: (l, 0, j)),
        out_shape=jax.ShapeDtypeStruct((depth, rows, 3 * D_MODEL), F32),
        compiler_params=pltpu.CompilerParams(
            dimension_semantics=("arbitrary", "arbitrary"), vmem_limit_bytes=VMEM_LIMIT),
        name="ada_mod",
    )(c_all, ada_w, ada_b.reshape(depth, 1, 3 * D_MODEL))


def _proj_body(x_ref, mod_ref, g_ref, w_ref, qn_ref, kn_ref, tabs_ref,
               q_ref, k_ref, v_ref, qi_ref, kw_ref, ki_ref, gs_ref):
    h = _modulated_norm(x_ref[...], g_ref[...], mod_ref[0]).astype(BF16)
    qn = qn_ref[...]
    kn = kn_ref[...]
    half_h = HEAD_DIM // ROT_FRAC // 2
    half_i = IDX_DIM // ROT_FRAC // 2

    def slabs(off, n_slabs):
        for c0 in range(0, n_slabs, MXU_COLS // LANES):
            z = _dot(h, w_ref[:, off + c0 * LANES:off + c0 * LANES + MXU_COLS])
            for e in range(MXU_COLS // LANES):
                yield c0 + e, z[:, e * LANES:(e + 1) * LANES]

    for hh, z in slabs(OFF_Q, N_HEADS):
        q_ref[:, hh * HEAD_DIM:(hh + 1) * HEAD_DIM] = _rope(_rmsnorm(z, qn), tabs_ref, 0, half_h).astype(BF16)
    tm = x_ref.shape[0]
    for hh, z in slabs(OFF_K, N_KV_HEADS):
        k_ref[pl.ds(hh, tm, stride=N_KV_HEADS), :] = _rope(_rmsnorm(z, kn), tabs_ref, 0, half_h)
    for hh, z in slabs(OFF_V, N_KV_HEADS):
        v_ref[pl.ds(hh, tm, stride=N_KV_HEADS), :] = z
    for c, z in slabs(OFF_QI, IDX_WIDTH // LANES):
        qi_ref[:, c * LANES:(c + 1) * LANES] = _rope(z, tabs_ref, 3, half_i).astype(BF16)
    for c, z in slabs(OFF_G, ATTN_WIDTH // LANES):
        gs_ref[:, c * LANES:(c + 1) * LANES] = _silu(z).astype(BF16)
    kw = _rope(_dot(h, w_ref[:, OFF_KW:OFF_KW + LANES]), tabs_ref, 6, half_i)
    kw_ref[...] = kw
    ki_ref[...] = kw[:, 0:IDX_DIM]


def _proj(x, mod, g, w, qn, kn, tabs, *, tm, rows_per_mod, tab_blocks):
    n = x.shape[0]
    mod_rows = mod.shape[1]
    grid = (n // tm,)
    row = lambda i: (i, 0)
    outs = [
        jax.ShapeDtypeStruct((n, ATTN_WIDTH), BF16),
        jax.ShapeDtypeStruct((n * N_KV_HEADS, HEAD_DIM), F32),
        jax.ShapeDtypeStruct((n * N_KV_HEADS, HEAD_DIM), F32),
        jax.ShapeDtypeStruct((n, IDX_WIDTH), BF16),
        jax.ShapeDtypeStruct((n, LANES), F32),
        jax.ShapeDtypeStruct((n, IDX_DIM), F32),
        jax.ShapeDtypeStruct((n, ATTN_WIDTH), BF16),
    ]
    return pl.pallas_call(
        _proj_body,
        grid=grid,
        in_specs=[
            pl.BlockSpec((tm, D_MODEL), row),
            pl.BlockSpec((1, mod_rows, 3 * D_MODEL), lambda i: (i * tm // rows_per_mod, 0, 0)),
            pl.BlockSpec((1, D_MODEL), lambda i: (0, 0)),
            pl.BlockSpec((D_MODEL, PROJ_WIDTH), lambda i: (0, 0)),
            pl.BlockSpec((1, HEAD_DIM), lambda i: (0, 0)),
            pl.BlockSpec((1, HEAD_DIM), lambda i: (0, 0)),
            pl.BlockSpec((9, tm, LANES), lambda i: (0, i % tab_blocks, 0)),
        ],
        out_specs=[
            pl.BlockSpec((tm, ATTN_WIDTH), row),
            pl.BlockSpec((tm * N_KV_HEADS, HEAD_DIM), row),
            pl.BlockSpec((tm * N_KV_HEADS, HEAD_DIM), row),
            pl.BlockSpec((tm, IDX_WIDTH), row),
            pl.BlockSpec((tm, LANES), row),
            pl.BlockSpec((tm, IDX_DIM), row),
            pl.BlockSpec((tm, ATTN_WIDTH), row),
        ],
        out_shape=outs,
        compiler_params=pltpu.CompilerParams(
            dimension_semantics=("arbitrary",), vmem_limit_bytes=VMEM_LIMIT),
        name="attn_proj",
    )(x, mod, g, w, qn, kn, tabs)


def _rope_tables(pos):
    posf = pos.astype(F32)
    t = pos.shape[0]

    def base(d):
        r = d // ROT_FRAC
        half = r // 2
        inv = jnp.exp(-jnp.log(jnp.asarray(ROPE_THETA, F32)) * jnp.arange(half, dtype=F32) * 2.0 / r)
        ang = posf[:, None] * inv[None, :]
        cos = jnp.cos(ang)
        sin = jnp.sin(ang)
        c = jnp.concatenate([cos, cos, jnp.ones((t, d - r), F32)], axis=1)
        s1 = jnp.concatenate([-sin, jnp.zeros((t, d - half), F32)], axis=1)
        s2 = jnp.concatenate([jnp.zeros((t, half), F32), sin, jnp.zeros((t, d - r), F32)], axis=1)
        return c, s1, s2

    hc, hs1, hs2 = base(HEAD_DIM)
    ic, is1, is2 = base(IDX_DIM)
    wi_scale = N_IDX_HEADS ** -0.5 * IDX_DIM ** -0.5
    pad = LANES - IDX_DIM
    kc = jnp.concatenate([ic, jnp.full((t, N_IDX_HEADS), wi_scale, F32),
                          jnp.zeros((t, pad - N_IDX_HEADS), F32)], axis=1)
    ks1 = jnp.concatenate([is1, jnp.zeros((t, pad), F32)], axis=1)
    ks2 = jnp.concatenate([is2, jnp.zeros((t, pad), F32)], axis=1)
    two = lambda a: jnp.concatenate([a, a], axis=1)
    return jnp.stack([hc, hs1, hs2, two(ic), two(is1), two(is2), kc, ks1, ks2])


QB = 128
KEY_BUCKET = 256


def _attn_prompt_body(q_ref, qi_ref, kwb_ref, kws_ref, k_ref, v_ref, gs_ref, x_ref, gate_ref,
                      wout_ref, o_ref, kbf, vbf, kipar, sct_ref, *, seq, k_sel):
    i = pl.program_id(1)

    @pl.when(i == 0)
    def _():
        for kh in range(N_KV_HEADS):
            kbf[kh] = k_ref[pl.ds(kh, seq, stride=N_KV_HEADS), :].astype(BF16)
            vbf[kh] = v_ref[pl.ds(kh, seq, stride=N_KV_HEADS), :].astype(BF16)
        kw = kws_ref[...]
        lane = lax.broadcasted_iota(I32, kw.shape, 1)
        ke = jnp.where(lane < IDX_DIM, kw, 0.0)
        kipar[0] = ke.astype(BF16)
        kipar[1] = pltpu.roll(ke, IDX_DIM, axis=1).astype(BF16)

    def block(nk):
        kwb = kwb_ref[...]
        score = None
        for h in range(N_IDX_HEADS):
            p, par = divmod(h, 2)
            d = _dot_nt(qi_ref[:, p * LANES:(p + 1) * LANES], kipar[par, 0:nk, :])
            term = kwb[:, IDX_DIM + h:IDX_DIM + h + 1] * jnp.maximum(d, 0.0)
            score = term if score is None else score + term
        col = lax.broadcasted_iota(I32, (QB, nk), 1)
        pos = i * QB + lax.broadcasted_iota(I32, (QB, nk), 0)
        score = jnp.where(col <= pos, score, -jnp.inf)

        sct_ref[0:nk, :] = score.T
        thr = _kth_largest_key(sct_ref, nk, k_sel, 0)
        key_row = lax.broadcasted_iota(I32, (nk, QB), 0)
        q_pos = i * QB + lax.broadcasted_iota(I32, (nk, QB), 1)
        sel_t = jnp.logical_and(
            key_row <= q_pos,
            jnp.logical_or(sct_ref[0:nk, :] >= _key_to_f32(thr), thr <= KEY_NEG_INF))
        bias = jnp.where(sel_t, 0.0, -jnp.inf).T

        c = HEAD_DIM ** -0.5 * LOG2E
        heads = [None] * N_HEADS
        for kh in range(N_KV_HEADS):
            qs = jnp.concatenate(
                [q_ref[:, (kh * GROUP + g) * HEAD_DIM:(kh * GROUP + g + 1) * HEAD_DIM] for g in range(GROUP)],
                axis=0)
            s = _dot_nt(qs, kbf[kh, 0:nk, :])
            ps = []
            ls = []
            for g in range(GROUP):
                sg = s[g * QB:(g + 1) * QB] + bias
                pg = jnp.exp2((sg - jnp.max(sg, axis=1, keepdims=True)) * c)
                ls.append(jnp.sum(pg, axis=1, keepdims=True))
                ps.append(pg.astype(BF16))
            o = _dot(jnp.concatenate(ps, axis=0), vbf[kh, 0:nk, :])
            for g in range(GROUP):
                heads[kh * GROUP + g] = o[g * QB:(g + 1) * QB] / ls[g]
        attn = jnp.concatenate(heads, axis=1)
        y = _dot((attn * gs_ref[...].astype(F32)).astype(BF16), wout_ref[...])
        o_ref[...] = x_ref[...] + gate_ref[0] * y

    blocks_per_bucket = KEY_BUCKET // QB
    for bucket in range(seq // KEY_BUCKET):
        pl.when(i // blocks_per_bucket == bucket)(functools.partial(block, (bucket + 1) * KEY_BUCKET))


def _attn_prompt(q, qi, kw, k, v, gs, x, mod, wout, *, batch, seq):
    nb = seq // QB
    k_sel = min(TOPK_MAX, seq // 4)
    blk = lambda b, i: (b * nb + i, 0)
    whole = lambda b, i: (b, 0)
    body = functools.partial(_attn_prompt_body, seq=seq, k_sel=k_sel)
    return pl.pallas_call(
        body,
        grid=(batch, nb),
        in_specs=[
            pl.BlockSpec((QB, ATTN_WIDTH), blk),
            pl.BlockSpec((QB, IDX_WIDTH), blk),
            pl.BlockSpec((QB, LANES), blk),
            pl.BlockSpec((seq, LANES), whole),
            pl.BlockSpec((seq * N_KV_HEADS, HEAD_DIM), whole),
            pl.BlockSpec((seq * N_KV_HEADS, HEAD_DIM), whole),
            pl.BlockSpec((QB, ATTN_WIDTH), blk),
            pl.BlockSpec((QB, D_MODEL), blk),
            pl.BlockSpec((1, 1, D_MODEL), lambda b, i: (b, 0, 2)),
            pl.BlockSpec((ATTN_WIDTH, D_MODEL), lambda b, i: (0, 0)),
        ],
        out_specs=pl.BlockSpec((QB, D_MODEL), blk),
        out_shape=jax.ShapeDtypeStruct((batch * seq, D_MODEL), F32),
        scratch_shapes=[
            pltpu.VMEM((N_KV_HEADS, seq, HEAD_DIM), BF16),
            pltpu.VMEM((N_KV_HEADS, seq, HEAD_DIM), BF16),
            pltpu.VMEM((2, seq, LANES), BF16),
            pltpu.VMEM((seq, QB), F32),
        ],
        compiler_params=pltpu.CompilerParams(
            dimension_semantics=("arbitrary", "arbitrary"), vmem_limit_bytes=VMEM_LIMIT),
        name="attn_prompt",
    )(q, qi, kw, kw, k, v, gs, x, mod, wout)


PAGES_PER_STEP = 16
SEQ_PER_STEP = 2


def _sample_tile_score(qir, wcol, kpage_t):
    d = _dot(qir, kpage_t.astype(BF16))
    r = wcol * jnp.maximum(d, 0.0)
    acc = r[0:SUBLANES]
    for j in range(1, r.shape[0] // SUBLANES):
        acc = acc + r[j * SUBLANES:(j + 1) * SUBLANES]
    half = SUBLANES // 2
    return acc[0:half] + acc[half:SUBLANES]


def _sample_score_body(pt_ref, qir_ref, wcol_ref, *rest, n_chunks, dec_seq):
    npg = SEQ_PER_STEP * PAGES_PER_STEP
    pages = rest[:npg]
    knew_ref = rest[npg]
    o_ref = rest[npg + 1]
    c = pl.program_id(1)

    @pl.when(c < n_chunks)
    def _():
        for s in range(SEQ_PER_STEP):
            kcat = jnp.concatenate([pages[s * PAGES_PER_STEP + j][0] for j in range(PAGES_PER_STEP)], axis=1)
            o_ref[s * dec_seq:(s + 1) * dec_seq, :] = _sample_tile_score(qir_ref[s], wcol_ref[s], kcat)

    @pl.when(c == n_chunks)
    def _():
        o_ref[...] = jnp.zeros(o_ref.shape, F32)
        for s in range(SEQ_PER_STEP):
            o_ref[s * dec_seq:(s + 1) * dec_seq, 0:PAGE_SIZE] = _sample_tile_score(
                qir_ref[s], wcol_ref[s], knew_ref[s])


def _sample_scores(page_table, qir, wcol, cache_ik, knew, *, dec_batch, dec_seq, n_pages):
    n_chunks = n_pages // PAGES_PER_STEP
    width = (n_chunks + 1) * PAGES_PER_STEP * PAGE_SIZE
    rows = IDX_DIM

    def page_map(s, j):
        def f(bp, c, pt):
            cc = jnp.minimum(c, n_chunks - 1)
            return (pt[bp * SEQ_PER_STEP + s, cc * PAGES_PER_STEP + j], 0, 0)
        return f

    in_specs = [
        pl.BlockSpec((SEQ_PER_STEP, rows, IDX_DIM), lambda bp, c, pt: (bp, 0, 0)),
        pl.BlockSpec((SEQ_PER_STEP, rows, 1), lambda bp, c, pt: (bp, 0, 0)),
    ]
    for s in range(SEQ_PER_STEP):
        for j in range(PAGES_PER_STEP):
            in_specs.append(pl.BlockSpec((1, IDX_DIM, PAGE_SIZE), page_map(s, j)))
    in_specs.append(pl.BlockSpec((SEQ_PER_STEP, IDX_DIM, PAGE_SIZE), lambda bp, c, pt: (bp, 0, 0)))
    body = functools.partial(_sample_score_body, n_chunks=n_chunks, dec_seq=dec_seq)
    return pl.pallas_call(
        body,
        grid_spec=pltpu.PrefetchScalarGridSpec(
            num_scalar_prefetch=1,
            grid=(dec_batch // SEQ_PER_STEP, n_chunks + 1),
            in_specs=in_specs,
            out_specs=pl.BlockSpec((SEQ_PER_STEP * dec_seq, PAGES_PER_STEP * PAGE_SIZE),
                                   lambda bp, c, pt: (bp, c)),
        ),
        out_shape=jax.ShapeDtypeStruct((dec_batch * dec_seq, width), F32),
        compiler_params=pltpu.CompilerParams(
            dimension_semantics=("arbitrary", "arbitrary"), vmem_limit_bytes=VMEM_LIMIT),
        name="sample_scores",
    )(page_table, qir, wcol, *([cache_ik] * (SEQ_PER_STEP * PAGES_PER_STEP)), knew)


SELECT_ROWS = 32


def _sample_select_body(sc_ref, sel_ref, key_ref, *, past, dec_seq, k_sel):
    shape = sc_ref.shape
    col = lax.broadcasted_iota(I32, shape, 1)
    t = lax.broadcasted_iota(I32, shape, 0) % dec_seq
    adm = (col - past) <= t
    key_ref[...] = jnp.where(adm, sc_ref[...], -jnp.inf)
    thr = _kth_largest_key(key_ref, shape[1], k_sel, 1)
    picked = jnp.logical_or(key_ref[...] >= _key_to_f32(thr), thr <= KEY_NEG_INF)
    sel_ref[...] = jnp.where(jnp.logical_and(picked, adm), 1.0, 0.0)


def _sample_select(scores, *, past, dec_seq):
    k_sel = min(TOPK_MAX, (past + dec_seq) // 4)
    body = functools.partial(_sample_select_body, past=past, dec_seq=dec_seq, k_sel=k_sel)
    rows, width = scores.shape
    assert rows % SELECT_ROWS == 0 and SELECT_ROWS % dec_seq == 0
    return pl.pallas_call(
        body,
        grid=(rows // SELECT_ROWS,),
        in_specs=[pl.BlockSpec((SELECT_ROWS, width), lambda i: (i, 0))],
        out_specs=pl.BlockSpec((SELECT_ROWS, width), lambda i: (i, 0)),
        out_shape=jax.ShapeDtypeStruct(scores.shape, F32),
        scratch_shapes=[pltpu.VMEM((SELECT_ROWS, width), F32)],
        compiler_params=pltpu.CompilerParams(
            dimension_semantics=("arbitrary",), vmem_limit_bytes=VMEM_LIMIT),
        name="sample_select",
    )(scores)


def _sample_attn_body(pt_ref, q_ref, sel_ref, *rest, n_chunks, dec_seq):
    npg = SEQ_PER_STEP * PAGES_PER_STEP
    kpages = rest[:npg]
    vpages = rest[npg:2 * npg]
    knew_ref, vnew_ref, o_ref, m_ref, l_ref, acc_ref = rest[2 * npg:]
    c = pl.program_id(1)
    scale = HEAD_DIM ** -0.5

    @pl.when(c == 0)
    def _():
        m_ref[...] = jnp.full(m_ref.shape, -jnp.inf, F32)
        l_ref[...] = jnp.zeros(l_ref.shape, F32)
        acc_ref[...] = jnp.zeros(acc_ref.shape, F32)

    def update(tiles, finish):
        n = SEQ_PER_STEP * N_KV_HEADS
        m_old = [m_ref[idx] for idx in range(n)]
        l_old = [l_ref[idx] for idx in range(n)]
        a_old = [acc_ref[idx] for idx in range(n)]
        m_out, l_out, a_out = [], [], []
        for idx, (kt, vt, sel) in enumerate(tiles):
            s, kh = divmod(idx, N_KV_HEADS)
            sc = _dot_nt(q_ref[s, kh], kt) * scale
            selg = jnp.concatenate([sel] * GROUP, axis=0) > 0.5
            sc = jnp.where(selg, sc, -jnp.inf)
            m_new = jnp.maximum(m_old[idx], jnp.max(sc, axis=1, keepdims=True))
            m_safe = jnp.where(m_new == -jnp.inf, 0.0, m_new)
            alpha = jnp.exp(m_old[idx] - m_safe)
            p = jnp.exp(sc - m_safe)
            m_out.append(m_new)
            l_out.append(alpha * l_old[idx] + jnp.sum(p, axis=1, keepdims=True))
            a_out.append(alpha * a_old[idx] + _dot(p.astype(BF16), vt))
        for idx in range(n):
            if finish:
                s, kh = divmod(idx, N_KV_HEADS)
                o_ref[s, kh] = a_out[idx] / l_out[idx]
            else:
                m_ref[idx] = m_out[idx]
                l_ref[idx] = l_out[idx]
                acc_ref[idx] = a_out[idx]

    def head_rows(page_refs, s, kh):
        return jnp.concatenate(
            [page_refs[s * PAGES_PER_STEP + j][0, pl.ds(kh, PAGE_SIZE, stride=N_KV_HEADS), :].astype(BF16)
             for j in range(PAGES_PER_STEP)], axis=0)

    @pl.when(c < n_chunks)
    def _():
        tiles = []
        for s in range(SEQ_PER_STEP):
            sel = sel_ref[s * dec_seq:(s + 1) * dec_seq, :]
            for kh in range(N_KV_HEADS):
                tiles.append((head_rows(kpages, s, kh), head_rows(vpages, s, kh), sel))
        update(tiles, False)

    @pl.when(c == n_chunks)
    def _():
        tiles = []
        for s in range(SEQ_PER_STEP):
            sel = sel_ref[s * dec_seq:(s + 1) * dec_seq, 0:PAGE_SIZE]
            for kh in range(N_KV_HEADS):
                tiles.append((knew_ref[s, kh].astype(BF16), vnew_ref[s, kh].astype(BF16), sel))
        update(tiles, True)


def _sample_attn(page_table, qs, sel, cache_k2, cache_v2, knew, vnew, *, dec_batch, dec_seq, n_pages):
    n_chunks = n_pages // PAGES_PER_STEP
    rows = GROUP * dec_seq

    def page_map(s, j):
        def f(bp, c, pt):
            cc = jnp.minimum(c, n_chunks - 1)
            return (pt[bp * SEQ_PER_STEP + s, cc * PAGES_PER_STEP + j], 0, 0)
        return f

    page_specs = []
    for s in range(SEQ_PER_STEP):
        for j in range(PAGES_PER_STEP):
            page_specs.append(pl.BlockSpec((1, PAGE_SIZE * N_KV_HEADS, HEAD_DIM), page_map(s, j)))
    new_spec = pl.BlockSpec((SEQ_PER_STEP, N_KV_HEADS, PAGE_SIZE, HEAD_DIM), lambda bp, c, pt: (bp, 0, 0, 0))
    in_specs = [
        pl.BlockSpec((SEQ_PER_STEP, N_KV_HEADS, rows, HEAD_DIM), lambda bp, c, pt: (bp, 0, 0, 0)),
        pl.BlockSpec((SEQ_PER_STEP * dec_seq, PAGES_PER_STEP * PAGE_SIZE), lambda bp, c, pt: (bp, c)),
    ] + page_specs + page_specs + [new_spec, new_spec]
    npg = SEQ_PER_STEP * PAGES_PER_STEP
    nstate = SEQ_PER_STEP * N_KV_HEADS
    body = functools.partial(_sample_attn_body, n_chunks=n_chunks, dec_seq=dec_seq)
    return pl.pallas_call(
        body,
        grid_spec=pltpu.PrefetchScalarGridSpec(
            num_scalar_prefetch=1,
            grid=(dec_batch // SEQ_PER_STEP, n_chunks + 1),
            in_specs=in_specs,
            out_specs=pl.BlockSpec((SEQ_PER_STEP, N_KV_HEADS, rows, HEAD_DIM), lambda bp, c, pt: (bp, 0, 0, 0)),
            scratch_shapes=[
                pltpu.VMEM((nstate, rows, 1), F32),
                pltpu.VMEM((nstate, rows, 1), F32),
                pltpu.VMEM((nstate, rows, HEAD_DIM), F32),
            ],
        ),
        out_shape=jax.ShapeDtypeStruct((dec_batch, N_KV_HEADS, rows, HEAD_DIM), F32),
        compiler_params=pltpu.CompilerParams(
            dimension_semantics=("arbitrary", "arbitrary"), vmem_limit_bytes=VMEM_LIMIT),
        name="sample_attn",
    )(page_table, qs, sel, *([cache_k2] * npg), *([cache_v2] * npg), knew, vnew)


def _log_sigmoid(x):
    return jnp.minimum(x, 0.0) - jnp.log1p(jnp.exp(-jnp.abs(x)))


def _lru_gates(xc, wa_ref, ba_ref, wx_ref, bx_ref, lam_ref):
    xcb = xc.astype(BF16)
    ra = []
    ia = []
    for n in range(N_LRU_BLOCKS):
        blk = xcb[:, n * LRU_BLOCK:(n + 1) * LRU_BLOCK]
        ra.append(_dot(blk, wa_ref[n]))
        ia.append(_dot(blk, wx_ref[n]))
    r = _sigmoid(jnp.concatenate(ra, axis=1) + ba_ref[...])
    ig = _sigmoid(jnp.concatenate(ia, axis=1) + bx_ref[...])
    log_a = (LRU_C * r) * _log_sigmoid(lam_ref[...])
    a = jnp.exp(log_a)
    b = jnp.sqrt(-jnp.tanh(log_a) * (a * a + 1.0)) * (ig * xc)
    return a, b


TL = 512


def _lru_prompt_body(x_ref, mod_ref, g_ref, win_ref, cw_ref, cb_ref, wa_ref, ba_ref, wx_ref, bx_ref,
                     lam_ref, wout_ref, y_ref, conv_ref, hl_ref, xpad, a_scr, b_scr, h_scr, hcar):
    i = pl.program_id(1)

    @pl.when(i == 0)
    def _():
        xpad[0:SUBLANES] = jnp.zeros((SUBLANES, LRU_WIDTH), F32)
        hcar[...] = jnp.zeros(hcar.shape, F32)

    x = x_ref[...]
    mod = mod_ref[0]
    h = _modulated_norm(x, g_ref[...], mod).astype(BF16)
    xb = _dot(h, win_ref[:, 0:LRU_WIDTH])
    gg = _dot(h, win_ref[:, LRU_WIDTH:2 * LRU_WIDTH])
    xpad[SUBLANES:SUBLANES + TL] = xb
    xc = cw_ref[CONV_W - 1:CONV_W] * xb + cb_ref[...]
    for j in range(CONV_W - 1):
        off = SUBLANES - (CONV_W - 1) + j
        xc = xc + cw_ref[j:j + 1] * xpad[off:off + TL]
    tail = xpad[TL + SUBLANES - (CONV_W - 1):TL + SUBLANES]
    conv_ref[0] = tail
    xpad[SUBLANES - (CONV_W - 1):SUBLANES] = tail

    a, b = _lru_gates(xc, wa_ref, ba_ref, wx_ref, bx_ref, lam_ref)
    a_scr[...] = a
    b_scr[...] = b
    row = lax.broadcasted_iota(I32, (SUBLANES, LRU_WIDTH), 0)

    def group(j, hprev):
        r0 = pl.multiple_of(j * SUBLANES, SUBLANES)
        aa = a_scr[pl.ds(r0, SUBLANES), :]
        bb = b_scr[pl.ds(r0, SUBLANES), :]
        d = 1
        while d < SUBLANES:
            a_sh = pltpu.roll(aa, d, axis=0)
            b_sh = pltpu.roll(bb, d, axis=0)
            m = row >= d
            bb = jnp.where(m, aa * b_sh + bb, bb)
            aa = jnp.where(m, aa * a_sh, aa)
            d *= 2
        hh = aa * hprev + bb
        h_scr[pl.ds(r0, SUBLANES), :] = hh
        return jnp.broadcast_to(hh[SUBLANES - 1:SUBLANES, :], (SUBLANES, LRU_WIDTH))

    hlast = lax.fori_loop(0, TL // SUBLANES, group, hcar[...], unroll=2)
    hcar[...] = hlast
    hl_ref[0] = hlast[0:1]
    hs = h_scr[...]
    out = _dot((hs * _silu(gg)).astype(BF16), wout_ref[...])
    y_ref[...] = x + mod[:, 2 * D_MODEL:3 * D_MODEL] * out


def _lru_prompt(x, mod, g, win, cw, cb, wa, ba, wx, bx, lam, wout, *, batch, seq):
    nb = seq // TL
    blk = lambda b, i: (b * nb + i, 0)
    const2 = lambda b, i: (0, 0)
    const3 = lambda b, i: (0, 0, 0)
    per_b = lambda b, i: (b, 0, 0)
    return pl.pallas_call(
        _lru_prompt_body,
        grid=(batch, nb),
        in_specs=[
            pl.BlockSpec((TL, D_MODEL), blk),
            pl.BlockSpec((1, 1, 3 * D_MODEL), per_b),
            pl.BlockSpec((1, D_MODEL), const2),
            pl.BlockSpec((D_MODEL, 2 * LRU_WIDTH), const2),
            pl.BlockSpec((CONV_W, LRU_WIDTH), const2),
            pl.BlockSpec((1, LRU_WIDTH), const2),
            pl.BlockSpec((N_LRU_BLOCKS, LRU_BLOCK, LRU_BLOCK), const3),
            pl.BlockSpec((1, LRU_WIDTH), const2),
            pl.BlockSpec((N_LRU_BLOCKS, LRU_BLOCK, LRU_BLOCK), const3),
            pl.BlockSpec((1, LRU_WIDTH), const2),
            pl.BlockSpec((1, LRU_WIDTH), const2),
            pl.BlockSpec((LRU_WIDTH, D_MODEL), const2),
        ],
        out_specs=[
            pl.BlockSpec((TL, D_MODEL), blk),
            pl.BlockSpec((1, CONV_W - 1, LRU_WIDTH), per_b),
            pl.BlockSpec((1, 1, LRU_WIDTH), per_b),
        ],
        out_shape=[
            jax.ShapeDtypeStruct((batch * seq, D_MODEL), F32),
            jax.ShapeDtypeStruct((batch, CONV_W - 1, LRU_WIDTH), F32),
            jax.ShapeDtypeStruct((batch, 1, LRU_WIDTH), F32),
        ],
        scratch_shapes=[
            pltpu.VMEM((TL + SUBLANES, LRU_WIDTH), F32),
            pltpu.VMEM((TL, LRU_WIDTH), F32),
            pltpu.VMEM((TL, LRU_WIDTH), F32),
            pltpu.VMEM((TL, LRU_WIDTH), F32),
            pltpu.VMEM((SUBLANES, LRU_WIDTH), F32),
        ],
        compiler_params=pltpu.CompilerParams(
            dimension_semantics=("arbitrary", "arbitrary"), vmem_limit_bytes=VMEM_LIMIT),
        name="lru_prompt",
    )(x, mod, g, win, cw, cb, wa, ba, wx, bx, lam, wout)


def _lru_sample_body(o_ref, gs_ref, x_ref, mod0_ref, wout0_ref, mod_ref, g_ref, win_ref, cw_ref, cb_ref,
                     wa_ref, ba_ref, wx_ref, bx_ref, lam_ref, wout_ref, sc_ref, sh_ref,
                     y_ref, conv_ref, hl_ref, *, nb, nt):
    x0 = x_ref[...]
    y0 = _dot((o_ref[...] * gs_ref[...].astype(F32)).astype(BF16), wout0_ref[...])
    x = x0 + mod0_ref[:, 2 * D_MODEL:3 * D_MODEL] * y0
    mod = mod_ref[...]
    h = _modulated_norm(x, g_ref[...], mod).astype(BF16)
    xb = _dot(h, win_ref[:, 0:LRU_WIDTH])
    gg = _dot(h, win_ref[:, LRU_WIDTH:2 * LRU_WIDTH])
    slabs = [sc_ref[j] for j in range(CONV_W - 1)] + [xb[t * nb:(t + 1) * nb] for t in range(nt)]
    xcs = []
    for t in range(nt):
        acc = cb_ref[...] + cw_ref[0:1] * slabs[t]
        for j in range(1, CONV_W):
            acc = acc + cw_ref[j:j + 1] * slabs[t + j]
        xcs.append(acc)
    for j in range(CONV_W - 1):
        conv_ref[j] = slabs[nt + j]
    xc = jnp.concatenate(xcs, axis=0)
    a, b = _lru_gates(xc, wa_ref, ba_ref, wx_ref, bx_ref, lam_ref)
    hprev = sh_ref[...]
    hs = []
    for t in range(nt):
        hprev = a[t * nb:(t + 1) * nb] * hprev + b[t * nb:(t + 1) * nb]
        hs.append(hprev)
    hl_ref[...] = hprev
    out = _dot((jnp.concatenate(hs, axis=0) * _silu(gg)).astype(BF16), wout_ref[...])
    y_ref[...] = x + mod[:, 2 * D_MODEL:3 * D_MODEL] * out


def _lru_sample(o, gs, x, mod0, wout0, mod, g, win, cw, cb, wa, ba, wx, bx, lam, wout, sc, sh, *, nb, nt):
    args = (o, gs, x, mod0, wout0, mod, g, win, cw, cb, wa, ba, wx, bx, lam, wout, sc, sh)

    def full(a):
        nd = a.ndim
        return pl.BlockSpec(a.shape, lambda i, nd=nd: (0,) * nd)

    out_shape = [
        jax.ShapeDtypeStruct((nt * nb, D_MODEL), F32),
        jax.ShapeDtypeStruct((CONV_W - 1, nb, LRU_WIDTH), F32),
        jax.ShapeDtypeStruct((nb, LRU_WIDTH), F32),
    ]
    body = functools.partial(_lru_sample_body, nb=nb, nt=nt)
    return pl.pallas_call(
        body,
        grid=(1,),
        in_specs=[full(a) for a in args],
        out_specs=[pl.BlockSpec(s.shape, lambda i, nd=len(s.shape): (0,) * nd) for s in out_shape],
        out_shape=out_shape,
        compiler_params=pltpu.CompilerParams(
            dimension_semantics=("arbitrary",), vmem_limit_bytes=VMEM_LIMIT),
        name="lru_sample",
    )(*args)


def kernel(x_prompt, x_sample, cache_k, cache_v, cache_idx_k, state_conv, state_h, page_table, c_prompt, c_sample, norm_g, ada_w, ada_b, attn_w_in, attn_q_norm, attn_k_norm, attn_w_out, lru_w_in, lru_conv_w, lru_conv_b, lru_w_a, lru_b_a, lru_w_x, lru_b_x, lru_lam, lru_w_out):
    B, S, _ = x_prompt.shape
    Bd, T, _ = x_sample.shape
    n_pages = page_table.shape[1]
    past = n_pages * PAGE_SIZE
    n_pool = cache_k.shape[1]
    assert S % QB == 0 and S % TL == 0 and n_pages % PAGES_PER_STEP == 0
    assert Bd % SEQ_PER_STEP == 0 and SEQ_PER_STEP * T == SUBLANES

    mod = _ada(jnp.concatenate([c_prompt, c_sample], axis=0), ada_w, ada_b)
    mod_p = [mod[l, :B].reshape(B, 1, 3 * D_MODEL) for l in range(2)]
    mod_s = [jnp.repeat(mod[l, B:], T, axis=0) for l in range(2)]

    w = attn_w_in[0]
    o_k, o_v, o_qi, o_ki, o_wi, o_g = 1024, 1280, 1536, 2560, 2624, 2640
    wcat = jnp.concatenate(
        [w[:, :o_ki], w[:, o_g:], w[:, o_ki:o_g], jnp.zeros((D_MODEL, PROJ_WIDTH - w.shape[1]), F32)],
        axis=1).astype(BF16)
    g0 = norm_g[0].reshape(1, D_MODEL)
    qn = attn_q_norm[0].reshape(1, HEAD_DIM)
    kn = attn_k_norm[0].reshape(1, HEAD_DIM)
    wout0 = attn_w_out[0].astype(BF16)

    tm = 512
    tabs_p = _rope_tables(jnp.arange(S))
    xp = x_prompt.reshape(B * S, D_MODEL)
    q, k, v, qi, kw, ki, gs = _proj(xp, mod_p[0], g0, wcat, qn, kn, tabs_p,
                                    tm=tm, rows_per_mod=S, tab_blocks=S // tm)
    x1p = _attn_prompt(q, qi, kw, k, v, gs, xp, mod_p[0], wout0, batch=B, seq=S)

    pos_s = past + jnp.arange(T)
    tabs_s = jnp.tile(_rope_tables(pos_s), (1, Bd, 1))
    xs = x_sample.reshape(Bd * T, D_MODEL)
    q2, k2, v2, qi2, kw2, ki2, gs2 = _proj(xs, mod_s[0].reshape(1, Bd * T, 3 * D_MODEL), g0, wcat, qn, kn,
                                           tabs_s, tm=Bd * T, rows_per_mod=Bd * T, tab_blocks=1)
    qir = qi2.reshape(Bd, T, N_IDX_HEADS, IDX_DIM).transpose(0, 2, 1, 3).reshape(Bd, N_IDX_HEADS * T, IDX_DIM)
    wcol = kw2[:, IDX_DIM:IDX_DIM + N_IDX_HEADS].reshape(Bd, T, N_IDX_HEADS).transpose(0, 2, 1)
    wcol = wcol.reshape(Bd, N_IDX_HEADS * T, 1)
    pad_rows = lambda a: jnp.pad(a, ((0, 0), (0, PAGE_SIZE - T)) + ((0, 0),) * (a.ndim - 2))
    ki_new = pad_rows(ki2.reshape(Bd, T, IDX_DIM)).transpose(0, 2, 1)
    cache_ik_t = cache_idx_k.reshape(n_pool, PAGE_SIZE, IDX_DIM).transpose(0, 2, 1)
    scores = _sample_scores(page_table, qir, wcol, cache_ik_t, ki_new,
                            dec_batch=Bd, dec_seq=T, n_pages=n_pages)
    sel = _sample_select(scores, past=past, dec_seq=T)
    qs = q2.reshape(Bd, T, N_KV_HEADS, GROUP, HEAD_DIM).transpose(0, 2, 3, 1, 4)
    qs = qs.reshape(Bd, N_KV_HEADS, GROUP * T, HEAD_DIM)
    k_new = pad_rows(k2.reshape(Bd, T, N_KV_HEADS, HEAD_DIM)).transpose(0, 2, 1, 3)
    v_new = pad_rows(v2.reshape(Bd, T, N_KV_HEADS, HEAD_DIM)).transpose(0, 2, 1, 3)
    o2 = _sample_attn(page_table, qs, sel,
                      cache_k.reshape(n_pool, PAGE_SIZE * N_KV_HEADS, HEAD_DIM),
                      cache_v.reshape(n_pool, PAGE_SIZE * N_KV_HEADS, HEAD_DIM),
                      k_new, v_new, dec_batch=Bd, dec_seq=T, n_pages=n_pages)
    o2 = o2.reshape(Bd, N_KV_HEADS, GROUP, T, HEAD_DIM).transpose(3, 0, 1, 2, 4).reshape(T * Bd, ATTN_WIDTH)

    g1 = norm_g[1].reshape(1, D_MODEL)
    win = lru_w_in[0].astype(BF16)
    cw = lru_conv_w[0]
    cb = lru_conv_b[0].reshape(1, LRU_WIDTH)
    wa = lru_w_a[0].astype(BF16)
    wx = lru_w_x[0].astype(BF16)
    ba = lru_b_a[0].reshape(1, LRU_WIDTH)
    bx = lru_b_x[0].reshape(1, LRU_WIDTH)
    lam = lru_lam[0].reshape(1, LRU_WIDTH)
    wout1 = lru_w_out[0].astype(BF16)

    yp, conv_p, h_p = _lru_prompt(x1p, mod_p[1], g1, win, cw, cb, wa, ba, wx, bx, lam, wout1, batch=B, seq=S)

    tmaj = lambda a: a.reshape(Bd, T, -1).transpose(1, 0, 2).reshape(T * Bd, -1)
    ys, conv_s, h_s = _lru_sample(
        o2, tmaj(gs2), tmaj(xs), tmaj(mod_s[0]), wout0, tmaj(mod_s[1]), g1, win, cw, cb, wa, ba, wx, bx,
        lam, wout1, state_conv[0].transpose(1, 0, 2), state_h[0], nb=Bd, nt=T)

    y_prompt = yp.reshape(B, S, D_MODEL)
    y_sample = ys.reshape(T, Bd, D_MODEL).transpose(1, 0, 2)
    return (y_prompt, y_sample,
            k.reshape(1, B, S, N_KV_HEADS, HEAD_DIM), v.reshape(1, B, S, N_KV_HEADS, HEAD_DIM),
            ki.reshape(1, B, S, IDX_DIM),
            k2.reshape(1, Bd, T, N_KV_HEADS, HEAD_DIM), v2.reshape(1, Bd, T, N_KV_HEADS, HEAD_DIM),
            ki2.reshape(1, Bd, T, IDX_DIM),
            conv_p.reshape(1, B, CONV_W - 1, LRU_WIDTH), h_p.reshape(1, B, LRU_WIDTH),
            conv_s.transpose(1, 0, 2).reshape(1, Bd, CONV_W - 1, LRU_WIDTH), h_s.reshape(1, Bd, LRU_WIDTH))
```

```python
import functools
import math

import jax
import jax.numpy as jnp
from jax import lax
from jax.experimental import pallas as pl
from jax.experimental.pallas import tpu as pltpu

F32 = jnp.float32
BF16 = jnp.bfloat16
I32 = jnp.int32

D_MODEL = 1024
N_HEADS = 8
N_KV_HEADS = 2
HEAD_DIM = 128
GROUP = N_HEADS // N_KV_HEADS
ATTN_WIDTH = N_HEADS * HEAD_DIM
KV_WIDTH = N_KV_HEADS * HEAD_DIM
N_IDX_HEADS = 16
IDX_DIM = 64
IDX_WIDTH = N_IDX_HEADS * IDX_DIM
TOPK_MAX = 256
ROPE_THETA = 500000.0
ROT_FRAC = 4
PAGE_SIZE = 128
LRU_WIDTH = D_MODEL
N_LRU_BLOCKS = 4
LRU_BLOCK = LRU_WIDTH // N_LRU_BLOCKS
CONV_W = 4
LRU_C = 8.0
EPS = 1e-6

LANES = 128
SUBLANES = 8
MXU_COLS = 256
VMEM_LIMIT = 56 * 1024 * 1024

OFF_Q = 0
OFF_K = OFF_Q + ATTN_WIDTH
OFF_V = OFF_K + KV_WIDTH
OFF_QI = OFF_V + KV_WIDTH
OFF_G = OFF_QI + IDX_WIDTH
OFF_KW = OFF_G + ATTN_WIDTH
PROJ_WIDTH = OFF_KW + LANES

INT_MIN = -(2 ** 31)
KEY_NEG_INF = 0x807FFFFF - 2 ** 32
RADIX_UNROLL = 2
COUNT_SLAB_ROWS = 64
LOG2E = 1.4426950408889634
NT_DIMS = (((1,), (1,)), ((), ()))


def _dot(a, b):
    return jnp.dot(a, b, preferred_element_type=F32)


def _dot_nt(a, b):
    return lax.dot_general(a, b, NT_DIMS, preferred_element_type=F32)


def _silu(x):
    return x / (1.0 + jnp.exp(-x))


def _sigmoid(x):
    return 1.0 / (1.0 + jnp.exp(-x))


def _rmsnorm(x, g):
    return x * lax.rsqrt(jnp.mean(x * x, axis=-1, keepdims=True) + EPS) * g


def _modulated_norm(x, g, mod):
    shift = mod[:, 0:D_MODEL]
    scale = mod[:, D_MODEL:2 * D_MODEL]
    return _rmsnorm(x, g) * (1.0 + scale) + shift


def _rope(y, tabs_ref, base, half):
    c = tabs_ref[base]
    s1 = tabs_ref[base + 1]
    s2 = tabs_ref[base + 2]
    return y * c + pltpu.roll(y, LANES - half, axis=1) * s1 + pltpu.roll(y, half, axis=1) * s2


def _key_to_f32(key):
    bits = jnp.where(key >= 0, key, key ^ 0x7FFFFFFF)
    return pltpu.bitcast(bits, F32)


def _kth_largest_key(sc_ref, n, k, axis, window=None, unroll=RADIX_UNROLL):
    kf = float(k)
    shape = list(sc_ref.shape)
    if window is None:
        window = slice(0, shape[1 - axis])
    shape[1 - axis] = window.stop - window.start
    shape[axis] = 1

    slab = COUNT_SLAB_ROWS if axis == 0 else LANES
    assert n % slab == 0

    def body(it, prefix):
        cand = prefix + lax.shift_left(jnp.int32(1), 31 - it)
        cand_f = _key_to_f32(cand)
        parts = []
        for j in range(n // slab):
            span = slice(j * slab, (j + 1) * slab)
            sc = sc_ref[span, window] if axis == 0 else sc_ref[window, span]
            parts.append(jnp.where(sc >= cand_f, 1.0, 0.0))
        while len(parts) > 1:
            nxt = [parts[a] + parts[a + 1] for a in range(0, len(parts) - 1, 2)]
            if len(parts) % 2:
                nxt.append(parts[-1])
            parts = nxt
        cnt = jnp.sum(parts[0], axis=axis, keepdims=True)
        return jnp.where(cnt >= kf, cand, prefix)

    return lax.fori_loop(0, 32, body, jnp.full(tuple(shape), INT_MIN, I32), unroll=unroll)


def _ada_body(c_ref, w_ref, b_ref, o_ref):
    s = _silu(c_ref[...]).astype(BF16)
    o_ref[0] = _dot(s, w_ref[0].astype(BF16)) + b_ref[0]


def _ada(c_all, ada_w, ada_b):
    rows = c_all.shape[0]
    depth = ada_w.shape[0]
    nblk = 3
    return pl.pallas_call(
        _ada_body,
        grid=(depth, nblk),
        in_specs=[
            pl.BlockSpec((rows, D_MODEL), lambda l, j: (0, 0)),
            pl.BlockSpec((1, D_MODEL, D_MODEL), lambda l, j: (l, 0, j)),
            pl.BlockSpec((1, 1, D_MODEL), lambda l, j: (l, 0, j)),
        ],
        out_specs=pl.BlockSpec((1, rows, D_MODEL), lambda l, j: (l, 0, j)),
        out_shape=jax.ShapeDtypeStruct((depth, rows, 3 * D_MODEL), F32),
        compiler_params=pltpu.CompilerParams(
            dimension_semantics=("arbitrary", "arbitrary"), vmem_limit_bytes=VMEM_LIMIT),
        name="ada_mod",
    )(c_all, ada_w, ada_b.reshape(depth, 1, 3 * D_MODEL))


def _proj_body(x_ref, mod_ref, g_ref, w_ref, qn_ref, kn_ref, tabs_ref,
               q_ref, k_ref, v_ref, qi_ref, kw_ref, ki_ref, gs_ref):
    h = _modulated_norm(x_ref[...], g_ref[...], mod_ref[0]).astype(BF16)
    qn = qn_ref[...]
    kn = kn_ref[...]
    half_h = HEAD_DIM // ROT_FRAC // 2
    half_i = IDX_DIM // ROT_FRAC // 2

    def slabs(off, n_slabs):
        for c0 in range(0, n_slabs, MXU_COLS // LANES):
            z = _dot(h, w_ref[:, off + c0 * LANES:off + c0 * LANES + MXU_COLS])
            for e in range(MXU_COLS // LANES):
                yield c0 + e, z[:, e * LANES:(e + 1) * LANES]

    for hh, z in slabs(OFF_Q, N_HEADS):
        q_ref[:, hh * HEAD_DIM:(hh + 1) * HEAD_DIM] = _rope(_rmsnorm(z, qn), tabs_ref, 0, half_h).astype(BF16)
    tm = x_ref.shape[0]
    for hh, z in slabs(OFF_K, N_KV_HEADS):
        k_ref[pl.ds(hh, tm, stride=N_KV_HEADS), :] = _rope(_rmsnorm(z, kn), tabs_ref, 0, half_h)
    for hh, z in slabs(OFF_V, N_KV_HEADS):
        v_ref[pl.ds(hh, tm, stride=N_KV_HEADS), :] = z
    for c, z in slabs(OFF_QI, IDX_WIDTH // LANES):
        qi_ref[:, c * LANES:(c + 1) * LANES] = _rope(z, tabs_ref, 3, half_i).astype(BF16)
    for c, z in slabs(OFF_G, ATTN_WIDTH // LANES):
        gs_ref[:, c * LANES:(c + 1) * LANES] = _silu(z).astype(BF16)
    kw = _rope(_dot(h, w_ref[:, OFF_KW:OFF_KW + LANES]), tabs_ref, 6, half_i)
    kw_ref[...] = kw
    ki_ref[...] = kw[:, 0:IDX_DIM]


def _proj(x, mod, g, w, qn, kn, tabs, *, tm, rows_per_mod, tab_blocks):
    n = x.shape[0]
    mod_rows = mod.shape[1]
    grid = (n // tm,)
    row = lambda i: (i, 0)
    outs = [
        jax.ShapeDtypeStruct((n, ATTN_WIDTH), BF16),
        jax.ShapeDtypeStruct((n * N_KV_HEADS, HEAD_DIM), F32),
        jax.ShapeDtypeStruct((n * N_KV_HEADS, HEAD_DIM), F32),
        jax.ShapeDtypeStruct((n, IDX_WIDTH), BF16),
        jax.ShapeDtypeStruct((n, LANES), F32),
        jax.ShapeDtypeStruct((n, IDX_DIM), F32),
        jax.ShapeDtypeStruct((n, ATTN_WIDTH), BF16),
    ]
    return pl.pallas_call(
        _proj_body,
        grid=grid,
        in_specs=[
            pl.BlockSpec((tm, D_MODEL), row),
            pl.BlockSpec((1, mod_rows, 3 * D_MODEL), lambda i: (i * tm // rows_per_mod, 0, 0)),
            pl.BlockSpec((1, D_MODEL), lambda i: (0, 0)),
            pl.BlockSpec((D_MODEL, PROJ_WIDTH), lambda i: (0, 0)),
            pl.BlockSpec((1, HEAD_DIM), lambda i: (0, 0)),
            pl.BlockSpec((1, HEAD_DIM), lambda i: (0, 0)),
            pl.BlockSpec((9, tm, LANES), lambda i: (0, i % tab_blocks, 0)),
        ],
        out_specs=[
            pl.BlockSpec((tm, ATTN_WIDTH), row),
            pl.BlockSpec((tm * N_KV_HEADS, HEAD_DIM), row),
            pl.BlockSpec((tm * N_KV_HEADS, HEAD_DIM), row),
            pl.BlockSpec((tm, IDX_WIDTH), row),
            pl.BlockSpec((tm, LANES), row),
            pl.BlockSpec((tm, IDX_DIM), row),
            pl.BlockSpec((tm, ATTN_WIDTH), row),
        ],
        out_shape=outs,
        compiler_params=pltpu.CompilerParams(
            dimension_semantics=("arbitrary",), vmem_limit_bytes=VMEM_LIMIT),
        name="attn_proj",
    )(x, mod, g, w, qn, kn, tabs)


def _rope_tables(pos):
    posf = pos.astype(F32)
    t = pos.shape[0]

    def base(d):
        r = d // ROT_FRAC
        half = r // 2
        inv = jnp.exp(-jnp.log(jnp.asarray(ROPE_THETA, F32)) * jnp.arange(half, dtype=F32) * 2.0 / r)
        ang = posf[:, None] * inv[None, :]
        cos = jnp.cos(ang)
        sin = jnp.sin(ang)
        c = jnp.concatenate([cos, cos, jnp.ones((t, d - r), F32)], axis=1)
        s1 = jnp.concatenate([-sin, jnp.zeros((t, d - half), F32)], axis=1)
        s2 = jnp.concatenate([jnp.zeros((t, half), F32), sin, jnp.zeros((t, d - r), F32)], axis=1)
        return c, s1, s2

    hc, hs1, hs2 = base(HEAD_DIM)
    ic, is1, is2 = base(IDX_DIM)
    wi_scale = N_IDX_HEADS ** -0.5 * IDX_DIM ** -0.5
    pad = LANES - IDX_DIM
    kc = jnp.concatenate([ic, jnp.full((t, N_IDX_HEADS), wi_scale, F32),
                          jnp.zeros((t, pad - N_IDX_HEADS), F32)], axis=1)
    ks1 = jnp.concatenate([is1, jnp.zeros((t, pad), F32)], axis=1)
    ks2 = jnp.concatenate([is2, jnp.zeros((t, pad), F32)], axis=1)
    two = lambda a: jnp.concatenate([a, a], axis=1)
    return jnp.stack([hc, hs1, hs2, two(ic), two(is1), two(is2), kc, ks1, ks2])


QB = 128
SUB_BLOCKS = 1
KEY_BUCKET = SUB_BLOCKS * QB


def _attn_prompt_body(q_ref, qi_ref, kwb_ref, kws_ref, k_ref, v_ref, gs_ref, x_ref, gate_ref,
                      wout_ref, o_ref, kbf, vbf, kipar, sct_ref, *, seq, k_sel):
    i = pl.program_id(1)

    @pl.when(i == 0)
    def _():
        ones_col = jnp.where(lax.broadcasted_iota(I32, (seq, HEAD_DIM), 1) == 0, 1.0, 0.0).astype(BF16)
        for kh in range(N_KV_HEADS):
            kbf[kh] = k_ref[pl.ds(kh, seq, stride=N_KV_HEADS), :].astype(BF16)
            vbf[kh, :, 0:HEAD_DIM] = v_ref[pl.ds(kh, seq, stride=N_KV_HEADS), :].astype(BF16)
            vbf[kh, :, HEAD_DIM:2 * HEAD_DIM] = ones_col
        kw = kws_ref[...]
        lane = lax.broadcasted_iota(I32, kw.shape, 1)
        ke = jnp.where(lane < IDX_DIM, kw, 0.0)
        kipar[0] = ke.astype(BF16)
        kipar[1] = pltpu.roll(ke, IDX_DIM, axis=1).astype(BF16)

    def select(sb, nk):
        rows = slice(sb * QB, (sb + 1) * QB)
        first = (i * SUB_BLOCKS + sb) * QB
        col = lax.broadcasted_iota(I32, (QB, nk), 1)
        pos = first + lax.broadcasted_iota(I32, (QB, nk), 0)
        if nk <= k_sel:
            return jnp.where(col <= pos, 0.0, -jnp.inf)
        kwb = kwb_ref[rows, :]
        score = None
        for h in range(N_IDX_HEADS):
            p, par = divmod(h, 2)
            d = _dot_nt(qi_ref[rows, p * LANES:(p + 1) * LANES], kipar[par, 0:nk, :])
            term = kwb[:, IDX_DIM + h:IDX_DIM + h + 1] * jnp.maximum(d, 0.0)
            score = term if score is None else score + term
        score = jnp.where(col <= pos, score, -jnp.inf)
        sct_ref[0:nk, rows] = score.T
        thr = _kth_largest_key(sct_ref, nk, k_sel, 0, window=rows)
        key_row = lax.broadcasted_iota(I32, (nk, QB), 0)
        q_pos = first + lax.broadcasted_iota(I32, (nk, QB), 1)
        sel_t = jnp.logical_and(
            key_row <= q_pos,
            jnp.logical_or(sct_ref[0:nk, rows] >= _key_to_f32(thr), thr <= KEY_NEG_INF))
        return jnp.where(sel_t, 0.0, -jnp.inf).T

    def attend(sb, nk, bias):
        rows = slice(sb * QB, (sb + 1) * QB)
        c = HEAD_DIM ** -0.5 * LOG2E
        heads = [None] * N_HEADS
        for kh in range(N_KV_HEADS):
            qs = jnp.concatenate(
                [q_ref[rows, (kh * GROUP + g) * HEAD_DIM:(kh * GROUP + g + 1) * HEAD_DIM] for g in range(GROUP)],
                axis=0)
            s = _dot_nt(qs, kbf[kh, 0:nk, :])
            ps = []
            for g in range(GROUP):
                sg = s[g * QB:(g + 1) * QB] + bias
                ps.append(jnp.exp2((sg - jnp.max(sg, axis=1, keepdims=True)) * c).astype(BF16))
            o = _dot(jnp.concatenate(ps, axis=0), vbf[kh, 0:nk, :])
            for g in range(GROUP):
                og = o[g * QB:(g + 1) * QB]
                heads[kh * GROUP + g] = og[:, 0:HEAD_DIM] / og[:, HEAD_DIM:HEAD_DIM + 1]
        attn = jnp.concatenate(heads, axis=1)
        y = _dot((attn * gs_ref[rows, :].astype(F32)).astype(BF16), wout_ref[...])
        o_ref[rows, :] = x_ref[rows, :] + gate_ref[0] * y

    def block(nk):
        biases = [select(sb, nk) for sb in range(SUB_BLOCKS)]
        for sb in range(SUB_BLOCKS):
            attend(sb, nk, biases[sb])

    assert KEY_BUCKET == SUB_BLOCKS * QB
    for bucket in range(seq // KEY_BUCKET):
        pl.when(i == bucket)(functools.partial(block, (bucket + 1) * KEY_BUCKET))


def _attn_prompt(q, qi, kw, k, v, gs, x, mod, wout, *, batch, seq):
    step = SUB_BLOCKS * QB
    nb = seq // step
    k_sel = min(TOPK_MAX, seq // 4)
    blk = lambda b, i: (b * nb + i, 0)
    whole = lambda b, i: (b, 0)
    body = functools.partial(_attn_prompt_body, seq=seq, k_sel=k_sel)
    return pl.pallas_call(
        body,
        grid=(batch, nb),
        in_specs=[
            pl.BlockSpec((step, ATTN_WIDTH), blk),
            pl.BlockSpec((step, IDX_WIDTH), blk),
            pl.BlockSpec((step, LANES), blk),
            pl.BlockSpec((seq, LANES), whole),
            pl.BlockSpec((seq * N_KV_HEADS, HEAD_DIM), whole),
            pl.BlockSpec((seq * N_KV_HEADS, HEAD_DIM), whole),
            pl.BlockSpec((step, ATTN_WIDTH), blk),
            pl.BlockSpec((step, D_MODEL), blk),
            pl.BlockSpec((1, 1, D_MODEL), lambda b, i: (b, 0, 2)),
            pl.BlockSpec((ATTN_WIDTH, D_MODEL), lambda b, i: (0, 0)),
        ],
        out_specs=pl.BlockSpec((step, D_MODEL), blk),
        out_shape=jax.ShapeDtypeStruct((batch * seq, D_MODEL), F32),
        scratch_shapes=[
            pltpu.VMEM((N_KV_HEADS, seq, HEAD_DIM), BF16),
            pltpu.VMEM((N_KV_HEADS, seq, 2 * HEAD_DIM), BF16),
            pltpu.VMEM((2, seq, LANES), BF16),
            pltpu.VMEM((seq, step), F32),
        ],
        compiler_params=pltpu.CompilerParams(
            dimension_semantics=("arbitrary", "arbitrary"), vmem_limit_bytes=VMEM_LIMIT),
        name="attn_prompt",
    )(q, qi, kw, kw, k, v, gs, x, mod, wout)


PAGES_PER_STEP = 16
SEQ_PER_STEP = 2


def _sample_tile_score(qir, wcol, kpage_t):
    d = _dot(qir, kpage_t.astype(BF16))
    r = wcol * jnp.maximum(d, 0.0)
    acc = r[0:SUBLANES]
    for j in range(1, r.shape[0] // SUBLANES):
        acc = acc + r[j * SUBLANES:(j + 1) * SUBLANES]
    half = SUBLANES // 2
    return acc[0:half] + acc[half:SUBLANES]


def _sample_score_body(pt_ref, qir_ref, wcol_ref, *rest, n_chunks, dec_seq):
    npg = SEQ_PER_STEP * PAGES_PER_STEP
    pages = rest[:npg]
    knew_ref = rest[npg]
    o_ref = rest[npg + 1]
    c = pl.program_id(1)

    @pl.when(c < n_chunks)
    def _():
        for s in range(SEQ_PER_STEP):
            kcat = jnp.concatenate([pages[s * PAGES_PER_STEP + j][0] for j in range(PAGES_PER_STEP)], axis=1)
            o_ref[s * dec_seq:(s + 1) * dec_seq, :] = _sample_tile_score(qir_ref[s], wcol_ref[s], kcat)

    @pl.when(c == n_chunks)
    def _():
        o_ref[...] = jnp.zeros(o_ref.shape, F32)
        for s in range(SEQ_PER_STEP):
            o_ref[s * dec_seq:(s + 1) * dec_seq, 0:PAGE_SIZE] = _sample_tile_score(
                qir_ref[s], wcol_ref[s], knew_ref[s])


def _sample_scores(page_table, qir, wcol, cache_ik, knew, *, dec_batch, dec_seq, n_pages):
    n_chunks = n_pages // PAGES_PER_STEP
    width = (n_chunks + 1) * PAGES_PER_STEP * PAGE_SIZE
    rows = IDX_DIM

    def page_map(s, j):
        def f(bp, c, pt):
            cc = jnp.minimum(c, n_chunks - 1)
            return (pt[bp * SEQ_PER_STEP + s, cc * PAGES_PER_STEP + j], 0, 0)
        return f

    in_specs = [
        pl.BlockSpec((SEQ_PER_STEP, rows, IDX_DIM), lambda bp, c, pt: (bp, 0, 0)),
        pl.BlockSpec((SEQ_PER_STEP, rows, 1), lambda bp, c, pt: (bp, 0, 0)),
    ]
    for s in range(SEQ_PER_STEP):
        for j in range(PAGES_PER_STEP):
            in_specs.append(pl.BlockSpec((1, IDX_DIM, PAGE_SIZE), page_map(s, j)))
    in_specs.append(pl.BlockSpec((SEQ_PER_STEP, IDX_DIM, PAGE_SIZE), lambda bp, c, pt: (bp, 0, 0)))
    body = functools.partial(_sample_score_body, n_chunks=n_chunks, dec_seq=dec_seq)
    return pl.pallas_call(
        body,
        grid_spec=pltpu.PrefetchScalarGridSpec(
            num_scalar_prefetch=1,
            grid=(dec_batch // SEQ_PER_STEP, n_chunks + 1),
            in_specs=in_specs,
            out_specs=pl.BlockSpec((SEQ_PER_STEP * dec_seq, PAGES_PER_STEP * PAGE_SIZE),
                                   lambda bp, c, pt: (bp, c)),
        ),
        out_shape=jax.ShapeDtypeStruct((dec_batch * dec_seq, width), F32),
        compiler_params=pltpu.CompilerParams(
            dimension_semantics=("arbitrary", "arbitrary"), vmem_limit_bytes=VMEM_LIMIT),
        name="sample_scores",
    )(page_table, qir, wcol, *([cache_ik] * (SEQ_PER_STEP * PAGES_PER_STEP)), knew)


SELECT_ROWS = 32


def _sample_select_body(sc_ref, sel_ref, key_ref, *, past, dec_seq, k_sel):
    shape = sc_ref.shape
    col = lax.broadcasted_iota(I32, shape, 1)
    t = lax.broadcasted_iota(I32, shape, 0) % dec_seq
    adm = (col - past) <= t
    key_ref[...] = jnp.where(adm, sc_ref[...], -jnp.inf)
    thr = _kth_largest_key(key_ref, shape[1], k_sel, 1)
    picked = jnp.logical_or(key_ref[...] >= _key_to_f32(thr), thr <= KEY_NEG_INF)
    sel_ref[...] = jnp.where(jnp.logical_and(picked, adm), 1.0, 0.0)


def _sample_select(scores, *, past, dec_seq):
    k_sel = min(TOPK_MAX, (past + dec_seq) // 4)
    body = functools.partial(_sample_select_body, past=past, dec_seq=dec_seq, k_sel=k_sel)
    rows, width = scores.shape
    assert rows % SELECT_ROWS == 0 and SELECT_ROWS % dec_seq == 0
    return pl.pallas_call(
        body,
        grid=(rows // SELECT_ROWS,),
        in_specs=[pl.BlockSpec((SELECT_ROWS, width), lambda i: (i, 0))],
        out_specs=pl.BlockSpec((SELECT_ROWS, width), lambda i: (i, 0)),
        out_shape=jax.ShapeDtypeStruct(scores.shape, F32),
        scratch_shapes=[pltpu.VMEM((SELECT_ROWS, width), F32)],
        compiler_params=pltpu.CompilerParams(
            dimension_semantics=("arbitrary",), vmem_limit_bytes=VMEM_LIMIT),
        name="sample_select",
    )(scores)


def _sample_attn_body(pt_ref, q_ref, sel_ref, *rest, n_chunks, dec_seq):
    npg = SEQ_PER_STEP * PAGES_PER_STEP
    kpages = rest[:npg]
    vpages = rest[npg:2 * npg]
    knew_ref, vnew_ref, o_ref, m_ref, l_ref, acc_ref = rest[2 * npg:]
    c = pl.program_id(1)
    scale = HEAD_DIM ** -0.5

    @pl.when(c == 0)
    def _():
        m_ref[...] = jnp.full(m_ref.shape, -jnp.inf, F32)
        l_ref[...] = jnp.zeros(l_ref.shape, F32)
        acc_ref[...] = jnp.zeros(acc_ref.shape, F32)

    def update(tiles, finish):
        n = SEQ_PER_STEP * N_KV_HEADS
        m_old = [m_ref[idx] for idx in range(n)]
        l_old = [l_ref[idx] for idx in range(n)]
        a_old = [acc_ref[idx] for idx in range(n)]
        m_out, l_out, a_out = [], [], []
        for idx, (kt, vt, sel) in enumerate(tiles):
            s, kh = divmod(idx, N_KV_HEADS)
            sc = _dot_nt(q_ref[s, kh], kt) * scale
            selg = jnp.concatenate([sel] * GROUP, axis=0) > 0.5
            sc = jnp.where(selg, sc, -jnp.inf)
            m_new = jnp.maximum(m_old[idx], jnp.max(sc, axis=1, keepdims=True))
            m_safe = jnp.where(m_new == -jnp.inf, 0.0, m_new)
            alpha = jnp.exp(m_old[idx] - m_safe)
            p = jnp.exp(sc - m_safe)
            m_out.append(m_new)
            l_out.append(alpha * l_old[idx] + jnp.sum(p, axis=1, keepdims=True))
            a_out.append(alpha * a_old[idx] + _dot(p.astype(BF16), vt))
        for idx in range(n):
            if finish:
                s, kh = divmod(idx, N_KV_HEADS)
                o_ref[s, kh] = a_out[idx] / l_out[idx]
            else:
                m_ref[idx] = m_out[idx]
                l_ref[idx] = l_out[idx]
                acc_ref[idx] = a_out[idx]

    def head_rows(page_refs, s, kh):
        return jnp.concatenate(
            [page_refs[s * PAGES_PER_STEP + j][0, pl.ds(kh, PAGE_SIZE, stride=N_KV_HEADS), :].astype(BF16)
             for j in range(PAGES_PER_STEP)], axis=0)

    @pl.when(c < n_chunks)
    def _():
        tiles = []
        for s in range(SEQ_PER_STEP):
            sel = sel_ref[s * dec_seq:(s + 1) * dec_seq, :]
            for kh in range(N_KV_HEADS):
                tiles.append((head_rows(kpages, s, kh), head_rows(vpages, s, kh), sel))
        update(tiles, False)

    @pl.when(c == n_chunks)
    def _():
        tiles = []
        for s in range(SEQ_PER_STEP):
            sel = sel_ref[s * dec_seq:(s + 1) * dec_seq, 0:PAGE_SIZE]
            for kh in range(N_KV_HEADS):
                tiles.append((knew_ref[s, kh].astype(BF16), vnew_ref[s, kh].astype(BF16), sel))
        update(tiles, True)


def _sample_attn(page_table, qs, sel, cache_k2, cache_v2, knew, vnew, *, dec_batch, dec_seq, n_pages):
    n_chunks = n_pages // PAGES_PER_STEP
    rows = GROUP * dec_seq

    def page_map(s, j):
        def f(bp, c, pt):
            cc = jnp.minimum(c, n_chunks - 1)
            return (pt[bp * SEQ_PER_STEP + s, cc * PAGES_PER_STEP + j], 0, 0)
        return f

    page_specs = []
    for s in range(SEQ_PER_STEP):
        for j in range(PAGES_PER_STEP):
            page_specs.append(pl.BlockSpec((1, PAGE_SIZE * N_KV_HEADS, HEAD_DIM), page_map(s, j)))
    new_spec = pl.BlockSpec((SEQ_PER_STEP, N_KV_HEADS, PAGE_SIZE, HEAD_DIM), lambda bp, c, pt: (bp, 0, 0, 0))
    in_specs = [
        pl.BlockSpec((SEQ_PER_STEP, N_KV_HEADS, rows, HEAD_DIM), lambda bp, c, pt: (bp, 0, 0, 0)),
        pl.BlockSpec((SEQ_PER_STEP * dec_seq, PAGES_PER_STEP * PAGE_SIZE), lambda bp, c, pt: (bp, c)),
    ] + page_specs + page_specs + [new_spec, new_spec]
    npg = SEQ_PER_STEP * PAGES_PER_STEP
    nstate = SEQ_PER_STEP * N_KV_HEADS
    body = functools.partial(_sample_attn_body, n_chunks=n_chunks, dec_seq=dec_seq)
    return pl.pallas_call(
        body,
        grid_spec=pltpu.PrefetchScalarGridSpec(
            num_scalar_prefetch=1,
            grid=(dec_batch // SEQ_PER_STEP, n_chunks + 1),
            in_specs=in_specs,
            out_specs=pl.BlockSpec((SEQ_PER_STEP, N_KV_HEADS, rows, HEAD_DIM), lambda bp, c, pt: (bp, 0, 0, 0)),
            scratch_shapes=[
                pltpu.VMEM((nstate, rows, 1), F32),
                pltpu.VMEM((nstate, rows, 1), F32),
                pltpu.VMEM((nstate, rows, HEAD_DIM), F32),
            ],
        ),
        out_shape=jax.ShapeDtypeStruct((dec_batch, N_KV_HEADS, rows, HEAD_DIM), F32),
        compiler_params=pltpu.CompilerParams(
            dimension_semantics=("arbitrary", "arbitrary"), vmem_limit_bytes=VMEM_LIMIT),
        name="sample_attn",
    )(page_table, qs, sel, *([cache_k2] * npg), *([cache_v2] * npg), knew, vnew)


def _log_sigmoid(x):
    return jnp.minimum(x, 0.0) - jnp.log1p(jnp.exp(-jnp.abs(x)))


def _lru_gates(xc, wa_ref, ba_ref, wx_ref, bx_ref, lam_ref):
    xcb = xc.astype(BF16)
    ra = []
    ia = []
    for n in range(N_LRU_BLOCKS):
        blk = xcb[:, n * LRU_BLOCK:(n + 1) * LRU_BLOCK]
        ra.append(_dot(blk, wa_ref[n]))
        ia.append(_dot(blk, wx_ref[n]))
    r = _sigmoid(jnp.concatenate(ra, axis=1) + ba_ref[...])
    ig = _sigmoid(jnp.concatenate(ia, axis=1) + bx_ref[...])
    log_a = (LRU_C * r) * _log_sigmoid(lam_ref[...])
    a = jnp.exp(log_a)
    b = jnp.sqrt(-jnp.tanh(log_a) * (a * a + 1.0)) * (ig * xc)
    return a, b


TL = 512


def _lru_prompt_body(x_ref, mod_ref, g_ref, win_ref, cw_ref, cb_ref, wa_ref, ba_ref, wx_ref, bx_ref,
                     lam_ref, wout_ref, y_ref, conv_ref, hl_ref, xpad, a_scr, b_scr, h_scr, hcar):
    i = pl.program_id(1)

    @pl.when(i == 0)
    def _():
        xpad[0:SUBLANES] = jnp.zeros((SUBLANES, LRU_WIDTH), F32)
        hcar[...] = jnp.zeros(hcar.shape, F32)

    x = x_ref[...]
    mod = mod_ref[0]
    h = _modulated_norm(x, g_ref[...], mod).astype(BF16)
    xb = _dot(h, win_ref[:, 0:LRU_WIDTH])
    gg = _dot(h, win_ref[:, LRU_WIDTH:2 * LRU_WIDTH])
    xpad[SUBLANES:SUBLANES + TL] = xb
    xc = cw_ref[CONV_W - 1:CONV_W] * xb + cb_ref[...]
    for j in range(CONV_W - 1):
        off = SUBLANES - (CONV_W - 1) + j
        xc = xc + cw_ref[j:j + 1] * xpad[off:off + TL]
    tail = xpad[TL + SUBLANES - (CONV_W - 1):TL + SUBLANES]
    conv_ref[0] = tail
    xpad[SUBLANES - (CONV_W - 1):SUBLANES] = tail

    a, b = _lru_gates(xc, wa_ref, ba_ref, wx_ref, bx_ref, lam_ref)
    a_scr[...] = a
    b_scr[...] = b
    row = lax.broadcasted_iota(I32, (SUBLANES, LRU_WIDTH), 0)

    def group(j, hprev):
        r0 = pl.multiple_of(j * SUBLANES, SUBLANES)
        aa = a_scr[pl.ds(r0, SUBLANES), :]
        bb = b_scr[pl.ds(r0, SUBLANES), :]
        d = 1
        while d < SUBLANES:
            a_sh = pltpu.roll(aa, d, axis=0)
            b_sh = pltpu.roll(bb, d, axis=0)
            m = row >= d
            bb = jnp.where(m, aa * b_sh + bb, bb)
            aa = jnp.where(m, aa * a_sh, aa)
            d *= 2
        hh = aa * hprev + bb
        h_scr[pl.ds(r0, SUBLANES), :] = hh
        return jnp.broadcast_to(hh[SUBLANES - 1:SUBLANES, :], (SUBLANES, LRU_WIDTH))

    hlast = lax.fori_loop(0, TL // SUBLANES, group, hcar[...], unroll=2)
    hcar[...] = hlast
    hl_ref[0] = hlast[0:1]
    hs = h_scr[...]
    out = _dot((hs * _silu(gg)).astype(BF16), wout_ref[...])
    y_ref[...] = x + mod[:, 2 * D_MODEL:3 * D_MODEL] * out


def _lru_prompt(x, mod, g, win, cw, cb, wa, ba, wx, bx, lam, wout, *, batch, seq):
    nb = seq // TL
    blk = lambda b, i: (b * nb + i, 0)
    const2 = lambda b, i: (0, 0)
    const3 = lambda b, i: (0, 0, 0)
    per_b = lambda b, i: (b, 0, 0)
    return pl.pallas_call(
        _lru_prompt_body,
        grid=(batch, nb),
        in_specs=[
            pl.BlockSpec((TL, D_MODEL), blk),
            pl.BlockSpec((1, 1, 3 * D_MODEL), per_b),
            pl.BlockSpec((1, D_MODEL), const2),
            pl.BlockSpec((D_MODEL, 2 * LRU_WIDTH), const2),
            pl.BlockSpec((CONV_W, LRU_WIDTH), const2),
            pl.BlockSpec((1, LRU_WIDTH), const2),
            pl.BlockSpec((N_LRU_BLOCKS, LRU_BLOCK, LRU_BLOCK), const3),
            pl.BlockSpec((1, LRU_WIDTH), const2),
            pl.BlockSpec((N_LRU_BLOCKS, LRU_BLOCK, LRU_BLOCK), const3),
            pl.BlockSpec((1, LRU_WIDTH), const2),
            pl.BlockSpec((1, LRU_WIDTH), const2),
            pl.BlockSpec((LRU_WIDTH, D_MODEL), const2),
        ],
        out_specs=[
            pl.BlockSpec((TL, D_MODEL), blk),
            pl.BlockSpec((1, CONV_W - 1, LRU_WIDTH), per_b),
            pl.BlockSpec((1, 1, LRU_WIDTH), per_b),
        ],
        out_shape=[
            jax.ShapeDtypeStruct((batch * seq, D_MODEL), F32),
            jax.ShapeDtypeStruct((batch, CONV_W - 1, LRU_WIDTH), F32),
            jax.ShapeDtypeStruct((batch, 1, LRU_WIDTH), F32),
        ],
        scratch_shapes=[
            pltpu.VMEM((TL + SUBLANES, LRU_WIDTH), F32),
            pltpu.VMEM((TL, LRU_WIDTH), F32),
            pltpu.VMEM((TL, LRU_WIDTH), F32),
            pltpu.VMEM((TL, LRU_WIDTH), F32),
            pltpu.VMEM((SUBLANES, LRU_WIDTH), F32),
        ],
        compiler_params=pltpu.CompilerParams(
            dimension_semantics=("arbitrary", "arbitrary"), vmem_limit_bytes=VMEM_LIMIT),
        name="lru_prompt",
    )(x, mod, g, win, cw, cb, wa, ba, wx, bx, lam, wout)


def _lru_sample_body(o_ref, gs_ref, x_ref, mod0_ref, wout0_ref, mod_ref, g_ref, win_ref, cw_ref, cb_ref,
                     wa_ref, ba_ref, wx_ref, bx_ref, lam_ref, wout_ref, sc_ref, sh_ref,
                     y_ref, conv_ref, hl_ref, *, nb, nt):
    x0 = x_ref[...]
    y0 = _dot((o_ref[...] * gs_ref[...].astype(F32)).astype(BF16), wout0_ref[...])
    x = x0 + mod0_ref[:, 2 * D_MODEL:3 * D_MODEL] * y0
    mod = mod_ref[...]
    h = _modulated_norm(x, g_ref[...], mod).astype(BF16)
    xb = _dot(h, win_ref[:, 0:LRU_WIDTH])
    gg = _dot(h, win_ref[:, LRU_WIDTH:2 * LRU_WIDTH])
    slabs = [sc_ref[j] for j in range(CONV_W - 1)] + [xb[t * nb:(t + 1) * nb] for t in range(nt)]
    xcs = []
    for t in range(nt):
        acc = cb_ref[...] + cw_ref[0:1] * slabs[t]
        for j in range(1, CONV_W):
            acc = acc + cw_ref[j:j + 1] * slabs[t + j]
        xcs.append(acc)
    for j in range(CONV_W - 1):
        conv_ref[j] = slabs[nt + j]
    xc = jnp.concatenate(xcs, axis=0)
    a, b = _lru_gates(xc, wa_ref, ba_ref, wx_ref, bx_ref, lam_ref)
    hprev = sh_ref[...]
    hs = []
    for t in range(nt):
        hprev = a[t * nb:(t + 1) * nb] * hprev + b[t * nb:(t + 1) * nb]
        hs.append(hprev)
    hl_ref[...] = hprev
    out = _dot((jnp.concatenate(hs, axis=0) * _silu(gg)).astype(BF16), wout_ref[...])
    y_ref[...] = x + mod[:, 2 * D_MODEL:3 * D_MODEL] * out


def _lru_sample(o, gs, x, mod0, wout0, mod, g, win, cw, cb, wa, ba, wx, bx, lam, wout, sc, sh, *, nb, nt):
    args = (o, gs, x, mod0, wout0, mod, g, win, cw, cb, wa, ba, wx, bx, lam, wout, sc, sh)

    def full(a):
        nd = a.ndim
        return pl.BlockSpec(a.shape, lambda i, nd=nd: (0,) * nd)

    out_shape = [
        jax.ShapeDtypeStruct((nt * nb, D_MODEL), F32),
        jax.ShapeDtypeStruct((CONV_W - 1, nb, LRU_WIDTH), F32),
        jax.ShapeDtypeStruct((nb, LRU_WIDTH), F32),
    ]
    body = functools.partial(_lru_sample_body, nb=nb, nt=nt)
    return pl.pallas_call(
        body,
        grid=(1,),
        in_specs=[full(a) for a in args],
        out_specs=[pl.BlockSpec(s.shape, lambda i, nd=len(s.shape): (0,) * nd) for s in out_shape],
        out_shape=out_shape,
        compiler_params=pltpu.CompilerParams(
            dimension_semantics=("arbitrary",), vmem_limit_bytes=VMEM_LIMIT),
        name="lru_sample",
    )(*args)


def kernel(x_prompt, x_sample, cache_k, cache_v, cache_idx_k, state_conv, state_h, page_table, c_prompt, c_sample, norm_g, ada_w, ada_b, attn_w_in, attn_q_norm, attn_k_norm, attn_w_out, lru_w_in, lru_conv_w, lru_conv_b, lru_w_a, lru_b_a, lru_w_x, lru_b_x, lru_lam, lru_w_out):
    B, S, _ = x_prompt.shape
    Bd, T, _ = x_sample.shape
    n_pages = page_table.shape[1]
    past = n_pages * PAGE_SIZE
    n_pool = cache_k.shape[1]
    assert S % QB == 0 and S % TL == 0 and n_pages % PAGES_PER_STEP == 0
    assert Bd % SEQ_PER_STEP == 0 and SEQ_PER_STEP * T == SUBLANES

    mod = _ada(jnp.concatenate([c_prompt, c_sample], axis=0), ada_w, ada_b)
    mod_p = [mod[l, :B].reshape(B, 1, 3 * D_MODEL) for l in range(2)]
    mod_s = [jnp.repeat(mod[l, B:], T, axis=0) for l in range(2)]

    w = attn_w_in[0]
    o_k, o_v, o_qi, o_ki, o_wi, o_g = 1024, 1280, 1536, 2560, 2624, 2640
    wcat = jnp.concatenate(
        [w[:, :o_ki], w[:, o_g:], w[:, o_ki:o_g], jnp.zeros((D_MODEL, PROJ_WIDTH - w.shape[1]), F32)],
        axis=1).astype(BF16)
    g0 = norm_g[0].reshape(1, D_MODEL)
    qn = attn_q_norm[0].reshape(1, HEAD_DIM)
    kn = attn_k_norm[0].reshape(1, HEAD_DIM)
    wout0 = attn_w_out[0].astype(BF16)

    tm = 512
    tabs_p = _rope_tables(jnp.arange(S))
    xp = x_prompt.reshape(B * S, D_MODEL)
    q, k, v, qi, kw, ki, gs = _proj(xp, mod_p[0], g0, wcat, qn, kn, tabs_p,
                                    tm=tm, rows_per_mod=S, tab_blocks=S // tm)
    x1p = _attn_prompt(q, qi, kw, k, v, gs, xp, mod_p[0], wout0, batch=B, seq=S)

    pos_s = past + jnp.arange(T)
    tabs_s = jnp.tile(_rope_tables(pos_s), (1, Bd, 1))
    xs = x_sample.reshape(Bd * T, D_MODEL)
    q2, k2, v2, qi2, kw2, ki2, gs2 = _proj(xs, mod_s[0].reshape(1, Bd * T, 3 * D_MODEL), g0, wcat, qn, kn,
                                           tabs_s, tm=Bd * T, rows_per_mod=Bd * T, tab_blocks=1)
    qir = qi2.reshape(Bd, T, N_IDX_HEADS, IDX_DIM).transpose(0, 2, 1, 3).reshape(Bd, N_IDX_HEADS * T, IDX_DIM)
    wcol = kw2[:, IDX_DIM:IDX_DIM + N_IDX_HEADS].reshape(Bd, T, N_IDX_HEADS).transpose(0, 2, 1)
    wcol = wcol.reshape(Bd, N_IDX_HEADS * T, 1)
    pad_rows = lambda a: jnp.pad(a, ((0, 0), (0, PAGE_SIZE - T)) + ((0, 0),) * (a.ndim - 2))
    ki_new = pad_rows(ki2.reshape(Bd, T, IDX_DIM)).transpose(0, 2, 1)
    cache_ik_t = cache_idx_k.reshape(n_pool, PAGE_SIZE, IDX_DIM).transpose(0, 2, 1)
    scores = _sample_scores(page_table, qir, wcol, cache_ik_t, ki_new,
                            dec_batch=Bd, dec_seq=T, n_pages=n_pages)
    sel = _sample_select(scores, past=past, dec_seq=T)
    qs = q2.reshape(Bd, T, N_KV_HEADS, GROUP, HEAD_DIM).transpose(0, 2, 3, 1, 4)
    qs = qs.reshape(Bd, N_KV_HEADS, GROUP * T, HEAD_DIM)
    k_new = pad_rows(k2.reshape(Bd, T, N_KV_HEADS, HEAD_DIM)).transpose(0, 2, 1, 3)
    v_new = pad_rows(v2.reshape(Bd, T, N_KV_HEADS, HEAD_DIM)).transpose(0, 2, 1, 3)
    o2 = _sample_attn(page_table, qs, sel,
                      cache_k.reshape(n_pool, PAGE_SIZE * N_KV_HEADS, HEAD_DIM),
                      cache_v.reshape(n_pool, PAGE_SIZE * N_KV_HEADS, HEAD_DIM),
                      k_new, v_new, dec_batch=Bd, dec_seq=T, n_pages=n_pages)
    o2 = o2.reshape(Bd, N_KV_HEADS, GROUP, T, HEAD_DIM).transpose(3, 0, 1, 2, 4).reshape(T * Bd, ATTN_WIDTH)

    g1 = norm_g[1].reshape(1, D_MODEL)
    win = lru_w_in[0].astype(BF16)
    cw = lru_conv_w[0]
    cb = lru_conv_b[0].reshape(1, LRU_WIDTH)
    wa = lru_w_a[0].astype(BF16)
    wx = lru_w_x[0].astype(BF16)
    ba = lru_b_a[0].reshape(1, LRU_WIDTH)
    bx = lru_b_x[0].reshape(1, LRU_WIDTH)
    lam = lru_lam[0].reshape(1, LRU_WIDTH)
    wout1 = lru_w_out[0].astype(BF16)

    yp, conv_p, h_p = _lru_prompt(x1p, mod_p[1], g1, win, cw, cb, wa, ba, wx, bx, lam, wout1, batch=B, seq=S)

    tmaj = lambda a: a.reshape(Bd, T, -1).transpose(1, 0, 2).reshape(T * Bd, -1)
    ys, conv_s, h_s = _lru_sample(
        o2, tmaj(gs2), tmaj(xs), tmaj(mod_s[0]), wout0, tmaj(mod_s[1]), g1, win, cw, cb, wa, ba, wx, bx,
        lam, wout1, state_conv[0].transpose(1, 0, 2), state_h[0], nb=Bd, nt=T)

    y_prompt = yp.reshape(B, S, D_MODEL)
    y_sample = ys.reshape(T, Bd, D_MODEL).transpose(1, 0, 2)
    return (y_prompt, y_sample,
            k.reshape(1, B, S, N_KV_HEADS, HEAD_DIM), v.reshape(1, B, S, N_KV_HEADS, HEAD_DIM),
            ki.reshape(1, B, S, IDX_DIM),
            k2.reshape(1, Bd, T, N_KV_HEADS, HEAD_DIM), v2.reshape(1, Bd, T, N_KV_HEADS, HEAD_DIM),
            ki2.reshape(1, Bd, T, IDX_DIM),
            conv_p.reshape(1, B, CONV_W - 1, LRU_WIDTH), h_p.reshape(1, B, LRU_WIDTH),
            conv_s.transpose(1, 0, 2).reshape(1, Bd, CONV_W - 1, LRU_WIDTH), h_s.reshape(1, Bd, LRU_WIDTH))
```

```python
import functools
import math

import jax
import jax.numpy as jnp
from jax import lax
from jax.experimental import pallas as pl
from jax.experimental.pallas import tpu as pltpu

F32 = jnp.float32
BF16 = jnp.bfloat16
I32 = jnp.int32

D_MODEL = 1024
N_HEADS = 8
N_KV_HEADS = 2
HEAD_DIM = 128
GROUP = N_HEADS // N_KV_HEADS
ATTN_WIDTH = N_HEADS * HEAD_DIM
KV_WIDTH = N_KV_HEADS * HEAD_DIM
N_IDX_HEADS = 16
IDX_DIM = 64
IDX_WIDTH = N_IDX_HEADS * IDX_DIM
TOPK_MAX = 256
ROPE_THETA = 500000.0
ROT_FRAC = 4
PAGE_SIZE = 128
LRU_WIDTH = D_MODEL
N_LRU_BLOCKS = 4
LRU_BLOCK = LRU_WIDTH // N_LRU_BLOCKS
CONV_W = 4
LRU_C = 8.0
EPS = 1e-6

LANES = 128
SUBLANES = 8
MXU_COLS = 256
VMEM_LIMIT = 56 * 1024 * 1024

OFF_Q = 0
OFF_K = OFF_Q + ATTN_WIDTH
OFF_V = OFF_K + KV_WIDTH
OFF_QI = OFF_V + KV_WIDTH
OFF_G = OFF_QI + IDX_WIDTH
OFF_KW = OFF_G + ATTN_WIDTH
PROJ_WIDTH = OFF_KW + LANES

INT_MIN = -(2 ** 31)
KEY_NEG_INF = 0x807FFFFF - 2 ** 32
RADIX_UNROLL = 2
COUNT_SLAB_ROWS = 64
LOG2E = 1.4426950408889634
NT_DIMS = (((1,), (1,)), ((), ()))


def _dot(a, b):
    return jnp.dot(a, b, preferred_element_type=F32)


def _dot_nt(a, b):
    return lax.dot_general(a, b, NT_DIMS, preferred_element_type=F32)


def _silu(x):
    return x / (1.0 + jnp.exp(-x))


def _sigmoid(x):
    return 1.0 / (1.0 + jnp.exp(-x))


def _rmsnorm(x, g):
    return x * lax.rsqrt(jnp.mean(x * x, axis=-1, keepdims=True) + EPS) * g


def _modulated_norm(x, g, mod):
    shift = mod[:, 0:D_MODEL]
    scale = mod[:, D_MODEL:2 * D_MODEL]
    return _rmsnorm(x, g) * (1.0 + scale) + shift


def _rope(y, tabs_ref, base, half):
    c = tabs_ref[base]
    s1 = tabs_ref[base + 1]
    s2 = tabs_ref[base + 2]
    return y * c + pltpu.roll(y, LANES - half, axis=1) * s1 + pltpu.roll(y, half, axis=1) * s2


def _key_to_f32(key):
    bits = jnp.where(key >= 0, key, key ^ 0x7FFFFFFF)
    return pltpu.bitcast(bits, F32)


def _kth_largest_key(sc_ref, n, k, axis, window=None, unroll=RADIX_UNROLL):
    kf = float(k)
    shape = list(sc_ref.shape)
    if window is None:
        window = slice(0, shape[1 - axis])
    shape[1 - axis] = window.stop - window.start
    shape[axis] = 1

    slab = COUNT_SLAB_ROWS if axis == 0 else LANES
    assert n % slab == 0

    def body(it, prefix):
        cand = prefix + lax.shift_left(jnp.int32(1), 31 - it)
        cand_f = _key_to_f32(cand)
        parts = []
        for j in range(n // slab):
            span = slice(j * slab, (j + 1) * slab)
            sc = sc_ref[span, window] if axis == 0 else sc_ref[window, span]
            parts.append(jnp.where(sc >= cand_f, 1.0, 0.0))
        while len(parts) > 1:
            nxt = [parts[a] + parts[a + 1] for a in range(0, len(parts) - 1, 2)]
            if len(parts) % 2:
                nxt.append(parts[-1])
            parts = nxt
        cnt = jnp.sum(parts[0], axis=axis, keepdims=True)
        return jnp.where(cnt >= kf, cand, prefix)

    return lax.fori_loop(0, 32, body, jnp.full(tuple(shape), INT_MIN, I32), unroll=unroll)


def _ada_body(c_ref, w_ref, b_ref, o_ref):
    s = _silu(c_ref[...]).astype(BF16)
    o_ref[0] = _dot(s, w_ref[0].astype(BF16)) + b_ref[0]


def _ada(c_all, ada_w, ada_b):
    rows = c_all.shape[0]
    depth = ada_w.shape[0]
    nblk = 3
    return pl.pallas_call(
        _ada_body,
        grid=(depth, nblk),
        in_specs=[
            pl.BlockSpec((rows, D_MODEL), lambda l, j: (0, 0)),
            pl.BlockSpec((1, D_MODEL, D_MODEL), lambda l, j: (l, 0, j)),
            pl.BlockSpec((1, 1, D_MODEL), lambda l, j: (l, 0, j)),
        ],
        out_specs=pl.BlockSpec((1, rows, D_MODEL), lambda l, j: (l, 0, j)),
        out_shape=jax.ShapeDtypeStruct((depth, rows, 3 * D_MODEL), F32),
        compiler_params=pltpu.CompilerParams(
            dimension_semantics=("arbitrary", "arbitrary"), vmem_limit_bytes=VMEM_LIMIT),
        name="ada_mod",
    )(c_all, ada_w, ada_b.reshape(depth, 1, 3 * D_MODEL))


def _proj_body(x_ref, mod_ref, g_ref, w_ref, qn_ref, kn_ref, tabs_ref,
               q_ref, k_ref, v_ref, qi_ref, kw_ref, ki_ref, gs_ref):
    h = _modulated_norm(x_ref[...], g_ref[...], mod_ref[0]).astype(BF16)
    qn = qn_ref[...]
    kn = kn_ref[...]
    half_h = HEAD_DIM // ROT_FRAC // 2
    half_i = IDX_DIM // ROT_FRAC // 2

    def slabs(off, n_slabs):
        for c0 in range(0, n_slabs, MXU_COLS // LANES):
            z = _dot(h, w_ref[:, off + c0 * LANES:off + c0 * LANES + MXU_COLS])
            for e in range(MXU_COLS // LANES):
                yield c0 + e, z[:, e * LANES:(e + 1) * LANES]

    for hh, z in slabs(OFF_Q, N_HEADS):
        q_ref[:, hh * HEAD_DIM:(hh + 1) * HEAD_DIM] = _rope(_rmsnorm(z, qn), tabs_ref, 0, half_h).astype(BF16)
    tm = x_ref.shape[0]
    for hh, z in slabs(OFF_K, N_KV_HEADS):
        k_ref[pl.ds(hh, tm, stride=N_KV_HEADS), :] = _rope(_rmsnorm(z, kn), tabs_ref, 0, half_h)
    for hh, z in slabs(OFF_V, N_KV_HEADS):
        v_ref[pl.ds(hh, tm, stride=N_KV_HEADS), :] = z
    for c, z in slabs(OFF_QI, IDX_WIDTH // LANES):
        qi_ref[:, c * LANES:(c + 1) * LANES] = _rope(z, tabs_ref, 3, half_i).astype(BF16)
    for c, z in slabs(OFF_G, ATTN_WIDTH // LANES):
        gs_ref[:, c * LANES:(c + 1) * LANES] = _silu(z).astype(BF16)
    kw = _rope(_dot(h, w_ref[:, OFF_KW:OFF_KW + LANES]), tabs_ref, 6, half_i)
    kw_ref[...] = kw
    ki_ref[...] = kw[:, 0:IDX_DIM]


def _proj(x, mod, g, w, qn, kn, tabs, *, tm, rows_per_mod, tab_blocks):
    n = x.shape[0]
    mod_rows = mod.shape[1]
    grid = (n // tm,)
    row = lambda i: (i, 0)
    outs = [
        jax.ShapeDtypeStruct((n, ATTN_WIDTH), BF16),
        jax.ShapeDtypeStruct((n * N_KV_HEADS, HEAD_DIM), F32),
        jax.ShapeDtypeStruct((n * N_KV_HEADS, HEAD_DIM), F32),
        jax.ShapeDtypeStruct((n, IDX_WIDTH), BF16),
        jax.ShapeDtypeStruct((n, LANES), F32),
        jax.ShapeDtypeStruct((n, IDX_DIM), F32),
        jax.ShapeDtypeStruct((n, ATTN_WIDTH), BF16),
    ]
    return pl.pallas_call(
        _proj_body,
        grid=grid,
        in_specs=[
            pl.BlockSpec((tm, D_MODEL), row),
            pl.BlockSpec((1, mod_rows, 3 * D_MODEL), lambda i: (i * tm // rows_per_mod, 0, 0)),
            pl.BlockSpec((1, D_MODEL), lambda i: (0, 0)),
            pl.BlockSpec((D_MODEL, PROJ_WIDTH), lambda i: (0, 0)),
            pl.BlockSpec((1, HEAD_DIM), lambda i: (0, 0)),
            pl.BlockSpec((1, HEAD_DIM), lambda i: (0, 0)),
            pl.BlockSpec((9, tm, LANES), lambda i: (0, i % tab_blocks, 0)),
        ],
        out_specs=[
            pl.BlockSpec((tm, ATTN_WIDTH), row),
            pl.BlockSpec((tm * N_KV_HEADS, HEAD_DIM), row),
            pl.BlockSpec((tm * N_KV_HEADS, HEAD_DIM), row),
            pl.BlockSpec((tm, IDX_WIDTH), row),
            pl.BlockSpec((tm, LANES), row),
            pl.BlockSpec((tm, IDX_DIM), row),
            pl.BlockSpec((tm, ATTN_WIDTH), row),
        ],
        out_shape=outs,
        compiler_params=pltpu.CompilerParams(
            dimension_semantics=("arbitrary",), vmem_limit_bytes=VMEM_LIMIT),
        name="attn_proj",
    )(x, mod, g, w, qn, kn, tabs)


def _rope_tables(pos):
    posf = pos.astype(F32)
    t = pos.shape[0]

    def base(d):
        r = d // ROT_FRAC
        half = r // 2
        inv = jnp.exp(-jnp.log(jnp.asarray(ROPE_THETA, F32)) * jnp.arange(half, dtype=F32) * 2.0 / r)
        ang = posf[:, None] * inv[None, :]
        cos = jnp.cos(ang)
        sin = jnp.sin(ang)
        c = jnp.concatenate([cos, cos, jnp.ones((t, d - r), F32)], axis=1)
        s1 = jnp.concatenate([-sin, jnp.zeros((t, d - half), F32)], axis=1)
        s2 = jnp.concatenate([jnp.zeros((t, half), F32), sin, jnp.zeros((t, d - r), F32)], axis=1)
        return c, s1, s2

    hc, hs1, hs2 = base(HEAD_DIM)
    ic, is1, is2 = base(IDX_DIM)
    wi_scale = N_IDX_HEADS ** -0.5 * IDX_DIM ** -0.5
    pad = LANES - IDX_DIM
    kc = jnp.concatenate([ic, jnp.full((t, N_IDX_HEADS), wi_scale, F32),
                          jnp.zeros((t, pad - N_IDX_HEADS), F32)], axis=1)
    ks1 = jnp.concatenate([is1, jnp.zeros((t, pad), F32)], axis=1)
    ks2 = jnp.concatenate([is2, jnp.zeros((t, pad), F32)], axis=1)
    two = lambda a: jnp.concatenate([a, a], axis=1)
    return jnp.stack([hc, hs1, hs2, two(ic), two(is1), two(is2), kc, ks1, ks2])


QB = 128
SUB_BLOCKS = 1
KEY_BUCKET = 256


def _attn_prompt_body(q_ref, qi_ref, kwb_ref, kws_ref, k_ref, v_ref, gs_ref, x_ref, gate_ref,
                      wout_ref, o_ref, kbf, vbf, kipar, sct_ref, *, seq, k_sel):
    i = pl.program_id(1)

    @pl.when(i == 0)
    def _():
        ones_col = jnp.where(lax.broadcasted_iota(I32, (seq, HEAD_DIM), 1) == 0, 1.0, 0.0).astype(BF16)
        for kh in range(N_KV_HEADS):
            kbf[kh] = k_ref[pl.ds(kh, seq, stride=N_KV_HEADS), :].astype(BF16)
            vbf[kh, :, 0:HEAD_DIM] = v_ref[pl.ds(kh, seq, stride=N_KV_HEADS), :].astype(BF16)
            vbf[kh, :, HEAD_DIM:2 * HEAD_DIM] = ones_col
        kw = kws_ref[...]
        lane = lax.broadcasted_iota(I32, kw.shape, 1)
        ke = jnp.where(lane < IDX_DIM, kw, 0.0)
        kipar[0] = ke.astype(BF16)
        kipar[1] = pltpu.roll(ke, IDX_DIM, axis=1).astype(BF16)

    def select(sb, nk):
        rows = slice(sb * QB, (sb + 1) * QB)
        first = (i * SUB_BLOCKS + sb) * QB
        col = lax.broadcasted_iota(I32, (QB, nk), 1)
        pos = first + lax.broadcasted_iota(I32, (QB, nk), 0)
        if nk <= k_sel:
            return jnp.where(col <= pos, 0.0, -jnp.inf)
        kwb = kwb_ref[rows, :]
        score = None
        for p in range(N_IDX_HEADS // 2):
            pair = None
            for par in range(2):
                h = 2 * p + par
                d = _dot_nt(qi_ref[rows, p * LANES:(p + 1) * LANES], kipar[par, 0:nk, :])
                term = kwb[:, IDX_DIM + h:IDX_DIM + h + 1] * jnp.maximum(d, 0.0)
                pair = term if pair is None else pair + term
            score = pair if score is None else score + pair
        score = jnp.where(col <= pos, score, -jnp.inf)
        sct_ref[0:nk, rows] = score.T
        thr = _kth_largest_key(sct_ref, nk, k_sel, 0, window=rows)
        key_row = lax.broadcasted_iota(I32, (nk, QB), 0)
        q_pos = first + lax.broadcasted_iota(I32, (nk, QB), 1)
        sel_t = jnp.logical_and(
            key_row <= q_pos,
            jnp.logical_or(sct_ref[0:nk, rows] >= _key_to_f32(thr), thr <= KEY_NEG_INF))
        return jnp.where(sel_t, 0.0, -jnp.inf).T

    def attend(sb, nk, bias):
        rows = slice(sb * QB, (sb + 1) * QB)
        c = HEAD_DIM ** -0.5 * LOG2E
        heads = [None] * N_HEADS
        for kh in range(N_KV_HEADS):
            qs = jnp.concatenate(
                [q_ref[rows, (kh * GROUP + g) * HEAD_DIM:(kh * GROUP + g + 1) * HEAD_DIM] for g in range(GROUP)],
                axis=0)
            s = _dot_nt(qs, kbf[kh, 0:nk, :])
            ps = []
            for g in range(GROUP):
                sg = s[g * QB:(g + 1) * QB] + bias
                ps.append(jnp.exp2((sg - jnp.max(sg, axis=1, keepdims=True)) * c).astype(BF16))
            o = _dot(jnp.concatenate(ps, axis=0), vbf[kh, 0:nk, :])
            for g in range(GROUP):
                og = o[g * QB:(g + 1) * QB]
                heads[kh * GROUP + g] = og[:, 0:HEAD_DIM] / og[:, HEAD_DIM:HEAD_DIM + 1]
        attn = jnp.concatenate(heads, axis=1)
        y = _dot((attn * gs_ref[rows, :].astype(F32)).astype(BF16), wout_ref[...])
        o_ref[rows, :] = x_ref[rows, :] + gate_ref[0] * y

    def block(nk):
        biases = [select(sb, nk) for sb in range(SUB_BLOCKS)]
        for sb in range(SUB_BLOCKS):
            attend(sb, nk, biases[sb])

    steps_per_bucket = KEY_BUCKET // (SUB_BLOCKS * QB)
    for bucket in range(seq // KEY_BUCKET):
        pl.when(i // steps_per_bucket == bucket)(functools.partial(block, (bucket + 1) * KEY_BUCKET))


def _attn_prompt(q, qi, kw, k, v, gs, x, mod, wout, *, batch, seq):
    step = SUB_BLOCKS * QB
    nb = seq // step
    k_sel = min(TOPK_MAX, seq // 4)
    blk = lambda b, i: (b * nb + i, 0)
    whole = lambda b, i: (b, 0)
    body = functools.partial(_attn_prompt_body, seq=seq, k_sel=k_sel)
    return pl.pallas_call(
        body,
        grid=(batch, nb),
        in_specs=[
            pl.BlockSpec((step, ATTN_WIDTH), blk),
            pl.BlockSpec((step, IDX_WIDTH), blk),
            pl.BlockSpec((step, LANES), blk),
            pl.BlockSpec((seq, LANES), whole),
            pl.BlockSpec((seq * N_KV_HEADS, HEAD_DIM), whole),
            pl.BlockSpec((seq * N_KV_HEADS, HEAD_DIM), whole),
            pl.BlockSpec((step, ATTN_WIDTH), blk),
            pl.BlockSpec((step, D_MODEL), blk),
            pl.BlockSpec((1, 1, D_MODEL), lambda b, i: (b, 0, 2)),
            pl.BlockSpec((ATTN_WIDTH, D_MODEL), lambda b, i: (0, 0)),
        ],
        out_specs=pl.BlockSpec((step, D_MODEL), blk),
        out_shape=jax.ShapeDtypeStruct((batch * seq, D_MODEL), F32),
        scratch_shapes=[
            pltpu.VMEM((N_KV_HEADS, seq, HEAD_DIM), BF16),
            pltpu.VMEM((N_KV_HEADS, seq, 2 * HEAD_DIM), BF16),
            pltpu.VMEM((2, seq, LANES), BF16),
            pltpu.VMEM((seq, step), F32),
        ],
        compiler_params=pltpu.CompilerParams(
            dimension_semantics=("arbitrary", "arbitrary"), vmem_limit_bytes=VMEM_LIMIT),
        name="attn_prompt",
    )(q, qi, kw, kw, k, v, gs, x, mod, wout)


PAGES_PER_STEP = 16
SEQ_PER_STEP = 2


def _sample_tile_score(qir, wcol, kpage_t):
    d = _dot(qir, kpage_t.astype(BF16))
    r = wcol * jnp.maximum(d, 0.0)
    acc = r[0:SUBLANES]
    for j in range(1, r.shape[0] // SUBLANES):
        acc = acc + r[j * SUBLANES:(j + 1) * SUBLANES]
    half = SUBLANES // 2
    return acc[0:half] + acc[half:SUBLANES]


def _sample_score_body(pt_ref, qir_ref, wcol_ref, *rest, n_chunks, dec_seq):
    npg = SEQ_PER_STEP * PAGES_PER_STEP
    pages = rest[:npg]
    knew_ref = rest[npg]
    o_ref = rest[npg + 1]
    c = pl.program_id(1)

    @pl.when(c < n_chunks)
    def _():
        for s in range(SEQ_PER_STEP):
            kcat = jnp.concatenate([pages[s * PAGES_PER_STEP + j][0] for j in range(PAGES_PER_STEP)], axis=1)
            o_ref[s * dec_seq:(s + 1) * dec_seq, :] = _sample_tile_score(qir_ref[s], wcol_ref[s], kcat)

    @pl.when(c == n_chunks)
    def _():
        o_ref[...] = jnp.zeros(o_ref.shape, F32)
        for s in range(SEQ_PER_STEP):
            o_ref[s * dec_seq:(s + 1) * dec_seq, 0:PAGE_SIZE] = _sample_tile_score(
                qir_ref[s], wcol_ref[s], knew_ref[s])


def _sample_scores(page_table, qir, wcol, cache_ik, knew, *, dec_batch, dec_seq, n_pages):
    n_chunks = n_pages // PAGES_PER_STEP
    width = (n_chunks + 1) * PAGES_PER_STEP * PAGE_SIZE
    rows = IDX_DIM

    def page_map(s, j):
        def f(bp, c, pt):
            cc = jnp.minimum(c, n_chunks - 1)
            return (pt[bp * SEQ_PER_STEP + s, cc * PAGES_PER_STEP + j], 0, 0)
        return f

    in_specs = [
        pl.BlockSpec((SEQ_PER_STEP, rows, IDX_DIM), lambda bp, c, pt: (bp, 0, 0)),
        pl.BlockSpec((SEQ_PER_STEP, rows, 1), lambda bp, c, pt: (bp, 0, 0)),
    ]
    for s in range(SEQ_PER_STEP):
        for j in range(PAGES_PER_STEP):
            in_specs.append(pl.BlockSpec((1, IDX_DIM, PAGE_SIZE), page_map(s, j)))
    in_specs.append(pl.BlockSpec((SEQ_PER_STEP, IDX_DIM, PAGE_SIZE), lambda bp, c, pt: (bp, 0, 0)))
    body = functools.partial(_sample_score_body, n_chunks=n_chunks, dec_seq=dec_seq)
    return pl.pallas_call(
        body,
        grid_spec=pltpu.PrefetchScalarGridSpec(
            num_scalar_prefetch=1,
            grid=(dec_batch // SEQ_PER_STEP, n_chunks + 1),
            in_specs=in_specs,
            out_specs=pl.BlockSpec((SEQ_PER_STEP * dec_seq, PAGES_PER_STEP * PAGE_SIZE),
                                   lambda bp, c, pt: (bp, c)),
        ),
        out_shape=jax.ShapeDtypeStruct((dec_batch * dec_seq, width), F32),
        compiler_params=pltpu.CompilerParams(
            dimension_semantics=("arbitrary", "arbitrary"), vmem_limit_bytes=VMEM_LIMIT),
        name="sample_scores",
    )(page_table, qir, wcol, *([cache_ik] * (SEQ_PER_STEP * PAGES_PER_STEP)), knew)


SELECT_ROWS = 32


def _sample_select_body(sc_ref, sel_ref, key_ref, *, past, dec_seq, k_sel):
    shape = sc_ref.shape
    col = lax.broadcasted_iota(I32, shape, 1)
    t = lax.broadcasted_iota(I32, shape, 0) % dec_seq
    adm = (col - past) <= t
    key_ref[...] = jnp.where(adm, sc_ref[...], -jnp.inf)
    thr = _kth_largest_key(key_ref, shape[1], k_sel, 1)
    picked = jnp.logical_or(key_ref[...] >= _key_to_f32(thr), thr <= KEY_NEG_INF)
    sel_ref[...] = jnp.where(jnp.logical_and(picked, adm), 1.0, 0.0)


def _sample_select(scores, *, past, dec_seq):
    k_sel = min(TOPK_MAX, (past + dec_seq) // 4)
    body = functools.partial(_sample_select_body, past=past, dec_seq=dec_seq, k_sel=k_sel)
    rows, width = scores.shape
    assert rows % SELECT_ROWS == 0 and SELECT_ROWS % dec_seq == 0
    return pl.pallas_call(
        body,
        grid=(rows // SELECT_ROWS,),
        in_specs=[pl.BlockSpec((SELECT_ROWS, width), lambda i: (i, 0))],
        out_specs=pl.BlockSpec((SELECT_ROWS, width), lambda i: (i, 0)),
        out_shape=jax.ShapeDtypeStruct(scores.shape, F32),
        scratch_shapes=[pltpu.VMEM((SELECT_ROWS, width), F32)],
        compiler_params=pltpu.CompilerParams(
            dimension_semantics=("arbitrary",), vmem_limit_bytes=VMEM_LIMIT),
        name="sample_select",
    )(scores)


def _sample_attn_body(pt_ref, q_ref, sel_ref, *rest, n_chunks, dec_seq):
    npg = SEQ_PER_STEP * PAGES_PER_STEP
    kpages = rest[:npg]
    vpages = rest[npg:2 * npg]
    knew_ref, vnew_ref, o_ref, m_ref, l_ref, acc_ref = rest[2 * npg:]
    c = pl.program_id(1)
    scale = HEAD_DIM ** -0.5

    @pl.when(c == 0)
    def _():
        m_ref[...] = jnp.full(m_ref.shape, -jnp.inf, F32)
        l_ref[...] = jnp.zeros(l_ref.shape, F32)
        acc_ref[...] = jnp.zeros(acc_ref.shape, F32)

    def update(tiles, finish):
        n = SEQ_PER_STEP * N_KV_HEADS
        m_old = [m_ref[idx] for idx in range(n)]
        l_old = [l_ref[idx] for idx in range(n)]
        a_old = [acc_ref[idx] for idx in range(n)]
        m_out, l_out, a_out = [], [], []
        for idx, (kt, vt, sel) in enumerate(tiles):
            s, kh = divmod(idx, N_KV_HEADS)
            sc = _dot_nt(q_ref[s, kh], kt) * scale
            selg = jnp.concatenate([sel] * GROUP, axis=0) > 0.5
            sc = jnp.where(selg, sc, -jnp.inf)
            m_new = jnp.maximum(m_old[idx], jnp.max(sc, axis=1, keepdims=True))
            m_safe = jnp.where(m_new == -jnp.inf, 0.0, m_new)
            alpha = jnp.exp(m_old[idx] - m_safe)
            p = jnp.exp(sc - m_safe)
            m_out.append(m_new)
            l_out.append(alpha * l_old[idx] + jnp.sum(p, axis=1, keepdims=True))
            a_out.append(alpha * a_old[idx] + _dot(p.astype(BF16), vt))
        for idx in range(n):
            if finish:
                s, kh = divmod(idx, N_KV_HEADS)
                o_ref[s, kh] = a_out[idx] / l_out[idx]
            else:
                m_ref[idx] = m_out[idx]
                l_ref[idx] = l_out[idx]
                acc_ref[idx] = a_out[idx]

    def head_rows(page_refs, s, kh):
        return jnp.concatenate(
            [page_refs[s * PAGES_PER_STEP + j][0, pl.ds(kh, PAGE_SIZE, stride=N_KV_HEADS), :].astype(BF16)
             for j in range(PAGES_PER_STEP)], axis=0)

    @pl.when(c < n_chunks)
    def _():
        tiles = []
        for s in range(SEQ_PER_STEP):
            sel = sel_ref[s * dec_seq:(s + 1) * dec_seq, :]
            for kh in range(N_KV_HEADS):
                tiles.append((head_rows(kpages, s, kh), head_rows(vpages, s, kh), sel))
        update(tiles, False)

    @pl.when(c == n_chunks)
    def _():
        tiles = []
        for s in range(SEQ_PER_STEP):
            sel = sel_ref[s * dec_seq:(s + 1) * dec_seq, 0:PAGE_SIZE]
            for kh in range(N_KV_HEADS):
                tiles.append((knew_ref[s, kh].astype(BF16), vnew_ref[s, kh].astype(BF16), sel))
        update(tiles, True)


def _sample_attn(page_table, qs, sel, cache_k2, cache_v2, knew, vnew, *, dec_batch, dec_seq, n_pages):
    n_chunks = n_pages // PAGES_PER_STEP
    rows = GROUP * dec_seq

    def page_map(s, j):
        def f(bp, c, pt):
            cc = jnp.minimum(c, n_chunks - 1)
            return (pt[bp * SEQ_PER_STEP + s, cc * PAGES_PER_STEP + j], 0, 0)
        return f

    page_specs = []
    for s in range(SEQ_PER_STEP):
        for j in range(PAGES_PER_STEP):
            page_specs.append(pl.BlockSpec((1, PAGE_SIZE * N_KV_HEADS, HEAD_DIM), page_map(s, j)))
    new_spec = pl.BlockSpec((SEQ_PER_STEP, N_KV_HEADS, PAGE_SIZE, HEAD_DIM), lambda bp, c, pt: (bp, 0, 0, 0))
    in_specs = [
        pl.BlockSpec((SEQ_PER_STEP, N_KV_HEADS, rows, HEAD_DIM), lambda bp, c, pt: (bp, 0, 0, 0)),
        pl.BlockSpec((SEQ_PER_STEP * dec_seq, PAGES_PER_STEP * PAGE_SIZE), lambda bp, c, pt: (bp, c)),
    ] + page_specs + page_specs + [new_spec, new_spec]
    npg = SEQ_PER_STEP * PAGES_PER_STEP
    nstate = SEQ_PER_STEP * N_KV_HEADS
    body = functools.partial(_sample_attn_body, n_chunks=n_chunks, dec_seq=dec_seq)
    return pl.pallas_call(
        body,
        grid_spec=pltpu.PrefetchScalarGridSpec(
            num_scalar_prefetch=1,
            grid=(dec_batch // SEQ_PER_STEP, n_chunks + 1),
            in_specs=in_specs,
            out_specs=pl.BlockSpec((SEQ_PER_STEP, N_KV_HEADS, rows, HEAD_DIM), lambda bp, c, pt: (bp, 0, 0, 0)),
            scratch_shapes=[
                pltpu.VMEM((nstate, rows, 1), F32),
                pltpu.VMEM((nstate, rows, 1), F32),
                pltpu.VMEM((nstate, rows, HEAD_DIM), F32),
            ],
        ),
        out_shape=jax.ShapeDtypeStruct((dec_batch, N_KV_HEADS, rows, HEAD_DIM), F32),
        compiler_params=pltpu.CompilerParams(
            dimension_semantics=("arbitrary", "arbitrary"), vmem_limit_bytes=VMEM_LIMIT),
        name="sample_attn",
    )(page_table, qs, sel, *([cache_k2] * npg), *([cache_v2] * npg), knew, vnew)


def _log_sigmoid(x):
    return jnp.minimum(x, 0.0) - jnp.log1p(jnp.exp(-jnp.abs(x)))


def _lru_gates(xc, wa_ref, ba_ref, wx_ref, bx_ref, lam_ref):
    xcb = xc.astype(BF16)
    ra = []
    ia = []
    for n in range(N_LRU_BLOCKS):
        blk = xcb[:, n * LRU_BLOCK:(n + 1) * LRU_BLOCK]
        ra.append(_dot(blk, wa_ref[n]))
        ia.append(_dot(blk, wx_ref[n]))
    r = _sigmoid(jnp.concatenate(ra, axis=1) + ba_ref[...])
    ig = _sigmoid(jnp.concatenate(ia, axis=1) + bx_ref[...])
    log_a = (LRU_C * r) * _log_sigmoid(lam_ref[...])
    a = jnp.exp(log_a)
    b = jnp.sqrt(-jnp.tanh(log_a) * (a * a + 1.0)) * (ig * xc)
    return a, b


TL = 512


def _lru_prompt_body(x_ref, mod_ref, g_ref, win_ref, cw_ref, cb_ref, wa_ref, ba_ref, wx_ref, bx_ref,
                     lam_ref, wout_ref, y_ref, conv_ref, hl_ref, xpad, a_scr, b_scr, h_scr, hcar):
    i = pl.program_id(1)

    @pl.when(i == 0)
    def _():
        xpad[0:SUBLANES] = jnp.zeros((SUBLANES, LRU_WIDTH), F32)
        hcar[...] = jnp.zeros(hcar.shape, F32)

    x = x_ref[...]
    mod = mod_ref[0]
    h = _modulated_norm(x, g_ref[...], mod).astype(BF16)
    xb = _dot(h, win_ref[:, 0:LRU_WIDTH])
    gg = _dot(h, win_ref[:, LRU_WIDTH:2 * LRU_WIDTH])
    xpad[SUBLANES:SUBLANES + TL] = xb
    xc = cw_ref[CONV_W - 1:CONV_W] * xb + cb_ref[...]
    for j in range(CONV_W - 1):
        off = SUBLANES - (CONV_W - 1) + j
        xc = xc + cw_ref[j:j + 1] * xpad[off:off + TL]
    tail = xpad[TL + SUBLANES - (CONV_W - 1):TL + SUBLANES]
    conv_ref[0] = tail
    xpad[SUBLANES - (CONV_W - 1):SUBLANES] = tail

    a, b = _lru_gates(xc, wa_ref, ba_ref, wx_ref, bx_ref, lam_ref)
    a_scr[...] = a
    b_scr[...] = b
    row = lax.broadcasted_iota(I32, (SUBLANES, LRU_WIDTH), 0)

    def group(j, hprev):
        r0 = pl.multiple_of(j * SUBLANES, SUBLANES)
        aa = a_scr[pl.ds(r0, SUBLANES), :]
        bb = b_scr[pl.ds(r0, SUBLANES), :]
        d = 1
        while d < SUBLANES:
            a_sh = pltpu.roll(aa, d, axis=0)
            b_sh = pltpu.roll(bb, d, axis=0)
            m = row >= d
            bb = jnp.where(m, aa * b_sh + bb, bb)
            aa = jnp.where(m, aa * a_sh, aa)
            d *= 2
        hh = aa * hprev + bb
        h_scr[pl.ds(r0, SUBLANES), :] = hh
        return jnp.broadcast_to(hh[SUBLANES - 1:SUBLANES, :], (SUBLANES, LRU_WIDTH))

    hlast = lax.fori_loop(0, TL // SUBLANES, group, hcar[...], unroll=2)
    hcar[...] = hlast
    hl_ref[0] = hlast[0:1]
    hs = h_scr[...]
    out = _dot((hs * _silu(gg)).astype(BF16), wout_ref[...])
    y_ref[...] = x + mod[:, 2 * D_MODEL:3 * D_MODEL] * out


def _lru_prompt(x, mod, g, win, cw, cb, wa, ba, wx, bx, lam, wout, *, batch, seq):
    nb = seq // TL
    blk = lambda b, i: (b * nb + i, 0)
    const2 = lambda b, i: (0, 0)
    const3 = lambda b, i: (0, 0, 0)
    per_b = lambda b, i: (b, 0, 0)
    return pl.pallas_call(
        _lru_prompt_body,
        grid=(batch, nb),
        in_specs=[
            pl.BlockSpec((TL, D_MODEL), blk),
            pl.BlockSpec((1, 1, 3 * D_MODEL), per_b),
            pl.BlockSpec((1, D_MODEL), const2),
            pl.BlockSpec((D_MODEL, 2 * LRU_WIDTH), const2),
            pl.BlockSpec((CONV_W, LRU_WIDTH), const2),
            pl.BlockSpec((1, LRU_WIDTH), const2),
            pl.BlockSpec((N_LRU_BLOCKS, LRU_BLOCK, LRU_BLOCK), const3),
            pl.BlockSpec((1, LRU_WIDTH), const2),
            pl.BlockSpec((N_LRU_BLOCKS, LRU_BLOCK, LRU_BLOCK), const3),
            pl.BlockSpec((1, LRU_WIDTH), const2),
            pl.BlockSpec((1, LRU_WIDTH), const2),
            pl.BlockSpec((LRU_WIDTH, D_MODEL), const2),
        ],
        out_specs=[
            pl.BlockSpec((TL, D_MODEL), blk),
            pl.BlockSpec((1, CONV_W - 1, LRU_WIDTH), per_b),
            pl.BlockSpec((1, 1, LRU_WIDTH), per_b),
        ],
        out_shape=[
            jax.ShapeDtypeStruct((batch * seq, D_MODEL), F32),
            jax.ShapeDtypeStruct((batch, CONV_W - 1, LRU_WIDTH), F32),
            jax.ShapeDtypeStruct((batch, 1, LRU_WIDTH), F32),
        ],
        scratch_shapes=[
            pltpu.VMEM((TL + SUBLANES, LRU_WIDTH), F32),
            pltpu.VMEM((TL, LRU_WIDTH), F32),
            pltpu.VMEM((TL, LRU_WIDTH), F32),
            pltpu.VMEM((TL, LRU_WIDTH), F32),
            pltpu.VMEM((SUBLANES, LRU_WIDTH), F32),
        ],
        compiler_params=pltpu.CompilerParams(
            dimension_semantics=("arbitrary", "arbitrary"), vmem_limit_bytes=VMEM_LIMIT),
        name="lru_prompt",
    )(x, mod, g, win, cw, cb, wa, ba, wx, bx, lam, wout)


def _lru_sample_body(o_ref, gs_ref, x_ref, mod0_ref, wout0_ref, mod_ref, g_ref, win_ref, cw_ref, cb_ref,
                     wa_ref, ba_ref, wx_ref, bx_ref, lam_ref, wout_ref, sc_ref, sh_ref,
                     y_ref, conv_ref, hl_ref, *, nb, nt):
    x0 = x_ref[...]
    y0 = _dot((o_ref[...] * gs_ref[...].astype(F32)).astype(BF16), wout0_ref[...])
    x = x0 + mod0_ref[:, 2 * D_MODEL:3 * D_MODEL] * y0
    mod = mod_ref[...]
    h = _modulated_norm(x, g_ref[...], mod).astype(BF16)
    xb = _dot(h, win_ref[:, 0:LRU_WIDTH])
    gg = _dot(h, win_ref[:, LRU_WIDTH:2 * LRU_WIDTH])
    slabs = [sc_ref[j] for j in range(CONV_W - 1)] + [xb[t * nb:(t + 1) * nb] for t in range(nt)]
    xcs = []
    for t in range(nt):
        acc = cb_ref[...] + cw_ref[0:1] * slabs[t]
        for j in range(1, CONV_W):
            acc = acc + cw_ref[j:j + 1] * slabs[t + j]
        xcs.append(acc)
    for j in range(CONV_W - 1):
        conv_ref[j] = slabs[nt + j]
    xc = jnp.concatenate(xcs, axis=0)
    a, b = _lru_gates(xc, wa_ref, ba_ref, wx_ref, bx_ref, lam_ref)
    hprev = sh_ref[...]
    hs = []
    for t in range(nt):
        hprev = a[t * nb:(t + 1) * nb] * hprev + b[t * nb:(t + 1) * nb]
        hs.append(hprev)
    hl_ref[...] = hprev
    out = _dot((jnp.concatenate(hs, axis=0) * _silu(gg)).astype(BF16), wout_ref[...])
    y_ref[...] = x + mod[:, 2 * D_MODEL:3 * D_MODEL] * out


def _lru_sample(o, gs, x, mod0, wout0, mod, g, win, cw, cb, wa, ba, wx, bx, lam, wout, sc, sh, *, nb, nt):
    args = (o, gs, x, mod0, wout0, mod, g, win, cw, cb, wa, ba, wx, bx, lam, wout, sc, sh)

    def full(a):
        nd = a.ndim
        return pl.BlockSpec(a.shape, lambda i, nd=nd: (0,) * nd)

    out_shape = [
        jax.ShapeDtypeStruct((nt * nb, D_MODEL), F32),
        jax.ShapeDtypeStruct((CONV_W - 1, nb, LRU_WIDTH), F32),
        jax.ShapeDtypeStruct((nb, LRU_WIDTH), F32),
    ]
    body = functools.partial(_lru_sample_body, nb=nb, nt=nt)
    return pl.pallas_call(
        body,
        grid=(1,),
        in_specs=[full(a) for a in args],
        out_specs=[pl.BlockSpec(s.shape, lambda i, nd=len(s.shape): (0,) * nd) for s in out_shape],
        out_shape=out_shape,
        compiler_params=pltpu.CompilerParams(
            dimension_semantics=("arbitrary",), vmem_limit_bytes=VMEM_LIMIT),
        name="lru_sample",
    )(*args)


def kernel(x_prompt, x_sample, cache_k, cache_v, cache_idx_k, state_conv, state_h, page_table, c_prompt, c_sample, norm_g, ada_w, ada_b, attn_w_in, attn_q_norm, attn_k_norm, attn_w_out, lru_w_in, lru_conv_w, lru_conv_b, lru_w_a, lru_b_a, lru_w_x, lru_b_x, lru_lam, lru_w_out):
    B, S, _ = x_prompt.shape
    Bd, T, _ = x_sample.shape
    n_pages = page_table.shape[1]
    past = n_pages * PAGE_SIZE
    n_pool = cache_k.shape[1]
    assert S % QB == 0 and S % TL == 0 and n_pages % PAGES_PER_STEP == 0
    assert Bd % SEQ_PER_STEP == 0 and SEQ_PER_STEP * T == SUBLANES

    mod = _ada(jnp.concatenate([c_prompt, c_sample], axis=0), ada_w, ada_b)
    mod_p = [mod[l, :B].reshape(B, 1, 3 * D_MODEL) for l in range(2)]
    mod_s = [jnp.repeat(mod[l, B:], T, axis=0) for l in range(2)]

    w = attn_w_in[0]
    o_k, o_v, o_qi, o_ki, o_wi, o_g = 1024, 1280, 1536, 2560, 2624, 2640
    wcat = jnp.concatenate(
        [w[:, :o_ki], w[:, o_g:], w[:, o_ki:o_g], jnp.zeros((D_MODEL, PROJ_WIDTH - w.shape[1]), F32)],
        axis=1).astype(BF16)
    g0 = norm_g[0].reshape(1, D_MODEL)
    qn = attn_q_norm[0].reshape(1, HEAD_DIM)
    kn = attn_k_norm[0].reshape(1, HEAD_DIM)
    wout0 = attn_w_out[0].astype(BF16)

    tm = 512
    tabs_p = _rope_tables(jnp.arange(S))
    xp = x_prompt.reshape(B * S, D_MODEL)
    q, k, v, qi, kw, ki, gs = _proj(xp, mod_p[0], g0, wcat, qn, kn, tabs_p,
                                    tm=tm, rows_per_mod=S, tab_blocks=S // tm)
    x1p = _attn_prompt(q, qi, kw, k, v, gs, xp, mod_p[0], wout0, batch=B, seq=S)

    pos_s = past + jnp.arange(T)
    tabs_s = jnp.tile(_rope_tables(pos_s), (1, Bd, 1))
    xs = x_sample.reshape(Bd * T, D_MODEL)
    q2, k2, v2, qi2, kw2, ki2, gs2 = _proj(xs, mod_s[0].reshape(1, Bd * T, 3 * D_MODEL), g0, wcat, qn, kn,
                                           tabs_s, tm=Bd * T, rows_per_mod=Bd * T, tab_blocks=1)
    qir = qi2.reshape(Bd, T, N_IDX_HEADS, IDX_DIM).transpose(0, 2, 1, 3).reshape(Bd, N_IDX_HEADS * T, IDX_DIM)
    wcol = kw2[:, IDX_DIM:IDX_DIM + N_IDX_HEADS].reshape(Bd, T, N_IDX_HEADS).transpose(0, 2, 1)
    wcol = wcol.reshape(Bd, N_IDX_HEADS * T, 1)
    pad_rows = lambda a: jnp.pad(a, ((0, 0), (0, PAGE_SIZE - T)) + ((0, 0),) * (a.ndim - 2))
    ki_new = pad_rows(ki2.reshape(Bd, T, IDX_DIM)).transpose(0, 2, 1)
    cache_ik_t = cache_idx_k.reshape(n_pool, PAGE_SIZE, IDX_DIM).transpose(0, 2, 1)
    scores = _sample_scores(page_table, qir, wcol, cache_ik_t, ki_new,
                            dec_batch=Bd, dec_seq=T, n_pages=n_pages)
    sel = _sample_select(scores, past=past, dec_seq=T)
    qs = q2.reshape(Bd, T, N_KV_HEADS, GROUP, HEAD_DIM).transpose(0, 2, 3, 1, 4)
    qs = qs.reshape(Bd, N_KV_HEADS, GROUP * T, HEAD_DIM)
    k_new = pad_rows(k2.reshape(Bd, T, N_KV_HEADS, HEAD_DIM)).transpose(0, 2, 1, 3)
    v_new = pad_rows(v2.reshape(Bd, T, N_KV_HEADS, HEAD_DIM)).transpose(0, 2, 1, 3)
    o2 = _sample_attn(page_table, qs, sel,
                      cache_k.reshape(n_pool, PAGE_SIZE * N_KV_HEADS, HEAD_DIM),
                      cache_v.reshape(n_pool, PAGE_SIZE * N_KV_HEADS, HEAD_DIM),
                      k_new, v_new, dec_batch=Bd, dec_seq=T, n_pages=n_pages)
    o2 = o2.reshape(Bd, N_KV_HEADS, GROUP, T, HEAD_DIM).transpose(3, 0, 1, 2, 4).reshape(T * Bd, ATTN_WIDTH)

    g1 = norm_g[1].reshape(1, D_MODEL)
    win = lru_w_in[0].astype(BF16)
    cw = lru_conv_w[0]
    cb = lru_conv_b[0].reshape(1, LRU_WIDTH)
    wa = lru_w_a[0].astype(BF16)
    wx = lru_w_x[0].astype(BF16)
    ba = lru_b_a[0].reshape(1, LRU_WIDTH)
    bx = lru_b_x[0].reshape(1, LRU_WIDTH)
    lam = lru_lam[0].reshape(1, LRU_WIDTH)
    wout1 = lru_w_out[0].astype(BF16)

    yp, conv_p, h_p = _lru_prompt(x1p, mod_p[1], g1, win, cw, cb, wa, ba, wx, bx, lam, wout1, batch=B, seq=S)

    tmaj = lambda a: a.reshape(Bd, T, -1).transpose(1, 0, 2).reshape(T * Bd, -1)
    ys, conv_s, h_s = _lru_sample(
        o2, tmaj(gs2), tmaj(xs), tmaj(mod_s[0]), wout0, tmaj(mod_s[1]), g1, win, cw, cb, wa, ba, wx, bx,
        lam, wout1, state_conv[0].transpose(1, 0, 2), state_h[0], nb=Bd, nt=T)

    y_prompt = yp.reshape(B, S, D_MODEL)
    y_sample = ys.reshape(T, Bd, D_MODEL).transpose(1, 0, 2)
    return (y_prompt, y_sample,
            k.reshape(1, B, S, N_KV_HEADS, HEAD_DIM), v.reshape(1, B, S, N_KV_HEADS, HEAD_DIM),
            ki.reshape(1, B, S, IDX_DIM),
            k2.reshape(1, Bd, T, N_KV_HEADS, HEAD_DIM), v2.reshape(1, Bd, T, N_KV_HEADS, HEAD_DIM),
            ki2.reshape(1, Bd, T, IDX_DIM),
            conv_p.reshape(1, B, CONV_W - 1, LRU_WIDTH), h_p.reshape(1, B, LRU_WIDTH),
            conv_s.transpose(1, 0, 2).reshape(1, Bd, CONV_W - 1, LRU_WIDTH), h_s.reshape(1, Bd, LRU_WIDTH))
```

```python
import functools
import math

import jax
import jax.numpy as jnp
from jax import lax
from jax.experimental import pallas as pl
from jax.experimental.pallas import tpu as pltpu

F32 = jnp.float32
BF16 = jnp.bfloat16
I32 = jnp.int32

D_MODEL = 1024
N_HEADS = 8
N_KV_HEADS = 2
HEAD_DIM = 128
GROUP = N_HEADS // N_KV_HEADS
ATTN_WIDTH = N_HEADS * HEAD_DIM
KV_WIDTH = N_KV_HEADS * HEAD_DIM
N_IDX_HEADS = 16
IDX_DIM = 64
IDX_WIDTH = N_IDX_HEADS * IDX_DIM
TOPK_MAX = 256
ROPE_THETA = 500000.0
ROT_FRAC = 4
PAGE_SIZE = 128
LRU_WIDTH = D_MODEL
N_LRU_BLOCKS = 4
LRU_BLOCK = LRU_WIDTH // N_LRU_BLOCKS
CONV_W = 4
LRU_C = 8.0
EPS = 1e-6

LANES = 128
SUBLANES = 8
MXU_COLS = 256
VMEM_LIMIT = 56 * 1024 * 1024

OFF_Q = 0
OFF_K = OFF_Q + ATTN_WIDTH
OFF_V = OFF_K + KV_WIDTH
OFF_QI = OFF_V + KV_WIDTH
OFF_G = OFF_QI + IDX_WIDTH
OFF_KW = OFF_G + ATTN_WIDTH
PROJ_WIDTH = OFF_KW + LANES

INT_MIN = -(2 ** 31)
KEY_NEG_INF = 0x807FFFFF - 2 ** 32
RADIX_UNROLL = 4
RADIX_TRIPS = 32 // RADIX_UNROLL
COUNT_SLAB_ROWS = 64
LOG2E = 1.4426950408889634
NT_DIMS = (((1,), (1,)), ((), ()))


def _dot(a, b):
    return jnp.dot(a, b, preferred_element_type=F32)


def _dot_nt(a, b):
    return lax.dot_general(a, b, NT_DIMS, preferred_element_type=F32)


def _silu(x):
    return x / (1.0 + jnp.exp(-x))


def _sigmoid(x):
    return 1.0 / (1.0 + jnp.exp(-x))


def _rmsnorm(x, g):
    return x * lax.rsqrt(jnp.mean(x * x, axis=-1, keepdims=True) + EPS) * g


def _modulated_norm(x, g, mod):
    shift = mod[:, 0:D_MODEL]
    scale = mod[:, D_MODEL:2 * D_MODEL]
    return _rmsnorm(x, g) * (1.0 + scale) + shift


def _rope(y, tabs_ref, base, half):
    c = tabs_ref[base]
    s1 = tabs_ref[base + 1]
    s2 = tabs_ref[base + 2]
    return y * c + pltpu.roll(y, LANES - half, axis=1) * s1 + pltpu.roll(y, half, axis=1) * s2


def _key_to_f32(key):
    bits = jnp.where(key >= 0, key, key ^ 0x7FFFFFFF)
    return pltpu.bitcast(bits, F32)


def _kth_largest_key(sc_ref, n, k, axis, window=None, side_work=None, side_trips=0):
    kf = float(k)
    shape = list(sc_ref.shape)
    if window is None:
        window = slice(0, shape[1 - axis])
    shape[1 - axis] = window.stop - window.start
    shape[axis] = 1

    slab = COUNT_SLAB_ROWS if axis == 0 else LANES
    assert n % slab == 0

    def body(it, prefix):
        cand = prefix + lax.shift_left(jnp.int32(1), 31 - it)
        cand_f = _key_to_f32(cand)
        parts = []
        for j in range(n // slab):
            span = slice(j * slab, (j + 1) * slab)
            sc = sc_ref[span, window] if axis == 0 else sc_ref[window, span]
            parts.append(jnp.where(sc >= cand_f, 1.0, 0.0))
        while len(parts) > 1:
            nxt = [parts[a] + parts[a + 1] for a in range(0, len(parts) - 1, 2)]
            if len(parts) % 2:
                nxt.append(parts[-1])
            parts = nxt
        cnt = jnp.sum(parts[0], axis=axis, keepdims=True)
        return jnp.where(cnt >= kf, cand, prefix)

    def trip(with_side, t, prefix):
        for e in range(RADIX_UNROLL):
            prefix = body(t * RADIX_UNROLL + e, prefix)
        if with_side:
            side_work(t)
        return prefix

    assert side_trips <= RADIX_TRIPS
    prefix = jnp.full(tuple(shape), INT_MIN, I32)
    if side_trips:
        prefix = lax.fori_loop(0, side_trips, functools.partial(trip, True), prefix)
    return lax.fori_loop(side_trips, RADIX_TRIPS, functools.partial(trip, False), prefix)


def _ada_body(c_ref, w_ref, b_ref, o_ref):
    s = _silu(c_ref[...]).astype(BF16)
    o_ref[0] = _dot(s, w_ref[0].astype(BF16)) + b_ref[0]


def _ada(c_all, ada_w, ada_b):
    rows = c_all.shape[0]
    depth = ada_w.shape[0]
    nblk = 3
    return pl.pallas_call(
        _ada_body,
        grid=(depth, nblk),
        in_specs=[
            pl.BlockSpec((rows, D_MODEL), lambda l, j: (0, 0)),
            pl.BlockSpec((1, D_MODEL, D_MODEL), lambda l, j: (l, 0, j)),
            pl.BlockSpec((1, 1, D_MODEL), lambda l, j: (l, 0, j)),
        ],
        out_specs=pl.BlockSpec((1, rows, D_MODEL), lambda l, j: (l, 0, j)),
        out_shape=jax.ShapeDtypeStruct((depth, rows, 3 * D_MODEL), F32),
        compiler_params=pltpu.CompilerParams(
            dimension_semantics=("arbitrary", "arbitrary"), vmem_limit_bytes=VMEM_LIMIT),
        name="ada_mod",
    )(c_all, ada_w, ada_b.reshape(depth, 1, 3 * D_MODEL))


def _proj_body(x_ref, mod_ref, g_ref, w_ref, qn_ref, kn_ref, tabs_ref,
               q_ref, k_ref, v_ref, qi_ref, kw_ref, ki_ref, gs_ref):
    h = _modulated_norm(x_ref[...], g_ref[...], mod_ref[0]).astype(BF16)
    qn = qn_ref[...]
    kn = kn_ref[...]
    half_h = HEAD_DIM // ROT_FRAC // 2
    half_i = IDX_DIM // ROT_FRAC // 2

    def slabs(off, n_slabs):
        for c0 in range(0, n_slabs, MXU_COLS // LANES):
            z = _dot(h, w_ref[:, off + c0 * LANES:off + c0 * LANES + MXU_COLS])
            for e in range(MXU_COLS // LANES):
                yield c0 + e, z[:, e * LANES:(e + 1) * LANES]

    for hh, z in slabs(OFF_Q, N_HEADS):
        q_ref[:, hh * HEAD_DIM:(hh + 1) * HEAD_DIM] = _rope(_rmsnorm(z, qn), tabs_ref, 0, half_h).astype(BF16)
    tm = x_ref.shape[0]
    for hh, z in slabs(OFF_K, N_KV_HEADS):
        k_ref[pl.ds(hh, tm, stride=N_KV_HEADS), :] = _rope(_rmsnorm(z, kn), tabs_ref, 0, half_h)
    for hh, z in slabs(OFF_V, N_KV_HEADS):
        v_ref[pl.ds(hh, tm, stride=N_KV_HEADS), :] = z
    for c, z in slabs(OFF_QI, IDX_WIDTH // LANES):
        qi_ref[:, c * LANES:(c + 1) * LANES] = _rope(z, tabs_ref, 3, half_i).astype(BF16)
    for c, z in slabs(OFF_G, ATTN_WIDTH // LANES):
        gs_ref[:, c * LANES:(c + 1) * LANES] = _silu(z).astype(BF16)
    kw = _rope(_dot(h, w_ref[:, OFF_KW:OFF_KW + LANES]), tabs_ref, 6, half_i)
    kw_ref[...] = kw
    ki_ref[...] = kw[:, 0:IDX_DIM]


def _proj(x, mod, g, w, qn, kn, tabs, *, tm, rows_per_mod, tab_blocks):
    n = x.shape[0]
    mod_rows = mod.shape[1]
    grid = (n // tm,)
    row = lambda i: (i, 0)
    outs = [
        jax.ShapeDtypeStruct((n, ATTN_WIDTH), BF16),
        jax.ShapeDtypeStruct((n * N_KV_HEADS, HEAD_DIM), F32),
        jax.ShapeDtypeStruct((n * N_KV_HEADS, HEAD_DIM), F32),
        jax.ShapeDtypeStruct((n, IDX_WIDTH), BF16),
        jax.ShapeDtypeStruct((n, LANES), F32),
        jax.ShapeDtypeStruct((n, IDX_DIM), F32),
        jax.ShapeDtypeStruct((n, ATTN_WIDTH), BF16),
    ]
    return pl.pallas_call(
        _proj_body,
        grid=grid,
        in_specs=[
            pl.BlockSpec((tm, D_MODEL), row),
            pl.BlockSpec((1, mod_rows, 3 * D_MODEL), lambda i: (i * tm // rows_per_mod, 0, 0)),
            pl.BlockSpec((1, D_MODEL), lambda i: (0, 0)),
            pl.BlockSpec((D_MODEL, PROJ_WIDTH), lambda i: (0, 0)),
            pl.BlockSpec((1, HEAD_DIM), lambda i: (0, 0)),
            pl.BlockSpec((1, HEAD_DIM), lambda i: (0, 0)),
            pl.BlockSpec((9, tm, LANES), lambda i: (0, i % tab_blocks, 0)),
        ],
        out_specs=[
            pl.BlockSpec((tm, ATTN_WIDTH), row),
            pl.BlockSpec((tm * N_KV_HEADS, HEAD_DIM), row),
            pl.BlockSpec((tm * N_KV_HEADS, HEAD_DIM), row),
            pl.BlockSpec((tm, IDX_WIDTH), row),
            pl.BlockSpec((tm, LANES), row),
            pl.BlockSpec((tm, IDX_DIM), row),
            pl.BlockSpec((tm, ATTN_WIDTH), row),
        ],
        out_shape=outs,
        compiler_params=pltpu.CompilerParams(
            dimension_semantics=("arbitrary",), vmem_limit_bytes=VMEM_LIMIT),
        name="attn_proj",
    )(x, mod, g, w, qn, kn, tabs)


def _rope_tables(pos):
    posf = pos.astype(F32)
    t = pos.shape[0]

    def base(d):
        r = d // ROT_FRAC
        half = r // 2
        inv = jnp.exp(-jnp.log(jnp.asarray(ROPE_THETA, F32)) * jnp.arange(half, dtype=F32) * 2.0 / r)
        ang = posf[:, None] * inv[None, :]
        cos = jnp.cos(ang)
        sin = jnp.sin(ang)
        c = jnp.concatenate([cos, cos, jnp.ones((t, d - r), F32)], axis=1)
        s1 = jnp.concatenate([-sin, jnp.zeros((t, d - half), F32)], axis=1)
        s2 = jnp.concatenate([jnp.zeros((t, half), F32), sin, jnp.zeros((t, d - r), F32)], axis=1)
        return c, s1, s2

    hc, hs1, hs2 = base(HEAD_DIM)
    ic, is1, is2 = base(IDX_DIM)
    wi_scale = N_IDX_HEADS ** -0.5 * IDX_DIM ** -0.5
    pad = LANES - IDX_DIM
    kc = jnp.concatenate([ic, jnp.full((t, N_IDX_HEADS), wi_scale, F32),
                          jnp.zeros((t, pad - N_IDX_HEADS), F32)], axis=1)
    ks1 = jnp.concatenate([is1, jnp.zeros((t, pad), F32)], axis=1)
    ks2 = jnp.concatenate([is2, jnp.zeros((t, pad), F32)], axis=1)
    two = lambda a: jnp.concatenate([a, a], axis=1)
    return jnp.stack([hc, hs1, hs2, two(ic), two(is1), two(is2), kc, ks1, ks2])


QB = 128
SUB_BLOCKS = 1
KEY_BUCKET = 256


def _attn_prompt_body(q_ref, qi_ref, kwb_ref, kws_ref, k_ref, v_ref, gs_ref, x_ref, gate_ref,
                      wout_ref, o_ref, kbf, vbf, kipar, sct_ref, s_ref, *, seq, k_sel):
    i = pl.program_id(1)
    assert SUB_BLOCKS == 1

    @pl.when(i == 0)
    def _():
        ones_col = jnp.where(lax.broadcasted_iota(I32, (seq, HEAD_DIM), 1) == 0, 1.0, 0.0).astype(BF16)
        for kh in range(N_KV_HEADS):
            kbf[kh] = k_ref[pl.ds(kh, seq, stride=N_KV_HEADS), :].astype(BF16)
            vbf[kh, :, 0:HEAD_DIM] = v_ref[pl.ds(kh, seq, stride=N_KV_HEADS), :].astype(BF16)
            vbf[kh, :, HEAD_DIM:2 * HEAD_DIM] = ones_col
        kw = kws_ref[...]
        lane = lax.broadcasted_iota(I32, kw.shape, 1)
        ke = jnp.where(lane < IDX_DIM, kw, 0.0)
        kipar[0] = ke.astype(BF16)
        kipar[1] = pltpu.roll(ke, IDX_DIM, axis=1).astype(BF16)

    def stacked_q(rows, kh):
        return jnp.concatenate(
            [q_ref[rows, (kh * GROUP + g) * HEAD_DIM:(kh * GROUP + g + 1) * HEAD_DIM] for g in range(GROUP)],
            axis=0)

    def select(sb, nk):
        rows = slice(sb * QB, (sb + 1) * QB)
        first = (i * SUB_BLOCKS + sb) * QB
        col = lax.broadcasted_iota(I32, (QB, nk), 1)
        pos = first + lax.broadcasted_iota(I32, (QB, nk), 0)
        if nk <= k_sel:
            return jnp.where(col <= pos, 0.0, -jnp.inf)
        kwb = kwb_ref[rows, :]
        score = None
        for p in range(N_IDX_HEADS // 2):
            pair = None
            for par in range(2):
                h = 2 * p + par
                d = _dot_nt(qi_ref[rows, p * LANES:(p + 1) * LANES], kipar[par, 0:nk, :])
                term = kwb[:, IDX_DIM + h:IDX_DIM + h + 1] * jnp.maximum(d, 0.0)
                pair = term if pair is None else pair + term
            score = pair if score is None else score + pair
        score = jnp.where(col <= pos, score, -jnp.inf)
        sct_ref[0:nk, rows] = score.T

        def qk_tile(t):
            off = pl.multiple_of(t * MXU_COLS, MXU_COLS)
            for kh in range(N_KV_HEADS):
                s_ref[kh, :, pl.ds(off, MXU_COLS)] = _dot_nt(stacked_q(rows, kh), kbf[kh, pl.ds(off, MXU_COLS), :])

        assert nk % MXU_COLS == 0
        thr = _kth_largest_key(sct_ref, nk, k_sel, 0, window=rows,
                               side_work=qk_tile, side_trips=nk // MXU_COLS)
        key_row = lax.broadcasted_iota(I32, (nk, QB), 0)
        q_pos = first + lax.broadcasted_iota(I32, (nk, QB), 1)
        sel_t = jnp.logical_and(
            key_row <= q_pos,
            jnp.logical_or(sct_ref[0:nk, rows] >= _key_to_f32(thr), thr <= KEY_NEG_INF))
        return jnp.where(sel_t, 0.0, -jnp.inf).T

    def attend(sb, nk, bias):
        rows = slice(sb * QB, (sb + 1) * QB)
        c = HEAD_DIM ** -0.5 * LOG2E
        heads = [None] * N_HEADS
        for kh in range(N_KV_HEADS):
            if nk <= k_sel:
                s = _dot_nt(stacked_q(rows, kh), kbf[kh, 0:nk, :])
            else:
                s = s_ref[kh, :, 0:nk]
            ps = []
            for g in range(GROUP):
                sg = s[g * QB:(g + 1) * QB] + bias
                ps.append(jnp.exp2((sg - jnp.max(sg, axis=1, keepdims=True)) * c).astype(BF16))
            o = _dot(jnp.concatenate(ps, axis=0), vbf[kh, 0:nk, :])
            for g in range(GROUP):
                og = o[g * QB:(g + 1) * QB]
                heads[kh * GROUP + g] = og[:, 0:HEAD_DIM] / og[:, HEAD_DIM:HEAD_DIM + 1]
        attn = jnp.concatenate(heads, axis=1)
        y = _dot((attn * gs_ref[rows, :].astype(F32)).astype(BF16), wout_ref[...])
        o_ref[rows, :] = x_ref[rows, :] + gate_ref[0] * y

    def block(nk):
        biases = [select(sb, nk) for sb in range(SUB_BLOCKS)]
        for sb in range(SUB_BLOCKS):
            attend(sb, nk, biases[sb])

    steps_per_bucket = KEY_BUCKET // (SUB_BLOCKS * QB)
    for bucket in range(seq // KEY_BUCKET):
        pl.when(i // steps_per_bucket == bucket)(functools.partial(block, (bucket + 1) * KEY_BUCKET))


def _attn_prompt(q, qi, kw, k, v, gs, x, mod, wout, *, batch, seq):
    step = SUB_BLOCKS * QB
    nb = seq // step
    k_sel = min(TOPK_MAX, seq // 4)
    blk = lambda b, i: (b * nb + i, 0)
    whole = lambda b, i: (b, 0)
    body = functools.partial(_attn_prompt_body, seq=seq, k_sel=k_sel)
    return pl.pallas_call(
        body,
        grid=(batch, nb),
        in_specs=[
            pl.BlockSpec((step, ATTN_WIDTH), blk),
            pl.BlockSpec((step, IDX_WIDTH), blk),
            pl.BlockSpec((step, LANES), blk),
            pl.BlockSpec((seq, LANES), whole),
            pl.BlockSpec((seq * N_KV_HEADS, HEAD_DIM), whole),
            pl.BlockSpec((seq * N_KV_HEADS, HEAD_DIM), whole),
            pl.BlockSpec((step, ATTN_WIDTH), blk),
            pl.BlockSpec((step, D_MODEL), blk),
            pl.BlockSpec((1, 1, D_MODEL), lambda b, i: (b, 0, 2)),
            pl.BlockSpec((ATTN_WIDTH, D_MODEL), lambda b, i: (0, 0)),
        ],
        out_specs=pl.BlockSpec((step, D_MODEL), blk),
        out_shape=jax.ShapeDtypeStruct((batch * seq, D_MODEL), F32),
        scratch_shapes=[
            pltpu.VMEM((N_KV_HEADS, seq, HEAD_DIM), BF16),
            pltpu.VMEM((N_KV_HEADS, seq, 2 * HEAD_DIM), BF16),
            pltpu.VMEM((2, seq, LANES), BF16),
            pltpu.VMEM((seq, step), F32),
            pltpu.VMEM((N_KV_HEADS, GROUP * QB, seq), F32),
        ],
        compiler_params=pltpu.CompilerParams(
            dimension_semantics=("arbitrary", "arbitrary"), vmem_limit_bytes=VMEM_LIMIT),
        name="attn_prompt",
    )(q, qi, kw, kw, k, v, gs, x, mod, wout)


PAGES_PER_STEP = 16
SEQ_PER_STEP = 2


def _sample_tile_score(qir, wcol, kpage_t):
    d = _dot(qir, kpage_t.astype(BF16))
    r = wcol * jnp.maximum(d, 0.0)
    acc = r[0:SUBLANES]
    for j in range(1, r.shape[0] // SUBLANES):
        acc = acc + r[j * SUBLANES:(j + 1) * SUBLANES]
    half = SUBLANES // 2
    return acc[0:half] + acc[half:SUBLANES]


def _sample_score_body(pt_ref, qir_ref, wcol_ref, *rest, n_chunks, dec_seq):
    npg = SEQ_PER_STEP * PAGES_PER_STEP
    pages = rest[:npg]
    knew_ref = rest[npg]
    o_ref = rest[npg + 1]
    c = pl.program_id(1)

    @pl.when(c < n_chunks)
    def _():
        for s in range(SEQ_PER_STEP):
            kcat = jnp.concatenate([pages[s * PAGES_PER_STEP + j][0] for j in range(PAGES_PER_STEP)], axis=1)
            o_ref[s * dec_seq:(s + 1) * dec_seq, :] = _sample_tile_score(qir_ref[s], wcol_ref[s], kcat)

    @pl.when(c == n_chunks)
    def _():
        o_ref[...] = jnp.zeros(o_ref.shape, F32)
        for s in range(SEQ_PER_STEP):
            o_ref[s * dec_seq:(s + 1) * dec_seq, 0:PAGE_SIZE] = _sample_tile_score(
                qir_ref[s], wcol_ref[s], knew_ref[s])


def _sample_scores(page_table, qir, wcol, cache_ik, knew, *, dec_batch, dec_seq, n_pages):
    n_chunks = n_pages // PAGES_PER_STEP
    width = (n_chunks + 1) * PAGES_PER_STEP * PAGE_SIZE
    rows = IDX_DIM

    def page_map(s, j):
        def f(bp, c, pt):
            cc = jnp.minimum(c, n_chunks - 1)
            return (pt[bp * SEQ_PER_STEP + s, cc * PAGES_PER_STEP + j], 0, 0)
        return f

    in_specs = [
        pl.BlockSpec((SEQ_PER_STEP, rows, IDX_DIM), lambda bp, c, pt: (bp, 0, 0)),
        pl.BlockSpec((SEQ_PER_STEP, rows, 1), lambda bp, c, pt: (bp, 0, 0)),
    ]
    for s in range(SEQ_PER_STEP):
        for j in range(PAGES_PER_STEP):
            in_specs.append(pl.BlockSpec((1, IDX_DIM, PAGE_SIZE), page_map(s, j)))
    in_specs.append(pl.BlockSpec((SEQ_PER_STEP, IDX_DIM, PAGE_SIZE), lambda bp, c, pt: (bp, 0, 0)))
    body = functools.partial(_sample_score_body, n_chunks=n_chunks, dec_seq=dec_seq)
    return pl.pallas_call(
        body,
        grid_spec=pltpu.PrefetchScalarGridSpec(
            num_scalar_prefetch=1,
            grid=(dec_batch // SEQ_PER_STEP, n_chunks + 1),
            in_specs=in_specs,
            out_specs=pl.BlockSpec((SEQ_PER_STEP * dec_seq, PAGES_PER_STEP * PAGE_SIZE),
                                   lambda bp, c, pt: (bp, c)),
        ),
        out_shape=jax.ShapeDtypeStruct((dec_batch * dec_seq, width), F32),
        compiler_params=pltpu.CompilerParams(
            dimension_semantics=("arbitrary", "arbitrary"), vmem_limit_bytes=VMEM_LIMIT),
        name="sample_scores",
    )(page_table, qir, wcol, *([cache_ik] * (SEQ_PER_STEP * PAGES_PER_STEP)), knew)


SELECT_ROWS = 32


def _sample_select_body(sc_ref, sel_ref, key_ref, *, past, dec_seq, k_sel):
    shape = sc_ref.shape
    col = lax.broadcasted_iota(I32, shape, 1)
    t = lax.broadcasted_iota(I32, shape, 0) % dec_seq
    adm = (col - past) <= t
    key_ref[...] = jnp.where(adm, sc_ref[...], -jnp.inf)
    thr = _kth_largest_key(key_ref, shape[1], k_sel, 1)
    picked = jnp.logical_or(key_ref[...] >= _key_to_f32(thr), thr <= KEY_NEG_INF)
    sel_ref[...] = jnp.where(jnp.logical_and(picked, adm), 1.0, 0.0)


def _sample_select(scores, *, past, dec_seq):
    k_sel = min(TOPK_MAX, (past + dec_seq) // 4)
    body = functools.partial(_sample_select_body, past=past, dec_seq=dec_seq, k_sel=k_sel)
    rows, width = scores.shape
    assert rows % SELECT_ROWS == 0 and SELECT_ROWS % dec_seq == 0
    return pl.pallas_call(
        body,
        grid=(rows // SELECT_ROWS,),
        in_specs=[pl.BlockSpec((SELECT_ROWS, width), lambda i: (i, 0))],
        out_specs=pl.BlockSpec((SELECT_ROWS, width), lambda i: (i, 0)),
        out_shape=jax.ShapeDtypeStruct(scores.shape, F32),
        scratch_shapes=[pltpu.VMEM((SELECT_ROWS, width), F32)],
        compiler_params=pltpu.CompilerParams(
            dimension_semantics=("arbitrary",), vmem_limit_bytes=VMEM_LIMIT),
        name="sample_select",
    )(scores)


def _sample_attn_body(pt_ref, q_ref, sel_ref, *rest, n_chunks, dec_seq):
    npg = SEQ_PER_STEP * PAGES_PER_STEP
    kpages = rest[:npg]
    vpages = rest[npg:2 * npg]
    knew_ref, vnew_ref, o_ref, m_ref, l_ref, acc_ref = rest[2 * npg:]
    c = pl.program_id(1)
    scale = HEAD_DIM ** -0.5

    @pl.when(c == 0)
    def _():
        m_ref[...] = jnp.full(m_ref.shape, -jnp.inf, F32)
        l_ref[...] = jnp.zeros(l_ref.shape, F32)
        acc_ref[...] = jnp.zeros(acc_ref.shape, F32)

    def update(tiles, finish):
        n = SEQ_PER_STEP * N_KV_HEADS
        m_old = [m_ref[idx] for idx in range(n)]
        l_old = [l_ref[idx] for idx in range(n)]
        a_old = [acc_ref[idx] for idx in range(n)]
        m_out, l_out, a_out = [], [], []
        for idx, (kt, vt, sel) in enumerate(tiles):
            s, kh = divmod(idx, N_KV_HEADS)
            sc = _dot_nt(q_ref[s, kh], kt) * scale
            selg = jnp.concatenate([sel] * GROUP, axis=0) > 0.5
            sc = jnp.where(selg, sc, -jnp.inf)
            m_new = jnp.maximum(m_old[idx], jnp.max(sc, axis=1, keepdims=True))
            m_safe = jnp.where(m_new == -jnp.inf, 0.0, m_new)
            alpha = jnp.exp(m_old[idx] - m_safe)
            p = jnp.exp(sc - m_safe)
            m_out.append(m_new)
            l_out.append(alpha * l_old[idx] + jnp.sum(p, axis=1, keepdims=True))
            a_out.append(alpha * a_old[idx] + _dot(p.astype(BF16), vt))
        for idx in range(n):
            if finish:
                s, kh = divmod(idx, N_KV_HEADS)
                o_ref[s, kh] = a_out[idx] / l_out[idx]
            else:
                m_ref[idx] = m_out[idx]
                l_ref[idx] = l_out[idx]
                acc_ref[idx] = a_out[idx]

    def head_rows(page_refs, s, kh):
        return jnp.concatenate(
            [page_refs[s * PAGES_PER_STEP + j][0, pl.ds(kh, PAGE_SIZE, stride=N_KV_HEADS), :].astype(BF16)
             for j in range(PAGES_PER_STEP)], axis=0)

    @pl.when(c < n_chunks)
    def _():
        tiles = []
        for s in range(SEQ_PER_STEP):
            sel = sel_ref[s * dec_seq:(s + 1) * dec_seq, :]
            for kh in range(N_KV_HEADS):
                tiles.append((head_rows(kpages, s, kh), head_rows(vpages, s, kh), sel))
        update(tiles, False)

    @pl.when(c == n_chunks)
    def _():
        tiles = []
        for s in range(SEQ_PER_STEP):
            sel = sel_ref[s * dec_seq:(s + 1) * dec_seq, 0:PAGE_SIZE]
            for kh in range(N_KV_HEADS):
                tiles.append((knew_ref[s, kh].astype(BF16), vnew_ref[s, kh].astype(BF16), sel))
        update(tiles, True)


def _sample_attn(page_table, qs, sel, cache_k2, cache_v2, knew, vnew, *, dec_batch, dec_seq, n_pages):
    n_chunks = n_pages // PAGES_PER_STEP
    rows = GROUP * dec_seq

    def page_map(s, j):
        def f(bp, c, pt):
            cc = jnp.minimum(c, n_chunks - 1)
            return (pt[bp * SEQ_PER_STEP + s, cc * PAGES_PER_STEP + j], 0, 0)
        return f

    page_specs = []
    for s in range(SEQ_PER_STEP):
        for j in range(PAGES_PER_STEP):
            page_specs.append(pl.BlockSpec((1, PAGE_SIZE * N_KV_HEADS, HEAD_DIM), page_map(s, j)))
    new_spec = pl.BlockSpec((SEQ_PER_STEP, N_KV_HEADS, PAGE_SIZE, HEAD_DIM), lambda bp, c, pt: (bp, 0, 0, 0))
    in_specs = [
        pl.BlockSpec((SEQ_PER_STEP, N_KV_HEADS, rows, HEAD_DIM), lambda bp, c, pt: (bp, 0, 0, 0)),
        pl.BlockSpec((SEQ_PER_STEP * dec_seq, PAGES_PER_STEP * PAGE_SIZE), lambda bp, c, pt: (bp, c)),
    ] + page_specs + page_specs + [new_spec, new_spec]
    npg = SEQ_PER_STEP * PAGES_PER_STEP
    nstate = SEQ_PER_STEP * N_KV_HEADS
    body = functools.partial(_sample_attn_body, n_chunks=n_chunks, dec_seq=dec_seq)
    return pl.pallas_call(
        body,
        grid_spec=pltpu.PrefetchScalarGridSpec(
            num_scalar_prefetch=1,
            grid=(dec_batch // SEQ_PER_STEP, n_chunks + 1),
            in_specs=in_specs,
            out_specs=pl.BlockSpec((SEQ_PER_STEP, N_KV_HEADS, rows, HEAD_DIM), lambda bp, c, pt: (bp, 0, 0, 0)),
            scratch_shapes=[
                pltpu.VMEM((nstate, rows, 1), F32),
                pltpu.VMEM((nstate, rows, 1), F32),
                pltpu.VMEM((nstate, rows, HEAD_DIM), F32),
            ],
        ),
        out_shape=jax.ShapeDtypeStruct((dec_batch, N_KV_HEADS, rows, HEAD_DIM), F32),
        compiler_params=pltpu.CompilerParams(
            dimension_semantics=("arbitrary", "arbitrary"), vmem_limit_bytes=VMEM_LIMIT),
        name="sample_attn",
    )(page_table, qs, sel, *([cache_k2] * npg), *([cache_v2] * npg), knew, vnew)


def _log_sigmoid(x):
    return jnp.minimum(x, 0.0) - jnp.log1p(jnp.exp(-jnp.abs(x)))


def _lru_gates(xc, wa_ref, ba_ref, wx_ref, bx_ref, lam_ref):
    xcb = xc.astype(BF16)
    ra = []
    ia = []
    for n in range(N_LRU_BLOCKS):
        blk = xcb[:, n * LRU_BLOCK:(n + 1) * LRU_BLOCK]
        ra.append(_dot(blk, wa_ref[n]))
        ia.append(_dot(blk, wx_ref[n]))
    r = _sigmoid(jnp.concatenate(ra, axis=1) + ba_ref[...])
    ig = _sigmoid(jnp.concatenate(ia, axis=1) + bx_ref[...])
    log_a = (LRU_C * r) * _log_sigmoid(lam_ref[...])
    a = jnp.exp(log_a)
    b = jnp.sqrt(-jnp.tanh(log_a) * (a * a + 1.0)) * (ig * xc)
    return a, b


TL = 512


def _lru_prompt_body(x_ref, mod_ref, g_ref, win_ref, cw_ref, cb_ref, wa_ref, ba_ref, wx_ref, bx_ref,
                     lam_ref, wout_ref, y_ref, conv_ref, hl_ref, xpad, a_scr, b_scr, h_scr, hcar):
    i = pl.program_id(1)

    @pl.when(i == 0)
    def _():
        xpad[0:SUBLANES] = jnp.zeros((SUBLANES, LRU_WIDTH), F32)
        hcar[...] = jnp.zeros(hcar.shape, F32)

    x = x_ref[...]
    mod = mod_ref[0]
    h = _modulated_norm(x, g_ref[...], mod).astype(BF16)
    xb = _dot(h, win_ref[:, 0:LRU_WIDTH])
    gg = _dot(h, win_ref[:, LRU_WIDTH:2 * LRU_WIDTH])
    xpad[SUBLANES:SUBLANES + TL] = xb
    xc = cw_ref[CONV_W - 1:CONV_W] * xb + cb_ref[...]
    for j in range(CONV_W - 1):
        off = SUBLANES - (CONV_W - 1) + j
        xc = xc + cw_ref[j:j + 1] * xpad[off:off + TL]
    tail = xpad[TL + SUBLANES - (CONV_W - 1):TL + SUBLANES]
    conv_ref[0] = tail
    xpad[SUBLANES - (CONV_W - 1):SUBLANES] = tail

    a, b = _lru_gates(xc, wa_ref, ba_ref, wx_ref, bx_ref, lam_ref)
    a_scr[...] = a
    b_scr[...] = b
    row = lax.broadcasted_iota(I32, (SUBLANES, LRU_WIDTH), 0)

    def group(j, hprev):
        r0 = pl.multiple_of(j * SUBLANES, SUBLANES)
        aa = a_scr[pl.ds(r0, SUBLANES), :]
        bb = b_scr[pl.ds(r0, SUBLANES), :]
        d = 1
        while d < SUBLANES:
            a_sh = pltpu.roll(aa, d, axis=0)
            b_sh = pltpu.roll(bb, d, axis=0)
            m = row >= d
            bb = jnp.where(m, aa * b_sh + bb, bb)
            aa = jnp.where(m, aa * a_sh, aa)
            d *= 2
        hh = aa * hprev + bb
        h_scr[pl.ds(r0, SUBLANES), :] = hh
        return jnp.broadcast_to(hh[SUBLANES - 1:SUBLANES, :], (SUBLANES, LRU_WIDTH))

    hlast = lax.fori_loop(0, TL // SUBLANES, group, hcar[...], unroll=2)
    hcar[...] = hlast
    hl_ref[0] = hlast[0:1]
    hs = h_scr[...]
    out = _dot((hs * _silu(gg)).astype(BF16), wout_ref[...])
    y_ref[...] = x + mod[:, 2 * D_MODEL:3 * D_MODEL] * out


def _lru_prompt(x, mod, g, win, cw, cb, wa, ba, wx, bx, lam, wout, *, batch, seq):
    nb = seq // TL
    blk = lambda b, i: (b * nb + i, 0)
    const2 = lambda b, i: (0, 0)
    const3 = lambda b, i: (0, 0, 0)
    per_b = lambda b, i: (b, 0, 0)
    return pl.pallas_call(
        _lru_prompt_body,
        grid=(batch, nb),
        in_specs=[
            pl.BlockSpec((TL, D_MODEL), blk),
            pl.BlockSpec((1, 1, 3 * D_MODEL), per_b),
            pl.BlockSpec((1, D_MODEL), const2),
            pl.BlockSpec((D_MODEL, 2 * LRU_WIDTH), const2),
            pl.BlockSpec((CONV_W, LRU_WIDTH), const2),
            pl.BlockSpec((1, LRU_WIDTH), const2),
            pl.BlockSpec((N_LRU_BLOCKS, LRU_BLOCK, LRU_BLOCK), const3),
            pl.BlockSpec((1, LRU_WIDTH), const2),
            pl.BlockSpec((N_LRU_BLOCKS, LRU_BLOCK, LRU_BLOCK), const3),
            pl.BlockSpec((1, LRU_WIDTH), const2),
            pl.BlockSpec((1, LRU_WIDTH), const2),
            pl.BlockSpec((LRU_WIDTH, D_MODEL), const2),
        ],
        out_specs=[
            pl.BlockSpec((TL, D_MODEL), blk),
            pl.BlockSpec((1, CONV_W - 1, LRU_WIDTH), per_b),
            pl.BlockSpec((1, 1, LRU_WIDTH), per_b),
        ],
        out_shape=[
            jax.ShapeDtypeStruct((batch * seq, D_MODEL), F32),
            jax.ShapeDtypeStruct((batch, CONV_W - 1, LRU_WIDTH), F32),
            jax.ShapeDtypeStruct((batch, 1, LRU_WIDTH), F32),
        ],
        scratch_shapes=[
            pltpu.VMEM((TL + SUBLANES, LRU_WIDTH), F32),
            pltpu.VMEM((TL, LRU_WIDTH), F32),
            pltpu.VMEM((TL, LRU_WIDTH), F32),
            pltpu.VMEM((TL, LRU_WIDTH), F32),
            pltpu.VMEM((SUBLANES, LRU_WIDTH), F32),
        ],
        compiler_params=pltpu.CompilerParams(
            dimension_semantics=("arbitrary", "arbitrary"), vmem_limit_bytes=VMEM_LIMIT),
        name="lru_prompt",
    )(x, mod, g, win, cw, cb, wa, ba, wx, bx, lam, wout)


def _lru_sample_body(o_ref, gs_ref, x_ref, mod0_ref, wout0_ref, mod_ref, g_ref, win_ref, cw_ref, cb_ref,
                     wa_ref, ba_ref, wx_ref, bx_ref, lam_ref, wout_ref, sc_ref, sh_ref,
                     y_ref, conv_ref, hl_ref, *, nb, nt):
    x0 = x_ref[...]
    y0 = _dot((o_ref[...] * gs_ref[...].astype(F32)).astype(BF16), wout0_ref[...])
    x = x0 + mod0_ref[:, 2 * D_MODEL:3 * D_MODEL] * y0
    mod = mod_ref[...]
    h = _modulated_norm(x, g_ref[...], mod).astype(BF16)
    xb = _dot(h, win_ref[:, 0:LRU_WIDTH])
    gg = _dot(h, win_ref[:, LRU_WIDTH:2 * LRU_WIDTH])
    slabs = [sc_ref[j] for j in range(CONV_W - 1)] + [xb[t * nb:(t + 1) * nb] for t in range(nt)]
    xcs = []
    for t in range(nt):
        acc = cb_ref[...] + cw_ref[0:1] * slabs[t]
        for j in range(1, CONV_W):
            acc = acc + cw_ref[j:j + 1] * slabs[t + j]
        xcs.append(acc)
    for j in range(CONV_W - 1):
        conv_ref[j] = slabs[nt + j]
    xc = jnp.concatenate(xcs, axis=0)
    a, b = _lru_gates(xc, wa_ref, ba_ref, wx_ref, bx_ref, lam_ref)
    hprev = sh_ref[...]
    hs = []
    for t in range(nt):
        hprev = a[t * nb:(t + 1) * nb] * hprev + b[t * nb:(t + 1) * nb]
        hs.append(hprev)
    hl_ref[...] = hprev
    out = _dot((jnp.concatenate(hs, axis=0) * _silu(gg)).astype(BF16), wout_ref[...])
    y_ref[...] = x + mod[:, 2 * D_MODEL:3 * D_MODEL] * out


def _lru_sample(o, gs, x, mod0, wout0, mod, g, win, cw, cb, wa, ba, wx, bx, lam, wout, sc, sh, *, nb, nt):
    args = (o, gs, x, mod0, wout0, mod, g, win, cw, cb, wa, ba, wx, bx, lam, wout, sc, sh)

    def full(a):
        nd = a.ndim
        return pl.BlockSpec(a.shape, lambda i, nd=nd: (0,) * nd)

    out_shape = [
        jax.ShapeDtypeStruct((nt * nb, D_MODEL), F32),
        jax.ShapeDtypeStruct((CONV_W - 1, nb, LRU_WIDTH), F32),
        jax.ShapeDtypeStruct((nb, LRU_WIDTH), F32),
    ]
    body = functools.partial(_lru_sample_body, nb=nb, nt=nt)
    return pl.pallas_call(
        body,
        grid=(1,),
        in_specs=[full(a) for a in args],
        out_specs=[pl.BlockSpec(s.shape, lambda i, nd=len(s.shape): (0,) * nd) for s in out_shape],
        out_shape=out_shape,
        compiler_params=pltpu.CompilerParams(
            dimension_semantics=("arbitrary",), vmem_limit_bytes=VMEM_LIMIT),
        name="lru_sample",
    )(*args)


def kernel(x_prompt, x_sample, cache_k, cache_v, cache_idx_k, state_conv, state_h, page_table, c_prompt, c_sample, norm_g, ada_w, ada_b, attn_w_in, attn_q_norm, attn_k_norm, attn_w_out, lru_w_in, lru_conv_w, lru_conv_b, lru_w_a, lru_b_a, lru_w_x, lru_b_x, lru_lam, lru_w_out):
    B, S, _ = x_prompt.shape
    Bd, T, _ = x_sample.shape
    n_pages = page_table.shape[1]
    past = n_pages * PAGE_SIZE
    n_pool = cache_k.shape[1]
    assert S % QB == 0 and S % TL == 0 and n_pages % PAGES_PER_STEP == 0
    assert Bd % SEQ_PER_STEP == 0 and SEQ_PER_STEP * T == SUBLANES

    mod = _ada(jnp.concatenate([c_prompt, c_sample], axis=0), ada_w, ada_b)
    mod_p = [mod[l, :B].reshape(B, 1, 3 * D_MODEL) for l in range(2)]
    mod_s = [jnp.repeat(mod[l, B:], T, axis=0) for l in range(2)]

    w = attn_w_in[0]
    o_k, o_v, o_qi, o_ki, o_wi, o_g = 1024, 1280, 1536, 2560, 2624, 2640
    wcat = jnp.concatenate(
        [w[:, :o_ki], w[:, o_g:], w[:, o_ki:o_g], jnp.zeros((D_MODEL, PROJ_WIDTH - w.shape[1]), F32)],
        axis=1).astype(BF16)
    g0 = norm_g[0].reshape(1, D_MODEL)
    qn = attn_q_norm[0].reshape(1, HEAD_DIM)
    kn = attn_k_norm[0].reshape(1, HEAD_DIM)
    wout0 = attn_w_out[0].astype(BF16)

    tm = 512
    tabs_p = _rope_tables(jnp.arange(S))
    xp = x_prompt.reshape(B * S, D_MODEL)
    q, k, v, qi, kw, ki, gs = _proj(xp, mod_p[0], g0, wcat, qn, kn, tabs_p,
                                    tm=tm, rows_per_mod=S, tab_blocks=S // tm)
    x1p = _attn_prompt(q, qi, kw, k, v, gs, xp, mod_p[0], wout0, batch=B, seq=S)

    pos_s = past + jnp.arange(T)
    tabs_s = jnp.tile(_rope_tables(pos_s), (1, Bd, 1))
    xs = x_sample.reshape(Bd * T, D_MODEL)
    q2, k2, v2, qi2, kw2, ki2, gs2 = _proj(xs, mod_s[0].reshape(1, Bd * T, 3 * D_MODEL), g0, wcat, qn, kn,
                                           tabs_s, tm=Bd * T, rows_per_mod=Bd * T, tab_blocks=1)
    qir = qi2.reshape(Bd, T, N_IDX_HEADS, IDX_DIM).transpose(0, 2, 1, 3).reshape(Bd, N_IDX_HEADS * T, IDX_DIM)
    wcol = kw2[:, IDX_DIM:IDX_DIM + N_IDX_HEADS].reshape(Bd, T, N_IDX_HEADS).transpose(0, 2, 1)
    wcol = wcol.reshape(Bd, N_IDX_HEADS * T, 1)
    pad_rows = lambda a: jnp.pad(a, ((0, 0), (0, PAGE_SIZE - T)) + ((0, 0),) * (a.ndim - 2))
    ki_new = pad_rows(ki2.reshape(Bd, T, IDX_DIM)).transpose(0, 2, 1)
    cache_ik_t = cache_idx_k.reshape(n_pool, PAGE_SIZE, IDX_DIM).transpose(0, 2, 1)
    scores = _sample_scores(page_table, qir, wcol, cache_ik_t, ki_new,
                            dec_batch=Bd, dec_seq=T, n_pages=n_pages)
    sel = _sample_select(scores, past=past, dec_seq=T)
    qs = q2.reshape(Bd, T, N_KV_HEADS, GROUP, HEAD_DIM).transpose(0, 2, 3, 1, 4)
    qs = qs.reshape(Bd, N_KV_HEADS, GROUP * T, HEAD_DIM)
    k_new = pad_rows(k2.reshape(Bd, T, N_KV_HEADS, HEAD_DIM)).transpose(0, 2, 1, 3)
    v_new = pad_rows(v2.reshape(Bd, T, N_KV_HEADS, HEAD_DIM)).transpose(0, 2, 1, 3)
    o2 = _sample_attn(page_table, qs, sel,
                      cache_k.reshape(n_pool, PAGE_SIZE * N_KV_HEADS, HEAD_DIM),
                      cache_v.reshape(n_pool, PAGE_SIZE * N_KV_HEADS, HEAD_DIM),
                      k_new, v_new, dec_batch=Bd, dec_seq=T, n_pages=n_pages)
    o2 = o2.reshape(Bd, N_KV_HEADS, GROUP, T, HEAD_DIM).transpose(3, 0, 1, 2, 4).reshape(T * Bd, ATTN_WIDTH)

    g1 = norm_g[1].reshape(1, D_MODEL)
    win = lru_w_in[0].astype(BF16)
    cw = lru_conv_w[0]
    cb = lru_conv_b[0].reshape(1, LRU_WIDTH)
    wa = lru_w_a[0].astype(BF16)
    wx = lru_w_x[0].astype(BF16)
    ba = lru_b_a[0].reshape(1, LRU_WIDTH)
    bx = lru_b_x[0].reshape(1, LRU_WIDTH)
    lam = lru_lam[0].reshape(1, LRU_WIDTH)
    wout1 = lru_w_out[0].astype(BF16)

    yp, conv_p, h_p = _lru_prompt(x1p, mod_p[1], g1, win, cw, cb, wa, ba, wx, bx, lam, wout1, batch=B, seq=S)

    tmaj = lambda a: a.reshape(Bd, T, -1).transpose(1, 0, 2).reshape(T * Bd, -1)
    ys, conv_s, h_s = _lru_sample(
        o2, tmaj(gs2), tmaj(xs), tmaj(mod_s[0]), wout0, tmaj(mod_s[1]), g1, win, cw, cb, wa, ba, wx, bx,
        lam, wout1, state_conv[0].transpose(1, 0, 2), state_h[0], nb=Bd, nt=T)

    y_prompt = yp.reshape(B, S, D_MODEL)
    y_sample = ys.reshape(T, Bd, D_MODEL).transpose(1, 0, 2)
    return (y_prompt, y_sample,
            k.reshape(1, B, S, N_KV_HEADS, HEAD_DIM), v.reshape(1, B, S, N_KV_HEADS, HEAD_DIM),
            ki.reshape(1, B, S, IDX_DIM),
            k2.reshape(1, Bd, T, N_KV_HEADS, HEAD_DIM), v2.reshape(1, Bd, T, N_KV_HEADS, HEAD_DIM),
            ki2.reshape(1, Bd, T, IDX_DIM),
            conv_p.reshape(1, B, CONV_W - 1, LRU_WIDTH), h_p.reshape(1, B, LRU_WIDTH),
            conv_s.transpose(1, 0, 2).reshape(1, Bd, CONV_W - 1, LRU_WIDTH), h_s.reshape(1, Bd, LRU_WIDTH))
```

```python
import functools
import math

import jax
import jax.numpy as jnp
from jax import lax
from jax.experimental import pallas as pl
from jax.experimental.pallas import tpu as pltpu

F32 = jnp.float32
BF16 = jnp.bfloat16
I32 = jnp.int32

D_MODEL = 1024
N_HEADS = 8
N_KV_HEADS = 2
HEAD_DIM = 128
GROUP = N_HEADS // N_KV_HEADS
ATTN_WIDTH = N_HEADS * HEAD_DIM
KV_WIDTH = N_KV_HEADS * HEAD_DIM
N_IDX_HEADS = 16
IDX_DIM = 64
IDX_WIDTH = N_IDX_HEADS * IDX_DIM
TOPK_MAX = 256
ROPE_THETA = 500000.0
ROT_FRAC = 4
PAGE_SIZE = 128
LRU_WIDTH = D_MODEL
N_LRU_BLOCKS = 4
LRU_BLOCK = LRU_WIDTH // N_LRU_BLOCKS
CONV_W = 4
LRU_C = 8.0
EPS = 1e-6

LANES = 128
SUBLANES = 8
MXU_COLS = 256
VMEM_LIMIT = 56 * 1024 * 1024

OFF_Q = 0
OFF_K = OFF_Q + ATTN_WIDTH
OFF_V = OFF_K + KV_WIDTH
OFF_QI = OFF_V + KV_WIDTH
OFF_G = OFF_QI + IDX_WIDTH
OFF_KW = OFF_G + ATTN_WIDTH
PROJ_WIDTH = OFF_KW + LANES

INT_MIN = -(2 ** 31)
KEY_NEG_INF = 0x807FFFFF - 2 ** 32
RADIX_UNROLL = 4
RADIX_TRIPS = 32 // RADIX_UNROLL
COUNT_SLAB_ROWS = 64
LOG2E = 1.4426950408889634
NT_DIMS = (((1,), (1,)), ((), ()))


def _dot(a, b):
    return jnp.dot(a, b, preferred_element_type=F32)


def _dot_nt(a, b):
    return lax.dot_general(a, b, NT_DIMS, preferred_element_type=F32)


def _silu(x):
    return x / (1.0 + jnp.exp(-x))


def _sigmoid(x):
    return 1.0 / (1.0 + jnp.exp(-x))


def _rmsnorm(x, g):
    return x * lax.rsqrt(jnp.mean(x * x, axis=-1, keepdims=True) + EPS) * g


def _modulated_norm(x, g, mod):
    shift = mod[:, 0:D_MODEL]
    scale = mod[:, D_MODEL:2 * D_MODEL]
    return _rmsnorm(x, g) * (1.0 + scale) + shift


def _rope(y, tabs_ref, base, half):
    c = tabs_ref[base]
    s1 = tabs_ref[base + 1]
    s2 = tabs_ref[base + 2]
    return y * c + pltpu.roll(y, LANES - half, axis=1) * s1 + pltpu.roll(y, half, axis=1) * s2


def _key_to_f32(key):
    bits = jnp.where(key >= 0, key, key ^ 0x7FFFFFFF)
    return pltpu.bitcast(bits, F32)


def _kth_largest_key(sc_ref, n, k, axis, window=None, side_work=None, side_trips=0):
    kf = float(k)
    shape = list(sc_ref.shape)
    if window is None:
        window = slice(0, shape[1 - axis])
    shape[1 - axis] = window.stop - window.start
    shape[axis] = 1

    slab = COUNT_SLAB_ROWS if axis == 0 else LANES
    assert n % slab == 0

    def body(it, prefix):
        cand = prefix + lax.shift_left(jnp.int32(1), 31 - it)
        cand_f = _key_to_f32(cand)
        parts = []
        for j in range(n // slab):
            span = slice(j * slab, (j + 1) * slab)
            sc = sc_ref[span, window] if axis == 0 else sc_ref[window, span]
            parts.append(jnp.where(sc >= cand_f, 1.0, 0.0))
        while len(parts) > 1:
            nxt = [parts[a] + parts[a + 1] for a in range(0, len(parts) - 1, 2)]
            if len(parts) % 2:
                nxt.append(parts[-1])
            parts = nxt
        cnt = jnp.sum(parts[0], axis=axis, keepdims=True)
        return jnp.where(cnt >= kf, cand, prefix)

    def trip(with_side, t, prefix):
        for e in range(RADIX_UNROLL):
            prefix = body(t * RADIX_UNROLL + e, prefix)
        if with_side:
            side_work(t)
        return prefix

    assert side_trips <= RADIX_TRIPS
    prefix = jnp.full(tuple(shape), INT_MIN, I32)
    if side_trips:
        prefix = lax.fori_loop(0, side_trips, functools.partial(trip, True), prefix)
    return lax.fori_loop(side_trips, RADIX_TRIPS, functools.partial(trip, False), prefix)


def _ada_body(c_ref, w_ref, b_ref, o_ref):
    s = _silu(c_ref[...]).astype(BF16)
    o_ref[0] = _dot(s, w_ref[0].astype(BF16)) + b_ref[0]


def _ada(c_all, ada_w, ada_b):
    rows = c_all.shape[0]
    depth = ada_w.shape[0]
    nblk = 3
    return pl.pallas_call(
        _ada_body,
        grid=(depth, nblk),
        in_specs=[
            pl.BlockSpec((rows, D_MODEL), lambda l, j: (0, 0)),
            pl.BlockSpec((1, D_MODEL, D_MODEL), lambda l, j: (l, 0, j)),
            pl.BlockSpec((1, 1, D_MODEL), lambda l, j: (l, 0, j)),
        ],
        out_specs=pl.BlockSpec((1, rows, D_MODEL), lambda l, j: (l, 0, j)),
        out_shape=jax.ShapeDtypeStruct((depth, rows, 3 * D_MODEL), F32),
        compiler_params=pltpu.CompilerParams(
            dimension_semantics=("arbitrary", "arbitrary"), vmem_limit_bytes=VMEM_LIMIT),
        name="ada_mod",
    )(c_all, ada_w, ada_b.reshape(depth, 1, 3 * D_MODEL))


def _proj_body(x_ref, mod_ref, g_ref, w_ref, qn_ref, kn_ref, tabs_ref,
               q_ref, k_ref, v_ref, qi_ref, kw_ref, ki_ref, gs_ref):
    h = _modulated_norm(x_ref[...], g_ref[...], mod_ref[0]).astype(BF16)
    qn = qn_ref[...]
    kn = kn_ref[...]
    half_h = HEAD_DIM // ROT_FRAC // 2
    half_i = IDX_DIM // ROT_FRAC // 2

    def slabs(off, n_slabs):
        for c0 in range(0, n_slabs, MXU_COLS // LANES):
            z = _dot(h, w_ref[:, off + c0 * LANES:off + c0 * LANES + MXU_COLS])
            for e in range(MXU_COLS // LANES):
                yield c0 + e, z[:, e * LANES:(e + 1) * LANES]

    for hh, z in slabs(OFF_Q, N_HEADS):
        q_ref[:, hh * HEAD_DIM:(hh + 1) * HEAD_DIM] = _rope(_rmsnorm(z, qn), tabs_ref, 0, half_h).astype(BF16)
    tm = x_ref.shape[0]
    for hh, z in slabs(OFF_K, N_KV_HEADS):
        k_ref[pl.ds(hh, tm, stride=N_KV_HEADS), :] = _rope(_rmsnorm(z, kn), tabs_ref, 0, half_h)
    for hh, z in slabs(OFF_V, N_KV_HEADS):
        v_ref[pl.ds(hh, tm, stride=N_KV_HEADS), :] = z
    for c, z in slabs(OFF_QI, IDX_WIDTH // LANES):
        qi_ref[:, c * LANES:(c + 1) * LANES] = _rope(z, tabs_ref, 3, half_i).astype(BF16)
    for c, z in slabs(OFF_G, ATTN_WIDTH // LANES):
        gs_ref[:, c * LANES:(c + 1) * LANES] = _silu(z).astype(BF16)
    kw = _rope(_dot(h, w_ref[:, OFF_KW:OFF_KW + LANES]), tabs_ref, 6, half_i)
    kw_ref[...] = kw
    ki_ref[...] = kw[:, 0:IDX_DIM]


def _proj(x, mod, g, w, qn, kn, tabs, *, tm, rows_per_mod, tab_blocks):
    n = x.shape[0]
    mod_rows = mod.shape[1]
    grid = (n // tm,)
    row = lambda i: (i, 0)
    outs = [
        jax.ShapeDtypeStruct((n, ATTN_WIDTH), BF16),
        jax.ShapeDtypeStruct((n * N_KV_HEADS, HEAD_DIM), F32),
        jax.ShapeDtypeStruct((n * N_KV_HEADS, HEAD_DIM), F32),
        jax.ShapeDtypeStruct((n, IDX_WIDTH), BF16),
        jax.ShapeDtypeStruct((n, LANES), F32),
        jax.ShapeDtypeStruct((n, IDX_DIM), F32),
        jax.ShapeDtypeStruct((n, ATTN_WIDTH), BF16),
    ]
    return pl.pallas_call(
        _proj_body,
        grid=grid,
        in_specs=[
            pl.BlockSpec((tm, D_MODEL), row),
            pl.BlockSpec((1, mod_rows, 3 * D_MODEL), lambda i: (i * tm // rows_per_mod, 0, 0)),
            pl.BlockSpec((1, D_MODEL), lambda i: (0, 0)),
            pl.BlockSpec((D_MODEL, PROJ_WIDTH), lambda i: (0, 0)),
            pl.BlockSpec((1, HEAD_DIM), lambda i: (0, 0)),
            pl.BlockSpec((1, HEAD_DIM), lambda i: (0, 0)),
            pl.BlockSpec((9, tm, LANES), lambda i: (0, i % tab_blocks, 0)),
        ],
        out_specs=[
            pl.BlockSpec((tm, ATTN_WIDTH), row),
            pl.BlockSpec((tm * N_KV_HEADS, HEAD_DIM), row),
            pl.BlockSpec((tm * N_KV_HEADS, HEAD_DIM), row),
            pl.BlockSpec((tm, IDX_WIDTH), row),
            pl.BlockSpec((tm, LANES), row),
            pl.BlockSpec((tm, IDX_DIM), row),
            pl.BlockSpec((tm, ATTN_WIDTH), row),
        ],
        out_shape=outs,
        compiler_params=pltpu.CompilerParams(
            dimension_semantics=("arbitrary",), vmem_limit_bytes=VMEM_LIMIT),
        name="attn_proj",
    )(x, mod, g, w, qn, kn, tabs)


def _rope_tables(pos):
    posf = pos.astype(F32)
    t = pos.shape[0]

    def base(d):
        r = d // ROT_FRAC
        half = r // 2
        inv = jnp.exp(-jnp.log(jnp.asarray(ROPE_THETA, F32)) * jnp.arange(half, dtype=F32) * 2.0 / r)
        ang = posf[:, None] * inv[None, :]
        cos = jnp.cos(ang)
        sin = jnp.sin(ang)
        c = jnp.concatenate([cos, cos, jnp.ones((t, d - r), F32)], axis=1)
        s1 = jnp.concatenate([-sin, jnp.zeros((t, d - half), F32)], axis=1)
        s2 = jnp.concatenate([jnp.zeros((t, half), F32), sin, jnp.zeros((t, d - r), F32)], axis=1)
        return c, s1, s2

    hc, hs1, hs2 = base(HEAD_DIM)
    ic, is1, is2 = base(IDX_DIM)
    wi_scale = N_IDX_HEADS ** -0.5 * IDX_DIM ** -0.5
    pad = LANES - IDX_DIM
    kc = jnp.concatenate([ic, jnp.full((t, N_IDX_HEADS), wi_scale, F32),
                          jnp.zeros((t, pad - N_IDX_HEADS), F32)], axis=1)
    ks1 = jnp.concatenate([is1, jnp.zeros((t, pad), F32)], axis=1)
    ks2 = jnp.concatenate([is2, jnp.zeros((t, pad), F32)], axis=1)
    two = lambda a: jnp.concatenate([a, a], axis=1)
    return jnp.stack([hc, hs1, hs2, two(ic), two(is1), two(is2), kc, ks1, ks2])


QB = 128
SUB_BLOCKS = 1
KEY_BUCKET = 256


def _attn_prompt_body(q_ref, qi_ref, kwb_ref, kws_ref, k_ref, v_ref, gs_ref, x_ref, gate_ref,
                      wout_ref, o_ref, kbf, vbf, kipar, sct_ref, s_ref, *, seq, k_sel):
    i = pl.program_id(1)
    assert SUB_BLOCKS == 1

    @pl.when(i == 0)
    def _():
        ones_col = jnp.where(lax.broadcasted_iota(I32, (seq, HEAD_DIM), 1) == 0, 1.0, 0.0).astype(BF16)
        for kh in range(N_KV_HEADS):
            kbf[kh] = k_ref[pl.ds(kh, seq, stride=N_KV_HEADS), :].astype(BF16)
            vbf[kh, :, 0:HEAD_DIM] = v_ref[pl.ds(kh, seq, stride=N_KV_HEADS), :].astype(BF16)
            vbf[kh, :, HEAD_DIM:2 * HEAD_DIM] = ones_col
        kw = kws_ref[...]
        lane = lax.broadcasted_iota(I32, kw.shape, 1)
        ke = jnp.where(lane < IDX_DIM, kw, 0.0)
        kipar[0] = ke.astype(BF16)
        kipar[1] = pltpu.roll(ke, IDX_DIM, axis=1).astype(BF16)

    def stacked_q(rows, kh):
        return jnp.concatenate(
            [q_ref[rows, (kh * GROUP + g) * HEAD_DIM:(kh * GROUP + g + 1) * HEAD_DIM] for g in range(GROUP)],
            axis=0)

    def select(sb, nk):
        rows = slice(sb * QB, (sb + 1) * QB)
        first = (i * SUB_BLOCKS + sb) * QB
        col = lax.broadcasted_iota(I32, (QB, nk), 1)
        pos = first + lax.broadcasted_iota(I32, (QB, nk), 0)
        if nk <= k_sel:
            return jnp.where(col <= pos, 0.0, -jnp.inf)
        kwb = kwb_ref[rows, :]
        score = None
        for p in range(N_IDX_HEADS // 2):
            pair = None
            for par in range(2):
                h = 2 * p + par
                d = _dot_nt(qi_ref[rows, p * LANES:(p + 1) * LANES], kipar[par, 0:nk, :])
                term = kwb[:, IDX_DIM + h:IDX_DIM + h + 1] * jnp.maximum(d, 0.0)
                pair = term if pair is None else pair + term
            score = pair if score is None else score + pair
        score = jnp.where(col <= pos, score, -jnp.inf)
        sct_ref[0:nk, rows] = score.T

        def qk_tile(t):
            off = pl.multiple_of(t * MXU_COLS, MXU_COLS)
            for kh in range(N_KV_HEADS):
                s_ref[kh, :, pl.ds(off, MXU_COLS)] = _dot_nt(stacked_q(rows, kh), kbf[kh, pl.ds(off, MXU_COLS), :])

        assert nk % MXU_COLS == 0
        thr = _kth_largest_key(sct_ref, nk, k_sel, 0, window=rows,
                               side_work=qk_tile, side_trips=nk // MXU_COLS)
        key_row = lax.broadcasted_iota(I32, (nk, QB), 0)
        q_pos = first + lax.broadcasted_iota(I32, (nk, QB), 1)
        sel_t = jnp.logical_and(
            key_row <= q_pos,
            jnp.logical_or(sct_ref[0:nk, rows] >= _key_to_f32(thr), thr <= KEY_NEG_INF))
        return jnp.where(sel_t, 0.0, -jnp.inf).T

    def attend(sb, nk, bias):
        rows = slice(sb * QB, (sb + 1) * QB)
        c = HEAD_DIM ** -0.5 * LOG2E
        heads = [None] * N_HEADS
        for kh in range(N_KV_HEADS):
            if nk <= k_sel:
                s = _dot_nt(stacked_q(rows, kh), kbf[kh, 0:nk, :])
            else:
                s = s_ref[kh, :, 0:nk]
            ps = []
            for g in range(GROUP):
                sg = s[g * QB:(g + 1) * QB] + bias
                ps.append(jnp.exp2((sg - jnp.max(sg, axis=1, keepdims=True)) * c).astype(BF16))
            o = _dot(jnp.concatenate(ps, axis=0), vbf[kh, 0:nk, :])
            for g in range(GROUP):
                og = o[g * QB:(g + 1) * QB]
                heads[kh * GROUP + g] = og[:, 0:HEAD_DIM] / og[:, HEAD_DIM:HEAD_DIM + 1]
        attn = jnp.concatenate(heads, axis=1)
        y = _dot((attn * gs_ref[rows, :].astype(F32)).astype(BF16), wout_ref[...])
        o_ref[rows, :] = x_ref[rows, :] + gate_ref[0] * y

    def block(nk):
        biases = [select(sb, nk) for sb in range(SUB_BLOCKS)]
        for sb in range(SUB_BLOCKS):
            attend(sb, nk, biases[sb])

    steps_per_bucket = KEY_BUCKET // (SUB_BLOCKS * QB)
    for bucket in range(seq // KEY_BUCKET):
        pl.when(i // steps_per_bucket == bucket)(functools.partial(block, (bucket + 1) * KEY_BUCKET))


def _attn_prompt(q, qi, kw, k, v, gs, x, mod, wout, *, batch, seq):
    step = SUB_BLOCKS * QB
    nb = seq // step
    k_sel = min(TOPK_MAX, seq // 4)
    blk = lambda b, i: (b * nb + i, 0)
    whole = lambda b, i: (b, 0)
    body = functools.partial(_attn_prompt_body, seq=seq, k_sel=k_sel)
    return pl.pallas_call(
        body,
        grid=(batch, nb),
        in_specs=[
            pl.BlockSpec((step, ATTN_WIDTH), blk),
            pl.BlockSpec((step, IDX_WIDTH), blk),
            pl.BlockSpec((step, LANES), blk),
            pl.BlockSpec((seq, LANES), whole),
            pl.BlockSpec((seq * N_KV_HEADS, HEAD_DIM), whole),
            pl.BlockSpec((seq * N_KV_HEADS, HEAD_DIM), whole),
            pl.BlockSpec((step, ATTN_WIDTH), blk),
            pl.BlockSpec((step, D_MODEL), blk),
            pl.BlockSpec((1, 1, D_MODEL), lambda b, i: (b, 0, 2)),
            pl.BlockSpec((ATTN_WIDTH, D_MODEL), lambda b, i: (0, 0)),
        ],
        out_specs=pl.BlockSpec((step, D_MODEL), blk),
        out_shape=jax.ShapeDtypeStruct((batch * seq, D_MODEL), F32),
        scratch_shapes=[
            pltpu.VMEM((N_KV_HEADS, seq, HEAD_DIM), BF16),
            pltpu.VMEM((N_KV_HEADS, seq, 2 * HEAD_DIM), BF16),
            pltpu.VMEM((2, seq, LANES), BF16),
            pltpu.VMEM((seq, step), F32),
            pltpu.VMEM((N_KV_HEADS, GROUP * QB, seq), F32),
        ],
        compiler_params=pltpu.CompilerParams(
            dimension_semantics=("arbitrary", "arbitrary"), vmem_limit_bytes=VMEM_LIMIT),
        name="attn_prompt",
    )(q, qi, kw, kw, k, v, gs, x, mod, wout)


PAGES_PER_STEP = 16
SEQ_PER_STEP = 2


def _sample_tile_score(qir, wcol, kpage_t):
    d = _dot(qir, kpage_t.astype(BF16))
    r = wcol * jnp.maximum(d, 0.0)
    acc = r[0:SUBLANES]
    for j in range(1, r.shape[0] // SUBLANES):
        acc = acc + r[j * SUBLANES:(j + 1) * SUBLANES]
    half = SUBLANES // 2
    return acc[0:half] + acc[half:SUBLANES]


def _sample_score_body(pt_ref, qir_ref, wcol_ref, *rest, n_chunks, dec_seq):
    npg = SEQ_PER_STEP * PAGES_PER_STEP
    pages = rest[:npg]
    knew_ref = rest[npg]
    o_ref = rest[npg + 1]
    c = pl.program_id(1)

    @pl.when(c < n_chunks)
    def _():
        for s in range(SEQ_PER_STEP):
            kcat = jnp.concatenate([pages[s * PAGES_PER_STEP + j][0] for j in range(PAGES_PER_STEP)], axis=1)
            o_ref[s * dec_seq:(s + 1) * dec_seq, :] = _sample_tile_score(qir_ref[s], wcol_ref[s], kcat)

    @pl.when(c == n_chunks)
    def _():
        o_ref[...] = jnp.zeros(o_ref.shape, F32)
        for s in range(SEQ_PER_STEP):
            o_ref[s * dec_seq:(s + 1) * dec_seq, 0:PAGE_SIZE] = _sample_tile_score(
                qir_ref[s], wcol_ref[s], knew_ref[s])


def _sample_scores(page_table, qir, wcol, cache_ik, knew, *, dec_batch, dec_seq, n_pages):
    n_chunks = n_pages // PAGES_PER_STEP
    width = (n_chunks + 1) * PAGES_PER_STEP * PAGE_SIZE
    rows = IDX_DIM

    def page_map(s, j):
        def f(bp, c, pt):
            cc = jnp.minimum(c, n_chunks - 1)
            return (pt[bp * SEQ_PER_STEP + s, cc * PAGES_PER_STEP + j], 0, 0)
        return f

    in_specs = [
        pl.BlockSpec((SEQ_PER_STEP, rows, IDX_DIM), lambda bp, c, pt: (bp, 0, 0)),
        pl.BlockSpec((SEQ_PER_STEP, rows, 1), lambda bp, c, pt: (bp, 0, 0)),
    ]
    for s in range(SEQ_PER_STEP):
        for j in range(PAGES_PER_STEP):
            in_specs.append(pl.BlockSpec((1, IDX_DIM, PAGE_SIZE), page_map(s, j)))
    in_specs.append(pl.BlockSpec((SEQ_PER_STEP, IDX_DIM, PAGE_SIZE), lambda bp, c, pt: (bp, 0, 0)))
    body = functools.partial(_sample_score_body, n_chunks=n_chunks, dec_seq=dec_seq)
    return pl.pallas_call(
        body,
        grid_spec=pltpu.PrefetchScalarGridSpec(
            num_scalar_prefetch=1,
            grid=(dec_batch // SEQ_PER_STEP, n_chunks + 1),
            in_specs=in_specs,
            out_specs=pl.BlockSpec((SEQ_PER_STEP * dec_seq, PAGES_PER_STEP * PAGE_SIZE),
                                   lambda bp, c, pt: (bp, c)),
        ),
        out_shape=jax.ShapeDtypeStruct((dec_batch * dec_seq, width), F32),
        compiler_params=pltpu.CompilerParams(
            dimension_semantics=("arbitrary", "arbitrary"), vmem_limit_bytes=VMEM_LIMIT),
        name="sample_scores",
    )(page_table, qir, wcol, *([cache_ik] * (SEQ_PER_STEP * PAGES_PER_STEP)), knew)


SELECT_ROWS = 32


def _sample_select_body(sc_ref, sel_ref, key_ref, *, past, dec_seq, k_sel):
    shape = sc_ref.shape
    col = lax.broadcasted_iota(I32, shape, 1)
    t = lax.broadcasted_iota(I32, shape, 0) % dec_seq
    adm = (col - past) <= t
    key_ref[...] = jnp.where(adm, sc_ref[...], -jnp.inf)
    thr = _kth_largest_key(key_ref, shape[1], k_sel, 1)
    picked = jnp.logical_or(key_ref[...] >= _key_to_f32(thr), thr <= KEY_NEG_INF)
    sel_ref[...] = jnp.where(jnp.logical_and(picked, adm), 1.0, 0.0)


def _sample_select(scores, *, past, dec_seq):
    k_sel = min(TOPK_MAX, (past + dec_seq) // 4)
    body = functools.partial(_sample_select_body, past=past, dec_seq=dec_seq, k_sel=k_sel)
    rows, width = scores.shape
    assert rows % SELECT_ROWS == 0 and SELECT_ROWS % dec_seq == 0
    return pl.pallas_call(
        body,
        grid=(rows // SELECT_ROWS,),
        in_specs=[pl.BlockSpec((SELECT_ROWS, width), lambda i: (i, 0))],
        out_specs=pl.BlockSpec((SELECT_ROWS, width), lambda i: (i, 0)),
        out_shape=jax.ShapeDtypeStruct(scores.shape, F32),
        scratch_shapes=[pltpu.VMEM((SELECT_ROWS, width), F32)],
        compiler_params=pltpu.CompilerParams(
            dimension_semantics=("arbitrary",), vmem_limit_bytes=VMEM_LIMIT),
        name="sample_select",
    )(scores)


def _sample_attn_body(pt_ref, q_ref, sel_ref, *rest, n_chunks, dec_seq):
    npg = SEQ_PER_STEP * PAGES_PER_STEP
    kpages = rest[:npg]
    vpages = rest[npg:2 * npg]
    knew_ref, vnew_ref, o_ref, m_ref, l_ref, acc_ref = rest[2 * npg:]
    c = pl.program_id(1)
    scale = HEAD_DIM ** -0.5
    rows_per_head = GROUP * dec_seq
    rows_per_head_log2 = rows_per_head.bit_length() - 1
    assert rows_per_head == 1 << rows_per_head_log2 and N_KV_HEADS & (N_KV_HEADS - 1) == 0

    @pl.when(c == 0)
    def _():
        m_ref[...] = jnp.full(m_ref.shape, -jnp.inf, F32)
        l_ref[...] = jnp.zeros(l_ref.shape, F32)
        acc_ref[...] = jnp.zeros(acc_ref.shape, F32)

    def update(tiles, finish):
        m_old = [m_ref[s] for s in range(SEQ_PER_STEP)]
        l_old = [l_ref[s] for s in range(SEQ_PER_STEP)]
        a_old = [acc_ref[s] for s in range(SEQ_PER_STEP)]
        m_out, l_out, a_out = [], [], []
        for s, (kt, vt, sel) in enumerate(tiles):
            sc = _dot_nt(q_ref[s], kt) * scale
            key_head = jnp.bitwise_and(lax.broadcasted_iota(I32, sc.shape, 1), N_KV_HEADS - 1)
            row_head = jnp.right_shift(lax.broadcasted_iota(I32, sc.shape, 0), rows_per_head_log2)
            picked = jnp.concatenate([sel] * (N_KV_HEADS * GROUP), axis=0) > 0.5
            sc = jnp.where(jnp.logical_and(picked, key_head == row_head), sc, -jnp.inf)
            m_new = jnp.maximum(m_old[s], jnp.max(sc, axis=1, keepdims=True))
            m_safe = jnp.where(m_new == -jnp.inf, 0.0, m_new)
            alpha = jnp.exp(m_old[s] - m_safe)
            p = jnp.exp(sc - m_safe)
            m_out.append(m_new)
            l_out.append(alpha * l_old[s] + jnp.sum(p, axis=1, keepdims=True))
            a_out.append(alpha * a_old[s] + _dot(p.astype(BF16), vt))
        for s in range(SEQ_PER_STEP):
            if finish:
                o_ref[s] = a_out[s] / l_out[s]
            else:
                m_ref[s] = m_out[s]
                l_ref[s] = l_out[s]
                acc_ref[s] = a_out[s]

    def page_rows(page_refs, s):
        return jnp.concatenate(
            [page_refs[s * PAGES_PER_STEP + j][0].astype(BF16) for j in range(PAGES_PER_STEP)], axis=0)

    @pl.when(c < n_chunks)
    def _():
        update([(page_rows(kpages, s), page_rows(vpages, s), sel_ref[s * dec_seq:(s + 1) * dec_seq, :])
                for s in range(SEQ_PER_STEP)], False)

    @pl.when(c == n_chunks)
    def _():
        new_rows = PAGE_SIZE * N_KV_HEADS
        update([(knew_ref[s].astype(BF16), vnew_ref[s].astype(BF16),
                 sel_ref[s * dec_seq:(s + 1) * dec_seq, 0:new_rows]) for s in range(SEQ_PER_STEP)], True)


def _sample_attn(page_table, qs, sel, cache_k2, cache_v2, knew, vnew, *, dec_batch, dec_seq, n_pages):
    n_chunks = n_pages // PAGES_PER_STEP
    rows = N_KV_HEADS * GROUP * dec_seq
    page_rows = PAGE_SIZE * N_KV_HEADS

    def page_map(s, j):
        def f(bp, c, pt):
            cc = jnp.minimum(c, n_chunks - 1)
            return (pt[bp * SEQ_PER_STEP + s, cc * PAGES_PER_STEP + j], 0, 0)
        return f

    page_specs = []
    for s in range(SEQ_PER_STEP):
        for j in range(PAGES_PER_STEP):
            page_specs.append(pl.BlockSpec((1, page_rows, HEAD_DIM), page_map(s, j)))
    new_spec = pl.BlockSpec((SEQ_PER_STEP, page_rows, HEAD_DIM), lambda bp, c, pt: (bp, 0, 0))
    in_specs = [
        pl.BlockSpec((SEQ_PER_STEP, rows, HEAD_DIM), lambda bp, c, pt: (bp, 0, 0)),
        pl.BlockSpec((SEQ_PER_STEP * dec_seq, PAGES_PER_STEP * page_rows), lambda bp, c, pt: (bp, c)),
    ] + page_specs + page_specs + [new_spec, new_spec]
    npg = SEQ_PER_STEP * PAGES_PER_STEP
    body = functools.partial(_sample_attn_body, n_chunks=n_chunks, dec_seq=dec_seq)
    return pl.pallas_call(
        body,
        grid_spec=pltpu.PrefetchScalarGridSpec(
            num_scalar_prefetch=1,
            grid=(dec_batch // SEQ_PER_STEP, n_chunks + 1),
            in_specs=in_specs,
            out_specs=pl.BlockSpec((SEQ_PER_STEP, rows, HEAD_DIM), lambda bp, c, pt: (bp, 0, 0)),
            scratch_shapes=[
                pltpu.VMEM((SEQ_PER_STEP, rows, 1), F32),
                pltpu.VMEM((SEQ_PER_STEP, rows, 1), F32),
                pltpu.VMEM((SEQ_PER_STEP, rows, HEAD_DIM), F32),
            ],
        ),
        out_shape=jax.ShapeDtypeStruct((dec_batch, rows, HEAD_DIM), F32),
        compiler_params=pltpu.CompilerParams(
            dimension_semantics=("arbitrary", "arbitrary"), vmem_limit_bytes=VMEM_LIMIT),
        name="sample_attn",
    )(page_table, qs, sel, *([cache_k2] * npg), *([cache_v2] * npg), knew, vnew)


def _log_sigmoid(x):
    return jnp.minimum(x, 0.0) - jnp.log1p(jnp.exp(-jnp.abs(x)))


def _lru_gates(xc, wa_ref, ba_ref, wx_ref, bx_ref, lam_ref):
    xcb = xc.astype(BF16)
    ra = []
    ia = []
    for n in range(N_LRU_BLOCKS):
        blk = xcb[:, n * LRU_BLOCK:(n + 1) * LRU_BLOCK]
        ra.append(_dot(blk, wa_ref[n]))
        ia.append(_dot(blk, wx_ref[n]))
    r = _sigmoid(jnp.concatenate(ra, axis=1) + ba_ref[...])
    ig = _sigmoid(jnp.concatenate(ia, axis=1) + bx_ref[...])
    log_a = (LRU_C * r) * _log_sigmoid(lam_ref[...])
    a = jnp.exp(log_a)
    b = jnp.sqrt(-jnp.tanh(log_a) * (a * a + 1.0)) * (ig * xc)
    return a, b


TL = 512


def _lru_prompt_body(x_ref, mod_ref, g_ref, win_ref, cw_ref, cb_ref, wa_ref, ba_ref, wx_ref, bx_ref,
                     lam_ref, wout_ref, y_ref, conv_ref, hl_ref, xpad, a_scr, b_scr, h_scr, hcar):
    i = pl.program_id(1)

    @pl.when(i == 0)
    def _():
        xpad[0:SUBLANES] = jnp.zeros((SUBLANES, LRU_WIDTH), F32)
        hcar[...] = jnp.zeros(hcar.shape, F32)

    x = x_ref[...]
    mod = mod_ref[0]
    h = _modulated_norm(x, g_ref[...], mod).astype(BF16)
    xb = _dot(h, win_ref[:, 0:LRU_WIDTH])
    gg = _dot(h, win_ref[:, LRU_WIDTH:2 * LRU_WIDTH])
    xpad[SUBLANES:SUBLANES + TL] = xb
    xc = cw_ref[CONV_W - 1:CONV_W] * xb + cb_ref[...]
    for j in range(CONV_W - 1):
        off = SUBLANES - (CONV_W - 1) + j
        xc = xc + cw_ref[j:j + 1] * xpad[off:off + TL]
    tail = xpad[TL + SUBLANES - (CONV_W - 1):TL + SUBLANES]
    conv_ref[0] = tail
    xpad[SUBLANES - (CONV_W - 1):SUBLANES] = tail

    a, b = _lru_gates(xc, wa_ref, ba_ref, wx_ref, bx_ref, lam_ref)
    a_scr[...] = a
    b_scr[...] = b
    row = lax.broadcasted_iota(I32, (SUBLANES, LRU_WIDTH), 0)

    def group(j, hprev):
        r0 = pl.multiple_of(j * SUBLANES, SUBLANES)
        aa = a_scr[pl.ds(r0, SUBLANES), :]
        bb = b_scr[pl.ds(r0, SUBLANES), :]
        d = 1
        while d < SUBLANES:
            a_sh = pltpu.roll(aa, d, axis=0)
            b_sh = pltpu.roll(bb, d, axis=0)
            m = row >= d
            bb = jnp.where(m, aa * b_sh + bb, bb)
            aa = jnp.where(m, aa * a_sh, aa)
            d *= 2
        hh = aa * hprev + bb
        h_scr[pl.ds(r0, SUBLANES), :] = hh
        return jnp.broadcast_to(hh[SUBLANES - 1:SUBLANES, :], (SUBLANES, LRU_WIDTH))

    hlast = lax.fori_loop(0, TL // SUBLANES, group, hcar[...], unroll=2)
    hcar[...] = hlast
    hl_ref[0] = hlast[0:1]
    hs = h_scr[...]
    out = _dot((hs * _silu(gg)).astype(BF16), wout_ref[...])
    y_ref[...] = x + mod[:, 2 * D_MODEL:3 * D_MODEL] * out


def _lru_prompt(x, mod, g, win, cw, cb, wa, ba, wx, bx, lam, wout, *, batch, seq):
    nb = seq // TL
    blk = lambda b, i: (b * nb + i, 0)
    const2 = lambda b, i: (0, 0)
    const3 = lambda b, i: (0, 0, 0)
    per_b = lambda b, i: (b, 0, 0)
    return pl.pallas_call(
        _lru_prompt_body,
        grid=(batch, nb),
        in_specs=[
            pl.BlockSpec((TL, D_MODEL), blk),
            pl.BlockSpec((1, 1, 3 * D_MODEL), per_b),
            pl.BlockSpec((1, D_MODEL), const2),
            pl.BlockSpec((D_MODEL, 2 * LRU_WIDTH), const2),
            pl.BlockSpec((CONV_W, LRU_WIDTH), const2),
            pl.BlockSpec((1, LRU_WIDTH), const2),
            pl.BlockSpec((N_LRU_BLOCKS, LRU_BLOCK, LRU_BLOCK), const3),
            pl.BlockSpec((1, LRU_WIDTH), const2),
            pl.BlockSpec((N_LRU_BLOCKS, LRU_BLOCK, LRU_BLOCK), const3),
            pl.BlockSpec((1, LRU_WIDTH), const2),
            pl.BlockSpec((1, LRU_WIDTH), const2),
            pl.BlockSpec((LRU_WIDTH, D_MODEL), const2),
        ],
        out_specs=[
            pl.BlockSpec((TL, D_MODEL), blk),
            pl.BlockSpec((1, CONV_W - 1, LRU_WIDTH), per_b),
            pl.BlockSpec((1, 1, LRU_WIDTH), per_b),
        ],
        out_shape=[
            jax.ShapeDtypeStruct((batch * seq, D_MODEL), F32),
            jax.ShapeDtypeStruct((batch, CONV_W - 1, LRU_WIDTH), F32),
            jax.ShapeDtypeStruct((batch, 1, LRU_WIDTH), F32),
        ],
        scratch_shapes=[
            pltpu.VMEM((TL + SUBLANES, LRU_WIDTH), F32),
            pltpu.VMEM((TL, LRU_WIDTH), F32),
            pltpu.VMEM((TL, LRU_WIDTH), F32),
            pltpu.VMEM((TL, LRU_WIDTH), F32),
            pltpu.VMEM((SUBLANES, LRU_WIDTH), F32),
        ],
        compiler_params=pltpu.CompilerParams(
            dimension_semantics=("arbitrary", "arbitrary"), vmem_limit_bytes=VMEM_LIMIT),
        name="lru_prompt",
    )(x, mod, g, win, cw, cb, wa, ba, wx, bx, lam, wout)


def _lru_sample_body(o_ref, gs_ref, x_ref, mod0_ref, wout0_ref, mod_ref, g_ref, win_ref, cw_ref, cb_ref,
                     wa_ref, ba_ref, wx_ref, bx_ref, lam_ref, wout_ref, sc_ref, sh_ref,
                     y_ref, conv_ref, hl_ref, *, nb, nt):
    x0 = x_ref[...]
    y0 = _dot((o_ref[...] * gs_ref[...].astype(F32)).astype(BF16), wout0_ref[...])
    x = x0 + mod0_ref[:, 2 * D_MODEL:3 * D_MODEL] * y0
    mod = mod_ref[...]
    h = _modulated_norm(x, g_ref[...], mod).astype(BF16)
    xb = _dot(h, win_ref[:, 0:LRU_WIDTH])
    gg = _dot(h, win_ref[:, LRU_WIDTH:2 * LRU_WIDTH])
    slabs = [sc_ref[j] for j in range(CONV_W - 1)] + [xb[t * nb:(t + 1) * nb] for t in range(nt)]
    xcs = []
    for t in range(nt):
        acc = cb_ref[...] + cw_ref[0:1] * slabs[t]
        for j in range(1, CONV_W):
            acc = acc + cw_ref[j:j + 1] * slabs[t + j]
        xcs.append(acc)
    for j in range(CONV_W - 1):
        conv_ref[j] = slabs[nt + j]
    xc = jnp.concatenate(xcs, axis=0)
    a, b = _lru_gates(xc, wa_ref, ba_ref, wx_ref, bx_ref, lam_ref)
    hprev = sh_ref[...]
    hs = []
    for t in range(nt):
        hprev = a[t * nb:(t + 1) * nb] * hprev + b[t * nb:(t + 1) * nb]
        hs.append(hprev)
    hl_ref[...] = hprev
    out = _dot((jnp.concatenate(hs, axis=0) * _silu(gg)).astype(BF16), wout_ref[...])
    y_ref[...] = x + mod[:, 2 * D_MODEL:3 * D_MODEL] * out


def _lru_sample(o, gs, x, mod0, wout0, mod, g, win, cw, cb, wa, ba, wx, bx, lam, wout, sc, sh, *, nb, nt):
    args = (o, gs, x, mod0, wout0, mod, g, win, cw, cb, wa, ba, wx, bx, lam, wout, sc, sh)

    def full(a):
        nd = a.ndim
        return pl.BlockSpec(a.shape, lambda i, nd=nd: (0,) * nd)

    out_shape = [
        jax.ShapeDtypeStruct((nt * nb, D_MODEL), F32),
        jax.ShapeDtypeStruct((CONV_W - 1, nb, LRU_WIDTH), F32),
        jax.ShapeDtypeStruct((nb, LRU_WIDTH), F32),
    ]
    body = functools.partial(_lru_sample_body, nb=nb, nt=nt)
    return pl.pallas_call(
        body,
        grid=(1,),
        in_specs=[full(a) for a in args],
        out_specs=[pl.BlockSpec(s.shape, lambda i, nd=len(s.shape): (0,) * nd) for s in out_shape],
        out_shape=out_shape,
        compiler_params=pltpu.CompilerParams(
            dimension_semantics=("arbitrary",), vmem_limit_bytes=VMEM_LIMIT),
        name="lru_sample",
    )(*args)


def kernel(x_prompt, x_sample, cache_k, cache_v, cache_idx_k, state_conv, state_h, page_table, c_prompt, c_sample, norm_g, ada_w, ada_b, attn_w_in, attn_q_norm, attn_k_norm, attn_w_out, lru_w_in, lru_conv_w, lru_conv_b, lru_w_a, lru_b_a, lru_w_x, lru_b_x, lru_lam, lru_w_out):
    B, S, _ = x_prompt.shape
    Bd, T, _ = x_sample.shape
    n_pages = page_table.shape[1]
    past = n_pages * PAGE_SIZE
    n_pool = cache_k.shape[1]
    assert S % QB == 0 and S % TL == 0 and n_pages % PAGES_PER_STEP == 0
    assert Bd % SEQ_PER_STEP == 0 and SEQ_PER_STEP * T == SUBLANES

    mod = _ada(jnp.concatenate([c_prompt, c_sample], axis=0), ada_w, ada_b)
    mod_p = [mod[l, :B].reshape(B, 1, 3 * D_MODEL) for l in range(2)]
    mod_s = [jnp.repeat(mod[l, B:], T, axis=0) for l in range(2)]

    w = attn_w_in[0]
    o_k, o_v, o_qi, o_ki, o_wi, o_g = 1024, 1280, 1536, 2560, 2624, 2640
    wcat = jnp.concatenate(
        [w[:, :o_ki], w[:, o_g:], w[:, o_ki:o_g], jnp.zeros((D_MODEL, PROJ_WIDTH - w.shape[1]), F32)],
        axis=1).astype(BF16)
    g0 = norm_g[0].reshape(1, D_MODEL)
    qn = attn_q_norm[0].reshape(1, HEAD_DIM)
    kn = attn_k_norm[0].reshape(1, HEAD_DIM)
    wout0 = attn_w_out[0].astype(BF16)

    tm = 512
    tabs_p = _rope_tables(jnp.arange(S))
    xp = x_prompt.reshape(B * S, D_MODEL)
    q, k, v, qi, kw, ki, gs = _proj(xp, mod_p[0], g0, wcat, qn, kn, tabs_p,
                                    tm=tm, rows_per_mod=S, tab_blocks=S // tm)
    x1p = _attn_prompt(q, qi, kw, k, v, gs, xp, mod_p[0], wout0, batch=B, seq=S)

    pos_s = past + jnp.arange(T)
    tabs_s = jnp.tile(_rope_tables(pos_s), (1, Bd, 1))
    xs = x_sample.reshape(Bd * T, D_MODEL)
    q2, k2, v2, qi2, kw2, ki2, gs2 = _proj(xs, mod_s[0].reshape(1, Bd * T, 3 * D_MODEL), g0, wcat, qn, kn,
                                           tabs_s, tm=Bd * T, rows_per_mod=Bd * T, tab_blocks=1)
    qir = qi2.reshape(Bd, T, N_IDX_HEADS, IDX_DIM).transpose(0, 2, 1, 3).reshape(Bd, N_IDX_HEADS * T, IDX_DIM)
    wcol = kw2[:, IDX_DIM:IDX_DIM + N_IDX_HEADS].reshape(Bd, T, N_IDX_HEADS).transpose(0, 2, 1)
    wcol = wcol.reshape(Bd, N_IDX_HEADS * T, 1)
    pad_rows = lambda a: jnp.pad(a, ((0, 0), (0, PAGE_SIZE - T)) + ((0, 0),) * (a.ndim - 2))
    ki_new = pad_rows(ki2.reshape(Bd, T, IDX_DIM)).transpose(0, 2, 1)
    cache_ik_t = cache_idx_k.reshape(n_pool, PAGE_SIZE, IDX_DIM).transpose(0, 2, 1)
    scores = _sample_scores(page_table, qir, wcol, cache_ik_t, ki_new,
                            dec_batch=Bd, dec_seq=T, n_pages=n_pages)
    sel = _sample_select(scores, past=past, dec_seq=T)
    qs = q2.reshape(Bd, T, N_KV_HEADS, GROUP, HEAD_DIM).transpose(0, 2, 3, 1, 4)
    qs = qs.reshape(Bd, N_KV_HEADS * GROUP * T, HEAD_DIM)
    new_pad = ((0, 0), (0, (PAGE_SIZE - T) * N_KV_HEADS), (0, 0))
    k_new = jnp.pad(k2.reshape(Bd, T * N_KV_HEADS, HEAD_DIM), new_pad)
    v_new = jnp.pad(v2.reshape(Bd, T * N_KV_HEADS, HEAD_DIM), new_pad)
    sel = jnp.repeat(sel, N_KV_HEADS, axis=1)
    o2 = _sample_attn(page_table, qs, sel,
                      cache_k.reshape(n_pool, PAGE_SIZE * N_KV_HEADS, HEAD_DIM),
                      cache_v.reshape(n_pool, PAGE_SIZE * N_KV_HEADS, HEAD_DIM),
                      k_new, v_new, dec_batch=Bd, dec_seq=T, n_pages=n_pages)
    o2 = o2.reshape(Bd, N_KV_HEADS, GROUP, T, HEAD_DIM).transpose(3, 0, 1, 2, 4).reshape(T * Bd, ATTN_WIDTH)

    g1 = norm_g[1].reshape(1, D_MODEL)
    win = lru_w_in[0].astype(BF16)
    cw = lru_conv_w[0]
    cb = lru_conv_b[0].reshape(1, LRU_WIDTH)
    wa = lru_w_a[0].astype(BF16)
    wx = lru_w_x[0].astype(BF16)
    ba = lru_b_a[0].reshape(1, LRU_WIDTH)
    bx = lru_b_x[0].reshape(1, LRU_WIDTH)
    lam = lru_lam[0].reshape(1, LRU_WIDTH)
    wout1 = lru_w_out[0].astype(BF16)

    yp, conv_p, h_p = _lru_prompt(x1p, mod_p[1], g1, win, cw, cb, wa, ba, wx, bx, lam, wout1, batch=B, seq=S)

    tmaj = lambda a: a.reshape(Bd, T, -1).transpose(1, 0, 2).reshape(T * Bd, -1)
    ys, conv_s, h_s = _lru_sample(
        o2, tmaj(gs2), tmaj(xs), tmaj(mod_s[0]), wout0, tmaj(mod_s[1]), g1, win, cw, cb, wa, ba, wx, bx,
        lam, wout1, state_conv[0].transpose(1, 0, 2), state_h[0], nb=Bd, nt=T)

    y_prompt = yp.reshape(B, S, D_MODEL)
    y_sample = ys.reshape(T, Bd, D_MODEL).transpose(1, 0, 2)
    return (y_prompt, y_sample,
            k.reshape(1, B, S, N_KV_HEADS, HEAD_DIM), v.reshape(1, B, S, N_KV_HEADS, HEAD_DIM),
            ki.reshape(1, B, S, IDX_DIM),
            k2.reshape(1, Bd, T, N_KV_HEADS, HEAD_DIM), v2.reshape(1, Bd, T, N_KV_HEADS, HEAD_DIM),
            ki2.reshape(1, Bd, T, IDX_DIM),
            conv_p.reshape(1, B, CONV_W - 1, LRU_WIDTH), h_p.reshape(1, B, LRU_WIDTH),
            conv_s.transpose(1, 0, 2).reshape(1, Bd, CONV_W - 1, LRU_WIDTH), h_s.reshape(1, Bd, LRU_WIDTH))
```

```python
import functools
import math

import jax
import jax.numpy as jnp
from jax import lax
from jax.experimental import pallas as pl
from jax.experimental.pallas import tpu as pltpu

F32 = jnp.float32
BF16 = jnp.bfloat16
I32 = jnp.int32

D_MODEL = 1024
N_HEADS = 8
N_KV_HEADS = 2
HEAD_DIM = 128
GROUP = N_HEADS // N_KV_HEADS
ATTN_WIDTH = N_HEADS * HEAD_DIM
KV_WIDTH = N_KV_HEADS * HEAD_DIM
N_IDX_HEADS = 16
IDX_DIM = 64
IDX_WIDTH = N_IDX_HEADS * IDX_DIM
TOPK_MAX = 256
ROPE_THETA = 500000.0
ROT_FRAC = 4
PAGE_SIZE = 128
LRU_WIDTH = D_MODEL
N_LRU_BLOCKS = 4
LRU_BLOCK = LRU_WIDTH // N_LRU_BLOCKS
CONV_W = 4
LRU_C = 8.0
EPS = 1e-6

LANES = 128
SUBLANES = 8
MXU_COLS = 256
VMEM_LIMIT = 56 * 1024 * 1024

OFF_Q = 0
OFF_K = OFF_Q + ATTN_WIDTH
OFF_V = OFF_K + KV_WIDTH
OFF_QI = OFF_V + KV_WIDTH
OFF_G = OFF_QI + IDX_WIDTH
OFF_KW = OFF_G + ATTN_WIDTH
PROJ_WIDTH = OFF_KW + LANES

INT_MIN = -(2 ** 31)
KEY_NEG_INF = 0x807FFFFF - 2 ** 32
RADIX_UNROLL = 4
RADIX_TRIPS = 32 // RADIX_UNROLL
COUNT_SLAB_ROWS = 64
LOG2E = 1.4426950408889634
NT_DIMS = (((1,), (1,)), ((), ()))


def _dot(a, b):
    return jnp.dot(a, b, preferred_element_type=F32)


def _dot_nt(a, b):
    return lax.dot_general(a, b, NT_DIMS, preferred_element_type=F32)


def _silu(x):
    return x / (1.0 + jnp.exp(-x))


def _sigmoid(x):
    return 1.0 / (1.0 + jnp.exp(-x))


def _rmsnorm(x, g):
    return x * lax.rsqrt(jnp.mean(x * x, axis=-1, keepdims=True) + EPS) * g


def _modulated_norm(x, g, mod):
    shift = mod[:, 0:D_MODEL]
    scale = mod[:, D_MODEL:2 * D_MODEL]
    return _rmsnorm(x, g) * (1.0 + scale) + shift


def _rope(y, tabs_ref, base, half):
    c = tabs_ref[base]
    s1 = tabs_ref[base + 1]
    s2 = tabs_ref[base + 2]
    return y * c + pltpu.roll(y, LANES - half, axis=1) * s1 + pltpu.roll(y, half, axis=1) * s2


def _key_to_f32(key):
    bits = jnp.where(key >= 0, key, key ^ 0x7FFFFFFF)
    return pltpu.bitcast(bits, F32)


def _kth_largest_key(sc_ref, n, k, axis, window=None, side_work=None, side_trips=0):
    kf = float(k)
    shape = list(sc_ref.shape)
    if window is None:
        window = slice(0, shape[1 - axis])
    shape[1 - axis] = window.stop - window.start
    shape[axis] = 1

    slab = COUNT_SLAB_ROWS if axis == 0 else LANES
    assert n % slab == 0

    def body(it, prefix):
        cand = prefix + lax.shift_left(jnp.int32(1), 31 - it)
        cand_f = _key_to_f32(cand)
        parts = []
        for j in range(n // slab):
            span = slice(j * slab, (j + 1) * slab)
            sc = sc_ref[span, window] if axis == 0 else sc_ref[window, span]
            parts.append(jnp.where(sc >= cand_f, 1.0, 0.0))
        while len(parts) > 1:
            nxt = [parts[a] + parts[a + 1] for a in range(0, len(parts) - 1, 2)]
            if len(parts) % 2:
                nxt.append(parts[-1])
            parts = nxt
        cnt = jnp.sum(parts[0], axis=axis, keepdims=True)
        return jnp.where(cnt >= kf, cand, prefix)

    def trip(with_side, t, prefix):
        for e in range(RADIX_UNROLL):
            prefix = body(t * RADIX_UNROLL + e, prefix)
        if with_side:
            side_work(t)
        return prefix

    assert side_trips <= RADIX_TRIPS
    prefix = jnp.full(tuple(shape), INT_MIN, I32)
    if side_trips:
        prefix = lax.fori_loop(0, side_trips, functools.partial(trip, True), prefix)
    return lax.fori_loop(side_trips, RADIX_TRIPS, functools.partial(trip, False), prefix)


def _ada_body(c_ref, w_ref, b_ref, o_ref):
    s = _silu(c_ref[...]).astype(BF16)
    o_ref[0] = _dot(s, w_ref[0].astype(BF16)) + b_ref[0]


def _ada(c_all, ada_w, ada_b):
    rows = c_all.shape[0]
    depth = ada_w.shape[0]
    nblk = 3
    return pl.pallas_call(
        _ada_body,
        grid=(depth, nblk),
        in_specs=[
            pl.BlockSpec((rows, D_MODEL), lambda l, j: (0, 0)),
            pl.BlockSpec((1, D_MODEL, D_MODEL), lambda l, j: (l, 0, j)),
            pl.BlockSpec((1, 1, D_MODEL), lambda l, j: (l, 0, j)),
        ],
        out_specs=pl.BlockSpec((1, rows, D_MODEL), lambda l, j: (l, 0, j)),
        out_shape=jax.ShapeDtypeStruct((depth, rows, 3 * D_MODEL), F32),
        compiler_params=pltpu.CompilerParams(
            dimension_semantics=("arbitrary", "arbitrary"), vmem_limit_bytes=VMEM_LIMIT),
        name="ada_mod",
    )(c_all, ada_w, ada_b.reshape(depth, 1, 3 * D_MODEL))


def _proj_body(x_ref, mod_ref, g_ref, w_ref, qn_ref, kn_ref, tabs_ref,
               q_ref, k_ref, v_ref, qi_ref, kw_ref, ki_ref, gs_ref):
    h = _modulated_norm(x_ref[...], g_ref[...], mod_ref[0]).astype(BF16)
    qn = qn_ref[...]
    kn = kn_ref[...]
    half_h = HEAD_DIM // ROT_FRAC // 2
    half_i = IDX_DIM // ROT_FRAC // 2

    def slabs(off, n_slabs):
        for c0 in range(0, n_slabs, MXU_COLS // LANES):
            z = _dot(h, w_ref[:, off + c0 * LANES:off + c0 * LANES + MXU_COLS])
            for e in range(MXU_COLS // LANES):
                yield c0 + e, z[:, e * LANES:(e + 1) * LANES]

    for hh, z in slabs(OFF_Q, N_HEADS):
        q_ref[:, hh * HEAD_DIM:(hh + 1) * HEAD_DIM] = _rope(_rmsnorm(z, qn), tabs_ref, 0, half_h).astype(BF16)
    tm = x_ref.shape[0]
    for hh, z in slabs(OFF_K, N_KV_HEADS):
        k_ref[pl.ds(hh, tm, stride=N_KV_HEADS), :] = _rope(_rmsnorm(z, kn), tabs_ref, 0, half_h)
    for hh, z in slabs(OFF_V, N_KV_HEADS):
        v_ref[pl.ds(hh, tm, stride=N_KV_HEADS), :] = z
    for c, z in slabs(OFF_QI, IDX_WIDTH // LANES):
        qi_ref[:, c * LANES:(c + 1) * LANES] = _rope(z, tabs_ref, 3, half_i).astype(BF16)
    for c, z in slabs(OFF_G, ATTN_WIDTH // LANES):
        gs_ref[:, c * LANES:(c + 1) * LANES] = _silu(z).astype(BF16)
    kw = _rope(_dot(h, w_ref[:, OFF_KW:OFF_KW + LANES]), tabs_ref, 6, half_i)
    kw_ref[...] = kw
    ki_ref[...] = kw[:, 0:IDX_DIM]


def _proj(x, mod, g, w, qn, kn, tabs, *, tm, rows_per_mod, tab_blocks):
    n = x.shape[0]
    mod_rows = mod.shape[1]
    grid = (n // tm,)
    row = lambda i: (i, 0)
    outs = [
        jax.ShapeDtypeStruct((n, ATTN_WIDTH), BF16),
        jax.ShapeDtypeStruct((n * N_KV_HEADS, HEAD_DIM), F32),
        jax.ShapeDtypeStruct((n * N_KV_HEADS, HEAD_DIM), F32),
        jax.ShapeDtypeStruct((n, IDX_WIDTH), BF16),
        jax.ShapeDtypeStruct((n, LANES), F32),
        jax.ShapeDtypeStruct((n, IDX_DIM), F32),
        jax.ShapeDtypeStruct((n, ATTN_WIDTH), BF16),
    ]
    return pl.pallas_call(
        _proj_body,
        grid=grid,
        in_specs=[
            pl.BlockSpec((tm, D_MODEL), row),
            pl.BlockSpec((1, mod_rows, 3 * D_MODEL), lambda i: (i * tm // rows_per_mod, 0, 0)),
            pl.BlockSpec((1, D_MODEL), lambda i: (0, 0)),
            pl.BlockSpec((D_MODEL, PROJ_WIDTH), lambda i: (0, 0)),
            pl.BlockSpec((1, HEAD_DIM), lambda i: (0, 0)),
            pl.BlockSpec((1, HEAD_DIM), lambda i: (0, 0)),
            pl.BlockSpec((9, tm, LANES), lambda i: (0, i % tab_blocks, 0)),
        ],
        out_specs=[
            pl.BlockSpec((tm, ATTN_WIDTH), row),
            pl.BlockSpec((tm * N_KV_HEADS, HEAD_DIM), row),
            pl.BlockSpec((tm * N_KV_HEADS, HEAD_DIM), row),
            pl.BlockSpec((tm, IDX_WIDTH), row),
            pl.BlockSpec((tm, LANES), row),
            pl.BlockSpec((tm, IDX_DIM), row),
            pl.BlockSpec((tm, ATTN_WIDTH), row),
        ],
        out_shape=outs,
        compiler_params=pltpu.CompilerParams(
            dimension_semantics=("arbitrary",), vmem_limit_bytes=VMEM_LIMIT),
        name="attn_proj",
    )(x, mod, g, w, qn, kn, tabs)


def _rope_tables(pos):
    posf = pos.astype(F32)
    t = pos.shape[0]

    def base(d):
        r = d // ROT_FRAC
        half = r // 2
        inv = jnp.exp(-jnp.log(jnp.asarray(ROPE_THETA, F32)) * jnp.arange(half, dtype=F32) * 2.0 / r)
        ang = posf[:, None] * inv[None, :]
        cos = jnp.cos(ang)
        sin = jnp.sin(ang)
        c = jnp.concatenate([cos, cos, jnp.ones((t, d - r), F32)], axis=1)
        s1 = jnp.concatenate([-sin, jnp.zeros((t, d - half), F32)], axis=1)
        s2 = jnp.concatenate([jnp.zeros((t, half), F32), sin, jnp.zeros((t, d - r), F32)], axis=1)
        return c, s1, s2

    hc, hs1, hs2 = base(HEAD_DIM)
    ic, is1, is2 = base(IDX_DIM)
    wi_scale = N_IDX_HEADS ** -0.5 * IDX_DIM ** -0.5
    pad = LANES - IDX_DIM
    kc = jnp.concatenate([ic, jnp.full((t, N_IDX_HEADS), wi_scale, F32),
                          jnp.zeros((t, pad - N_IDX_HEADS), F32)], axis=1)
    ks1 = jnp.concatenate([is1, jnp.zeros((t, pad), F32)], axis=1)
    ks2 = jnp.concatenate([is2, jnp.zeros((t, pad), F32)], axis=1)
    two = lambda a: jnp.concatenate([a, a], axis=1)
    return jnp.stack([hc, hs1, hs2, two(ic), two(is1), two(is2), kc, ks1, ks2])


QB = 128
SUB_BLOCKS = 1
KEY_BUCKET = 256


def _attn_prompt_body(q_ref, qi_ref, kwb_ref, kws_ref, k_ref, v_ref, gs_ref, x_ref, gate_ref,
                      wout_ref, o_ref, kbf, vbf, kipar, sct_ref, s_ref, *, seq, k_sel):
    i = pl.program_id(1)
    assert SUB_BLOCKS == 1

    @pl.when(i == 0)
    def _():
        ones_col = jnp.where(lax.broadcasted_iota(I32, (seq, HEAD_DIM), 1) == 0, 1.0, 0.0).astype(BF16)
        for kh in range(N_KV_HEADS):
            kbf[kh] = k_ref[pl.ds(kh, seq, stride=N_KV_HEADS), :].astype(BF16)
            vbf[kh, :, 0:HEAD_DIM] = v_ref[pl.ds(kh, seq, stride=N_KV_HEADS), :].astype(BF16)
            vbf[kh, :, HEAD_DIM:2 * HEAD_DIM] = ones_col
        kw = kws_ref[...]
        lane = lax.broadcasted_iota(I32, kw.shape, 1)
        ke = jnp.where(lane < IDX_DIM, kw, 0.0)
        kipar[0] = ke.astype(BF16)
        kipar[1] = pltpu.roll(ke, IDX_DIM, axis=1).astype(BF16)

    def stacked_q(rows, kh):
        return jnp.concatenate(
            [q_ref[rows, (kh * GROUP + g) * HEAD_DIM:(kh * GROUP + g + 1) * HEAD_DIM] for g in range(GROUP)],
            axis=0)

    def select(sb, nk):
        rows = slice(sb * QB, (sb + 1) * QB)
        first = (i * SUB_BLOCKS + sb) * QB
        col = lax.broadcasted_iota(I32, (QB, nk), 1)
        pos = first + lax.broadcasted_iota(I32, (QB, nk), 0)
        if nk <= k_sel:
            return jnp.where(col <= pos, 0.0, -jnp.inf)
        kwb = kwb_ref[rows, :]
        score = None
        for p in range(N_IDX_HEADS // 2):
            pair = None
            for par in range(2):
                h = 2 * p + par
                d = _dot_nt(qi_ref[rows, p * LANES:(p + 1) * LANES], kipar[par, 0:nk, :])
                term = kwb[:, IDX_DIM + h:IDX_DIM + h + 1] * jnp.maximum(d, 0.0)
                pair = term if pair is None else pair + term
            score = pair if score is None else score + pair
        score = jnp.where(col <= pos, score, -jnp.inf)
        sct_ref[0:nk, rows] = score.T

        def qk_tile(t):
            off = pl.multiple_of(t * MXU_COLS, MXU_COLS)
            for kh in range(N_KV_HEADS):
                s_ref[kh, :, pl.ds(off, MXU_COLS)] = _dot_nt(stacked_q(rows, kh), kbf[kh, pl.ds(off, MXU_COLS), :])

        assert nk % MXU_COLS == 0
        thr = _kth_largest_key(sct_ref, nk, k_sel, 0, window=rows,
                               side_work=qk_tile, side_trips=nk // MXU_COLS)
        key_row = lax.broadcasted_iota(I32, (nk, QB), 0)
        q_pos = first + lax.broadcasted_iota(I32, (nk, QB), 1)
        sel_t = jnp.logical_and(
            key_row <= q_pos,
            jnp.logical_or(sct_ref[0:nk, rows] >= _key_to_f32(thr), thr <= KEY_NEG_INF))
        return jnp.where(sel_t, 0.0, -jnp.inf).T

    def attend(sb, nk, bias):
        rows = slice(sb * QB, (sb + 1) * QB)
        c = HEAD_DIM ** -0.5 * LOG2E
        heads = [None] * N_HEADS
        for kh in range(N_KV_HEADS):
            if nk <= k_sel:
                s = _dot_nt(stacked_q(rows, kh), kbf[kh, 0:nk, :])
            else:
                s = s_ref[kh, :, 0:nk]
            ps = []
            for g in range(GROUP):
                sg = s[g * QB:(g + 1) * QB] + bias
                ps.append(jnp.exp2((sg - jnp.max(sg, axis=1, keepdims=True)) * c).astype(BF16))
            o = _dot(jnp.concatenate(ps, axis=0), vbf[kh, 0:nk, :])
            for g in range(GROUP):
                og = o[g * QB:(g + 1) * QB]
                heads[kh * GROUP + g] = og[:, 0:HEAD_DIM] / og[:, HEAD_DIM:HEAD_DIM + 1]
        attn = jnp.concatenate(heads, axis=1)
        y = _dot((attn * gs_ref[rows, :].astype(F32)).astype(BF16), wout_ref[...])
        o_ref[rows, :] = x_ref[rows, :] + gate_ref[0] * y

    def block(nk):
        biases = [select(sb, nk) for sb in range(SUB_BLOCKS)]
        for sb in range(SUB_BLOCKS):
            attend(sb, nk, biases[sb])

    steps_per_bucket = KEY_BUCKET // (SUB_BLOCKS * QB)
    for bucket in range(seq // KEY_BUCKET):
        pl.when(i // steps_per_bucket == bucket)(functools.partial(block, (bucket + 1) * KEY_BUCKET))


def _attn_prompt(q, qi, kw, k, v, gs, x, mod, wout, *, batch, seq):
    step = SUB_BLOCKS * QB
    nb = seq // step
    k_sel = min(TOPK_MAX, seq // 4)
    blk = lambda b, i: (b * nb + i, 0)
    whole = lambda b, i: (b, 0)
    body = functools.partial(_attn_prompt_body, seq=seq, k_sel=k_sel)
    return pl.pallas_call(
        body,
        grid=(batch, nb),
        in_specs=[
            pl.BlockSpec((step, ATTN_WIDTH), blk),
            pl.BlockSpec((step, IDX_WIDTH), blk),
            pl.BlockSpec((step, LANES), blk),
            pl.BlockSpec((seq, LANES), whole),
            pl.BlockSpec((seq * N_KV_HEADS, HEAD_DIM), whole),
            pl.BlockSpec((seq * N_KV_HEADS, HEAD_DIM), whole),
            pl.BlockSpec((step, ATTN_WIDTH), blk),
            pl.BlockSpec((step, D_MODEL), blk),
            pl.BlockSpec((1, 1, D_MODEL), lambda b, i: (b, 0, 2)),
            pl.BlockSpec((ATTN_WIDTH, D_MODEL), lambda b, i: (0, 0)),
        ],
        out_specs=pl.BlockSpec((step, D_MODEL), blk),
        out_shape=jax.ShapeDtypeStruct((batch * seq, D_MODEL), F32),
        scratch_shapes=[
            pltpu.VMEM((N_KV_HEADS, seq, HEAD_DIM), BF16),
            pltpu.VMEM((N_KV_HEADS, seq, 2 * HEAD_DIM), BF16),
            pltpu.VMEM((2, seq, LANES), BF16),
            pltpu.VMEM((seq, step), F32),
            pltpu.VMEM((N_KV_HEADS, GROUP * QB, seq), F32),
        ],
        compiler_params=pltpu.CompilerParams(
            dimension_semantics=("arbitrary", "arbitrary"), vmem_limit_bytes=VMEM_LIMIT),
        name="attn_prompt",
    )(q, qi, kw, kw, k, v, gs, x, mod, wout)


PAGES_PER_STEP = 16
SCORE_PAGES_PER_STEP = 32
SEQ_PER_STEP = 2


def _sample_tile_score(qir, wcol, kpage_t):
    d = _dot(qir, kpage_t.astype(BF16))
    r = wcol * jnp.maximum(d, 0.0)
    acc = r[0:SUBLANES]
    for j in range(1, r.shape[0] // SUBLANES):
        acc = acc + r[j * SUBLANES:(j + 1) * SUBLANES]
    half = SUBLANES // 2
    return acc[0:half] + acc[half:SUBLANES]


def _sample_score_body(pt_ref, qir_ref, wcol_ref, *rest, n_chunks, dec_seq):
    per_seq = SCORE_PAGES_PER_STEP
    npg = SEQ_PER_STEP * per_seq
    pages = rest[:npg]
    knew_ref = rest[npg]
    o_ref = rest[npg + 1]
    c = pl.program_id(1)

    @pl.when(c < n_chunks)
    def _():
        for s in range(SEQ_PER_STEP):
            kcat = jnp.concatenate([pages[s * per_seq + j][0] for j in range(per_seq)], axis=1)
            o_ref[s * dec_seq:(s + 1) * dec_seq, :] = _sample_tile_score(qir_ref[s], wcol_ref[s], kcat)

    @pl.when(c == n_chunks)
    def _():
        o_ref[...] = jnp.zeros(o_ref.shape, F32)
        for s in range(SEQ_PER_STEP):
            o_ref[s * dec_seq:(s + 1) * dec_seq, 0:PAGE_SIZE] = _sample_tile_score(
                qir_ref[s], wcol_ref[s], knew_ref[s])


def _sample_scores(page_table, qir, wcol, cache_ik, knew, *, dec_batch, dec_seq, n_pages):
    per_seq = SCORE_PAGES_PER_STEP
    assert n_pages % per_seq == 0 and per_seq % PAGES_PER_STEP == 0
    n_chunks = n_pages // per_seq
    width = (n_chunks + 1) * per_seq * PAGE_SIZE
    rows = IDX_DIM

    def page_map(s, j):
        def f(bp, c, pt):
            cc = jnp.minimum(c, n_chunks - 1)
            return (pt[bp * SEQ_PER_STEP + s, cc * per_seq + j], 0, 0)
        return f

    in_specs = [
        pl.BlockSpec((SEQ_PER_STEP, rows, IDX_DIM), lambda bp, c, pt: (bp, 0, 0)),
        pl.BlockSpec((SEQ_PER_STEP, rows, 1), lambda bp, c, pt: (bp, 0, 0)),
    ]
    for s in range(SEQ_PER_STEP):
        for j in range(per_seq):
            in_specs.append(pl.BlockSpec((1, IDX_DIM, PAGE_SIZE), page_map(s, j)))
    in_specs.append(pl.BlockSpec((SEQ_PER_STEP, IDX_DIM, PAGE_SIZE), lambda bp, c, pt: (bp, 0, 0)))
    body = functools.partial(_sample_score_body, n_chunks=n_chunks, dec_seq=dec_seq)
    return pl.pallas_call(
        body,
        grid_spec=pltpu.PrefetchScalarGridSpec(
            num_scalar_prefetch=1,
            grid=(dec_batch // SEQ_PER_STEP, n_chunks + 1),
            in_specs=in_specs,
            out_specs=pl.BlockSpec((SEQ_PER_STEP * dec_seq, per_seq * PAGE_SIZE),
                                   lambda bp, c, pt: (bp, c)),
        ),
        out_shape=jax.ShapeDtypeStruct((dec_batch * dec_seq, width), F32),
        compiler_params=pltpu.CompilerParams(
            dimension_semantics=("arbitrary", "arbitrary"), vmem_limit_bytes=VMEM_LIMIT),
        name="sample_scores",
    )(page_table, qir, wcol, *([cache_ik] * (SEQ_PER_STEP * per_seq)), knew)


SELECT_ROWS = 32


def _sample_select_body(sc_ref, sel_ref, key_ref, *, past, dec_seq, k_sel):
    shape = sc_ref.shape
    col = lax.broadcasted_iota(I32, shape, 1)
    t = lax.broadcasted_iota(I32, shape, 0) % dec_seq
    adm = (col - past) <= t
    key_ref[...] = jnp.where(adm, sc_ref[...], -jnp.inf)
    thr = _kth_largest_key(key_ref, shape[1], k_sel, 1)
    picked = jnp.logical_or(key_ref[...] >= _key_to_f32(thr), thr <= KEY_NEG_INF)
    sel_ref[...] = jnp.where(jnp.logical_and(picked, adm), 1.0, 0.0)


def _sample_select(scores, *, past, dec_seq):
    k_sel = min(TOPK_MAX, (past + dec_seq) // 4)
    body = functools.partial(_sample_select_body, past=past, dec_seq=dec_seq, k_sel=k_sel)
    rows, width = scores.shape
    assert rows % SELECT_ROWS == 0 and SELECT_ROWS % dec_seq == 0
    return pl.pallas_call(
        body,
        grid=(rows // SELECT_ROWS,),
        in_specs=[pl.BlockSpec((SELECT_ROWS, width), lambda i: (i, 0))],
        out_specs=pl.BlockSpec((SELECT_ROWS, width), lambda i: (i, 0)),
        out_shape=jax.ShapeDtypeStruct(scores.shape, F32),
        scratch_shapes=[pltpu.VMEM((SELECT_ROWS, width), F32)],
        compiler_params=pltpu.CompilerParams(
            dimension_semantics=("arbitrary",), vmem_limit_bytes=VMEM_LIMIT),
        name="sample_select",
    )(scores)


def _sample_attn_body(pt_ref, q_ref, sel_ref, *rest, n_chunks, dec_seq):
    npg = SEQ_PER_STEP * PAGES_PER_STEP
    kpages = rest[:npg]
    vpages = rest[npg:2 * npg]
    knew_ref, vnew_ref, o_ref, m_ref, l_ref, acc_ref = rest[2 * npg:]
    c = pl.program_id(1)
    scale = HEAD_DIM ** -0.5
    rows_per_head = GROUP * dec_seq
    rows_per_head_log2 = rows_per_head.bit_length() - 1
    assert rows_per_head == 1 << rows_per_head_log2 and N_KV_HEADS & (N_KV_HEADS - 1) == 0

    @pl.when(c == 0)
    def _():
        m_ref[...] = jnp.full(m_ref.shape, -jnp.inf, F32)
        l_ref[...] = jnp.zeros(l_ref.shape, F32)
        acc_ref[...] = jnp.zeros(acc_ref.shape, F32)

    def update(tiles, finish):
        m_old = [m_ref[s] for s in range(SEQ_PER_STEP)]
        l_old = [l_ref[s] for s in range(SEQ_PER_STEP)]
        a_old = [acc_ref[s] for s in range(SEQ_PER_STEP)]
        m_out, l_out, a_out = [], [], []
        for s, (kt, vt, sel) in enumerate(tiles):
            sc = _dot_nt(q_ref[s], kt) * scale
            key_head = jnp.bitwise_and(lax.broadcasted_iota(I32, sc.shape, 1), N_KV_HEADS - 1)
            row_head = jnp.right_shift(lax.broadcasted_iota(I32, sc.shape, 0), rows_per_head_log2)
            picked = jnp.concatenate([sel] * (N_KV_HEADS * GROUP), axis=0) > 0.5
            sc = jnp.where(jnp.logical_and(picked, key_head == row_head), sc, -jnp.inf)
            m_new = jnp.maximum(m_old[s], jnp.max(sc, axis=1, keepdims=True))
            m_safe = jnp.where(m_new == -jnp.inf, 0.0, m_new)
            alpha = jnp.exp(m_old[s] - m_safe)
            p = jnp.exp(sc - m_safe)
            m_out.append(m_new)
            l_out.append(alpha * l_old[s] + jnp.sum(p, axis=1, keepdims=True))
            a_out.append(alpha * a_old[s] + _dot(p.astype(BF16), vt))
        for s in range(SEQ_PER_STEP):
            if finish:
                o_ref[s] = a_out[s] / l_out[s]
            else:
                m_ref[s] = m_out[s]
                l_ref[s] = l_out[s]
                acc_ref[s] = a_out[s]

    def page_rows(page_refs, s):
        return jnp.concatenate(
            [page_refs[s * PAGES_PER_STEP + j][0].astype(BF16) for j in range(PAGES_PER_STEP)], axis=0)

    @pl.when(c < n_chunks)
    def _():
        update([(page_rows(kpages, s), page_rows(vpages, s), sel_ref[s * dec_seq:(s + 1) * dec_seq, :])
                for s in range(SEQ_PER_STEP)], False)

    @pl.when(c == n_chunks)
    def _():
        new_rows = PAGE_SIZE * N_KV_HEADS
        update([(knew_ref[s].astype(BF16), vnew_ref[s].astype(BF16),
                 sel_ref[s * dec_seq:(s + 1) * dec_seq, 0:new_rows]) for s in range(SEQ_PER_STEP)], True)


def _sample_attn(page_table, qs, sel, cache_k2, cache_v2, knew, vnew, *, dec_batch, dec_seq, n_pages):
    n_chunks = n_pages // PAGES_PER_STEP
    rows = N_KV_HEADS * GROUP * dec_seq
    page_rows = PAGE_SIZE * N_KV_HEADS

    def page_map(s, j):
        def f(bp, c, pt):
            cc = jnp.minimum(c, n_chunks - 1)
            return (pt[bp * SEQ_PER_STEP + s, cc * PAGES_PER_STEP + j], 0, 0)
        return f

    page_specs = []
    for s in range(SEQ_PER_STEP):
        for j in range(PAGES_PER_STEP):
            page_specs.append(pl.BlockSpec((1, page_rows, HEAD_DIM), page_map(s, j)))
    new_spec = pl.BlockSpec((SEQ_PER_STEP, page_rows, HEAD_DIM), lambda bp, c, pt: (bp, 0, 0))
    in_specs = [
        pl.BlockSpec((SEQ_PER_STEP, rows, HEAD_DIM), lambda bp, c, pt: (bp, 0, 0)),
        pl.BlockSpec((SEQ_PER_STEP * dec_seq, PAGES_PER_STEP * page_rows), lambda bp, c, pt: (bp, c)),
    ] + page_specs + page_specs + [new_spec, new_spec]
    npg = SEQ_PER_STEP * PAGES_PER_STEP
    body = functools.partial(_sample_attn_body, n_chunks=n_chunks, dec_seq=dec_seq)
    return pl.pallas_call(
        body,
        grid_spec=pltpu.PrefetchScalarGridSpec(
            num_scalar_prefetch=1,
            grid=(dec_batch // SEQ_PER_STEP, n_chunks + 1),
            in_specs=in_specs,
            out_specs=pl.BlockSpec((SEQ_PER_STEP, rows, HEAD_DIM), lambda bp, c, pt: (bp, 0, 0)),
            scratch_shapes=[
                pltpu.VMEM((SEQ_PER_STEP, rows, 1), F32),
                pltpu.VMEM((SEQ_PER_STEP, rows, 1), F32),
                pltpu.VMEM((SEQ_PER_STEP, rows, HEAD_DIM), F32),
            ],
        ),
        out_shape=jax.ShapeDtypeStruct((dec_batch, rows, HEAD_DIM), F32),
        compiler_params=pltpu.CompilerParams(
            dimension_semantics=("arbitrary", "arbitrary"), vmem_limit_bytes=VMEM_LIMIT),
        name="sample_attn",
    )(page_table, qs, sel, *([cache_k2] * npg), *([cache_v2] * npg), knew, vnew)


def _log_sigmoid(x):
    return jnp.minimum(x, 0.0) - jnp.log1p(jnp.exp(-jnp.abs(x)))


def _lru_gates(xc, wa_ref, ba_ref, wx_ref, bx_ref, lam_ref):
    xcb = xc.astype(BF16)
    ra = []
    ia = []
    for n in range(N_LRU_BLOCKS):
        blk = xcb[:, n * LRU_BLOCK:(n + 1) * LRU_BLOCK]
        ra.append(_dot(blk, wa_ref[n]))
        ia.append(_dot(blk, wx_ref[n]))
    r = _sigmoid(jnp.concatenate(ra, axis=1) + ba_ref[...])
    ig = _sigmoid(jnp.concatenate(ia, axis=1) + bx_ref[...])
    log_a = (LRU_C * r) * _log_sigmoid(lam_ref[...])
    a = jnp.exp(log_a)
    b = jnp.sqrt(-jnp.tanh(log_a) * (a * a + 1.0)) * (ig * xc)
    return a, b


TL = 512


def _lru_prompt_body(x_ref, mod_ref, g_ref, win_ref, cw_ref, cb_ref, wa_ref, ba_ref, wx_ref, bx_ref,
                     lam_ref, wout_ref, y_ref, conv_ref, hl_ref, xpad, a_scr, b_scr, h_scr, hcar):
    i = pl.program_id(1)

    @pl.when(i == 0)
    def _():
        xpad[0:SUBLANES] = jnp.zeros((SUBLANES, LRU_WIDTH), F32)
        hcar[...] = jnp.zeros(hcar.shape, F32)

    x = x_ref[...]
    mod = mod_ref[0]
    h = _modulated_norm(x, g_ref[...], mod).astype(BF16)
    xb = _dot(h, win_ref[:, 0:LRU_WIDTH])
    gg = _dot(h, win_ref[:, LRU_WIDTH:2 * LRU_WIDTH])
    xpad[SUBLANES:SUBLANES + TL] = xb
    xc = cw_ref[CONV_W - 1:CONV_W] * xb + cb_ref[...]
    for j in range(CONV_W - 1):
        off = SUBLANES - (CONV_W - 1) + j
        xc = xc + cw_ref[j:j + 1] * xpad[off:off + TL]
    tail = xpad[TL + SUBLANES - (CONV_W - 1):TL + SUBLANES]
    conv_ref[0] = tail
    xpad[SUBLANES - (CONV_W - 1):SUBLANES] = tail

    a, b = _lru_gates(xc, wa_ref, ba_ref, wx_ref, bx_ref, lam_ref)
    a_scr[...] = a
    b_scr[...] = b
    row = lax.broadcasted_iota(I32, (SUBLANES, LRU_WIDTH), 0)

    def group(j, hprev):
        r0 = pl.multiple_of(j * SUBLANES, SUBLANES)
        aa = a_scr[pl.ds(r0, SUBLANES), :]
        bb = b_scr[pl.ds(r0, SUBLANES), :]
        d = 1
        while d < SUBLANES:
            a_sh = pltpu.roll(aa, d, axis=0)
            b_sh = pltpu.roll(bb, d, axis=0)
            m = row >= d
            bb = jnp.where(m, aa * b_sh + bb, bb)
            aa = jnp.where(m, aa * a_sh, aa)
            d *= 2
        hh = aa * hprev + bb
        h_scr[pl.ds(r0, SUBLANES), :] = hh
        return jnp.broadcast_to(hh[SUBLANES - 1:SUBLANES, :], (SUBLANES, LRU_WIDTH))

    hlast = lax.fori_loop(0, TL // SUBLANES, group, hcar[...], unroll=2)
    hcar[...] = hlast
    hl_ref[0] = hlast[0:1]
    hs = h_scr[...]
    out = _dot((hs * _silu(gg)).astype(BF16), wout_ref[...])
    y_ref[...] = x + mod[:, 2 * D_MODEL:3 * D_MODEL] * out


def _lru_prompt(x, mod, g, win, cw, cb, wa, ba, wx, bx, lam, wout, *, batch, seq):
    nb = seq // TL
    blk = lambda b, i: (b * nb + i, 0)
    const2 = lambda b, i: (0, 0)
    const3 = lambda b, i: (0, 0, 0)
    per_b = lambda b, i: (b, 0, 0)
    return pl.pallas_call(
        _lru_prompt_body,
        grid=(batch, nb),
        in_specs=[
            pl.BlockSpec((TL, D_MODEL), blk),
            pl.BlockSpec((1, 1, 3 * D_MODEL), per_b),
            pl.BlockSpec((1, D_MODEL), const2),
            pl.BlockSpec((D_MODEL, 2 * LRU_WIDTH), const2),
            pl.BlockSpec((CONV_W, LRU_WIDTH), const2),
            pl.BlockSpec((1, LRU_WIDTH), const2),
            pl.BlockSpec((N_LRU_BLOCKS, LRU_BLOCK, LRU_BLOCK), const3),
            pl.BlockSpec((1, LRU_WIDTH), const2),
            pl.BlockSpec((N_LRU_BLOCKS, LRU_BLOCK, LRU_BLOCK), const3),
            pl.BlockSpec((1, LRU_WIDTH), const2),
            pl.BlockSpec((1, LRU_WIDTH), const2),
            pl.BlockSpec((LRU_WIDTH, D_MODEL), const2),
        ],
        out_specs=[
            pl.BlockSpec((TL, D_MODEL), blk),
            pl.BlockSpec((1, CONV_W - 1, LRU_WIDTH), per_b),
            pl.BlockSpec((1, 1, LRU_WIDTH), per_b),
        ],
        out_shape=[
            jax.ShapeDtypeStruct((batch * seq, D_MODEL), F32),
            jax.ShapeDtypeStruct((batch, CONV_W - 1, LRU_WIDTH), F32),
            jax.ShapeDtypeStruct((batch, 1, LRU_WIDTH), F32),
        ],
        scratch_shapes=[
            pltpu.VMEM((TL + SUBLANES, LRU_WIDTH), F32),
            pltpu.VMEM((TL, LRU_WIDTH), F32),
            pltpu.VMEM((TL, LRU_WIDTH), F32),
            pltpu.VMEM((TL, LRU_WIDTH), F32),
            pltpu.VMEM((SUBLANES, LRU_WIDTH), F32),
        ],
        compiler_params=pltpu.CompilerParams(
            dimension_semantics=("arbitrary", "arbitrary"), vmem_limit_bytes=VMEM_LIMIT),
        name="lru_prompt",
    )(x, mod, g, win, cw, cb, wa, ba, wx, bx, lam, wout)


def _lru_sample_body(o_ref, gs_ref, x_ref, mod0_ref, wout0_ref, mod_ref, g_ref, win_ref, cw_ref, cb_ref,
                     wa_ref, ba_ref, wx_ref, bx_ref, lam_ref, wout_ref, sc_ref, sh_ref,
                     y_ref, conv_ref, hl_ref, *, nb, nt):
    x0 = x_ref[...]
    y0 = _dot((o_ref[...] * gs_ref[...].astype(F32)).astype(BF16), wout0_ref[...])
    x = x0 + mod0_ref[:, 2 * D_MODEL:3 * D_MODEL] * y0
    mod = mod_ref[...]
    h = _modulated_norm(x, g_ref[...], mod).astype(BF16)
    xb = _dot(h, win_ref[:, 0:LRU_WIDTH])
    gg = _dot(h, win_ref[:, LRU_WIDTH:2 * LRU_WIDTH])
    slabs = [sc_ref[j] for j in range(CONV_W - 1)] + [xb[t * nb:(t + 1) * nb] for t in range(nt)]
    xcs = []
    for t in range(nt):
        acc = cb_ref[...] + cw_ref[0:1] * slabs[t]
        for j in range(1, CONV_W):
            acc = acc + cw_ref[j:j + 1] * slabs[t + j]
        xcs.append(acc)
    for j in range(CONV_W - 1):
        conv_ref[j] = slabs[nt + j]
    xc = jnp.concatenate(xcs, axis=0)
    a, b = _lru_gates(xc, wa_ref, ba_ref, wx_ref, bx_ref, lam_ref)
    hprev = sh_ref[...]
    hs = []
    for t in range(nt):
        hprev = a[t * nb:(t + 1) * nb] * hprev + b[t * nb:(t + 1) * nb]
        hs.append(hprev)
    hl_ref[...] = hprev
    out = _dot((jnp.concatenate(hs, axis=0) * _silu(gg)).astype(BF16), wout_ref[...])
    y_ref[...] = x + mod[:, 2 * D_MODEL:3 * D_MODEL] * out


def _lru_sample(o, gs, x, mod0, wout0, mod, g, win, cw, cb, wa, ba, wx, bx, lam, wout, sc, sh, *, nb, nt):
    args = (o, gs, x, mod0, wout0, mod, g, win, cw, cb, wa, ba, wx, bx, lam, wout, sc, sh)

    def full(a):
        nd = a.ndim
        return pl.BlockSpec(a.shape, lambda i, nd=nd: (0,) * nd)

    out_shape = [
        jax.ShapeDtypeStruct((nt * nb, D_MODEL), F32),
        jax.ShapeDtypeStruct((CONV_W - 1, nb, LRU_WIDTH), F32),
        jax.ShapeDtypeStruct((nb, LRU_WIDTH), F32),
    ]
    body = functools.partial(_lru_sample_body, nb=nb, nt=nt)
    return pl.pallas_call(
        body,
        grid=(1,),
        in_specs=[full(a) for a in args],
        out_specs=[pl.BlockSpec(s.shape, lambda i, nd=len(s.shape): (0,) * nd) for s in out_shape],
        out_shape=out_shape,
        compiler_params=pltpu.CompilerParams(
            dimension_semantics=("arbitrary",), vmem_limit_bytes=VMEM_LIMIT),
        name="lru_sample",
    )(*args)


def kernel(x_prompt, x_sample, cache_k, cache_v, cache_idx_k, state_conv, state_h, page_table, c_prompt, c_sample, norm_g, ada_w, ada_b, attn_w_in, attn_q_norm, attn_k_norm, attn_w_out, lru_w_in, lru_conv_w, lru_conv_b, lru_w_a, lru_b_a, lru_w_x, lru_b_x, lru_lam, lru_w_out):
    B, S, _ = x_prompt.shape
    Bd, T, _ = x_sample.shape
    n_pages = page_table.shape[1]
    past = n_pages * PAGE_SIZE
    n_pool = cache_k.shape[1]
    assert S % QB == 0 and S % TL == 0 and n_pages % PAGES_PER_STEP == 0
    assert Bd % SEQ_PER_STEP == 0 and SEQ_PER_STEP * T == SUBLANES

    mod = _ada(jnp.concatenate([c_prompt, c_sample], axis=0), ada_w, ada_b)
    mod_p = [mod[l, :B].reshape(B, 1, 3 * D_MODEL) for l in range(2)]
    mod_s = [jnp.repeat(mod[l, B:], T, axis=0) for l in range(2)]

    w = attn_w_in[0]
    o_k, o_v, o_qi, o_ki, o_wi, o_g = 1024, 1280, 1536, 2560, 2624, 2640
    wcat = jnp.concatenate(
        [w[:, :o_ki], w[:, o_g:], w[:, o_ki:o_g], jnp.zeros((D_MODEL, PROJ_WIDTH - w.shape[1]), F32)],
        axis=1).astype(BF16)
    g0 = norm_g[0].reshape(1, D_MODEL)
    qn = attn_q_norm[0].reshape(1, HEAD_DIM)
    kn = attn_k_norm[0].reshape(1, HEAD_DIM)
    wout0 = attn_w_out[0].astype(BF16)

    tm = 512
    tabs_p = _rope_tables(jnp.arange(S))
    xp = x_prompt.reshape(B * S, D_MODEL)
    q, k, v, qi, kw, ki, gs = _proj(xp, mod_p[0], g0, wcat, qn, kn, tabs_p,
                                    tm=tm, rows_per_mod=S, tab_blocks=S // tm)
    x1p = _attn_prompt(q, qi, kw, k, v, gs, xp, mod_p[0], wout0, batch=B, seq=S)

    pos_s = past + jnp.arange(T)
    tabs_s = jnp.tile(_rope_tables(pos_s), (1, Bd, 1))
    xs = x_sample.reshape(Bd * T, D_MODEL)
    q2, k2, v2, qi2, kw2, ki2, gs2 = _proj(xs, mod_s[0].reshape(1, Bd * T, 3 * D_MODEL), g0, wcat, qn, kn,
                                           tabs_s, tm=Bd * T, rows_per_mod=Bd * T, tab_blocks=1)
    qir = qi2.reshape(Bd, T, N_IDX_HEADS, IDX_DIM).transpose(0, 2, 1, 3).reshape(Bd, N_IDX_HEADS * T, IDX_DIM)
    wcol = kw2[:, IDX_DIM:IDX_DIM + N_IDX_HEADS].reshape(Bd, T, N_IDX_HEADS).transpose(0, 2, 1)
    wcol = wcol.reshape(Bd, N_IDX_HEADS * T, 1)
    pad_rows = lambda a: jnp.pad(a, ((0, 0), (0, PAGE_SIZE - T)) + ((0, 0),) * (a.ndim - 2))
    ki_new = pad_rows(ki2.reshape(Bd, T, IDX_DIM)).transpose(0, 2, 1)
    cache_ik_t = cache_idx_k.reshape(n_pool, PAGE_SIZE, IDX_DIM).transpose(0, 2, 1)
    scores = _sample_scores(page_table, qir, wcol, cache_ik_t, ki_new,
                            dec_batch=Bd, dec_seq=T, n_pages=n_pages)
    sel = _sample_select(scores, past=past, dec_seq=T)
    qs = q2.reshape(Bd, T, N_KV_HEADS, GROUP, HEAD_DIM).transpose(0, 2, 3, 1, 4)
    qs = qs.reshape(Bd, N_KV_HEADS * GROUP * T, HEAD_DIM)
    new_pad = ((0, 0), (0, (PAGE_SIZE - T) * N_KV_HEADS), (0, 0))
    k_new = jnp.pad(k2.reshape(Bd, T * N_KV_HEADS, HEAD_DIM), new_pad)
    v_new = jnp.pad(v2.reshape(Bd, T * N_KV_HEADS, HEAD_DIM), new_pad)
    sel = jnp.repeat(sel, N_KV_HEADS, axis=1)
    o2 = _sample_attn(page_table, qs, sel,
                      cache_k.reshape(n_pool, PAGE_SIZE * N_KV_HEADS, HEAD_DIM),
                      cache_v.reshape(n_pool, PAGE_SIZE * N_KV_HEADS, HEAD_DIM),
                      k_new, v_new, dec_batch=Bd, dec_seq=T, n_pages=n_pages)
    o2 = o2.reshape(Bd, N_KV_HEADS, GROUP, T, HEAD_DIM).transpose(3, 0, 1, 2, 4).reshape(T * Bd, ATTN_WIDTH)

    g1 = norm_g[1].reshape(1, D_MODEL)
    win = lru_w_in[0].astype(BF16)
    cw = lru_conv_w[0]
    cb = lru_conv_b[0].reshape(1, LRU_WIDTH)
    wa = lru_w_a[0].astype(BF16)
    wx = lru_w_x[0].astype(BF16)
    ba = lru_b_a[0].reshape(1, LRU_WIDTH)
    bx = lru_b_x[0].reshape(1, LRU_WIDTH)
    lam = lru_lam[0].reshape(1, LRU_WIDTH)
    wout1 = lru_w_out[0].astype(BF16)

    yp, conv_p, h_p = _lru_prompt(x1p, mod_p[1], g1, win, cw, cb, wa, ba, wx, bx, lam, wout1, batch=B, seq=S)

    tmaj = lambda a: a.reshape(Bd, T, -1).transpose(1, 0, 2).reshape(T * Bd, -1)
    ys, conv_s, h_s = _lru_sample(
        o2, tmaj(gs2), tmaj(xs), tmaj(mod_s[0]), wout0, tmaj(mod_s[1]), g1, win, cw, cb, wa, ba, wx, bx,
        lam, wout1, state_conv[0].transpose(1, 0, 2), state_h[0], nb=Bd, nt=T)

    y_prompt = yp.reshape(B, S, D_MODEL)
    y_sample = ys.reshape(T, Bd, D_MODEL).transpose(1, 0, 2)
    return (y_prompt, y_sample,
            k.reshape(1, B, S, N_KV_HEADS, HEAD_DIM), v.reshape(1, B, S, N_KV_HEADS, HEAD_DIM),
            ki.reshape(1, B, S, IDX_DIM),
            k2.reshape(1, Bd, T, N_KV_HEADS, HEAD_DIM), v2.reshape(1, Bd, T, N_KV_HEADS, HEAD_DIM),
            ki2.reshape(1, Bd, T, IDX_DIM),
            conv_p.reshape(1, B, CONV_W - 1, LRU_WIDTH), h_p.reshape(1, B, LRU_WIDTH),
            conv_s.transpose(1, 0, 2).reshape(1, Bd, CONV_W - 1, LRU_WIDTH), h_s.reshape(1, Bd, LRU_WIDTH))
```

```python
import functools
import math

import jax
import jax.numpy as jnp
from jax import lax
from jax.experimental import pallas as pl
from jax.experimental.pallas import tpu as pltpu

F32 = jnp.float32
BF16 = jnp.bfloat16
I32 = jnp.int32

D_MODEL = 1024
N_HEADS = 8
N_KV_HEADS = 2
HEAD_DIM = 128
GROUP = N_HEADS // N_KV_HEADS
ATTN_WIDTH = N_HEADS * HEAD_DIM
KV_WIDTH = N_KV_HEADS * HEAD_DIM
N_IDX_HEADS = 16
IDX_DIM = 64
IDX_WIDTH = N_IDX_HEADS * IDX_DIM
TOPK_MAX = 256
ROPE_THETA = 500000.0
ROT_FRAC = 4
PAGE_SIZE = 128
LRU_WIDTH = D_MODEL
N_LRU_BLOCKS = 4
LRU_BLOCK = LRU_WIDTH // N_LRU_BLOCKS
CONV_W = 4
LRU_C = 8.0
EPS = 1e-6

LANES = 128
SUBLANES = 8
MXU_COLS = 256
VMEM_LIMIT = 56 * 1024 * 1024

OFF_Q = 0
OFF_K = OFF_Q + ATTN_WIDTH
OFF_V = OFF_K + KV_WIDTH
OFF_QI = OFF_V + KV_WIDTH
OFF_G = OFF_QI + IDX_WIDTH
OFF_KW = OFF_G + ATTN_WIDTH
PROJ_WIDTH = OFF_KW + LANES

INT_MIN = -(2 ** 31)
KEY_NEG_INF = 0x807FFFFF - 2 ** 32
RADIX_UNROLL = 4
RADIX_TRIPS = 32 // RADIX_UNROLL
COUNT_SLAB_ROWS = 64
LOG2E = 1.4426950408889634
NT_DIMS = (((1,), (1,)), ((), ()))


def _dot(a, b):
    return jnp.dot(a, b, preferred_element_type=F32)


def _dot_nt(a, b):
    return lax.dot_general(a, b, NT_DIMS, preferred_element_type=F32)


def _silu(x):
    return x / (1.0 + jnp.exp(-x))


def _sigmoid(x):
    return 1.0 / (1.0 + jnp.exp(-x))


def _rmsnorm(x, g):
    return x * lax.rsqrt(jnp.mean(x * x, axis=-1, keepdims=True) + EPS) * g


def _modulated_norm(x, g, mod):
    shift = mod[:, 0:D_MODEL]
    scale = mod[:, D_MODEL:2 * D_MODEL]
    return _rmsnorm(x, g) * (1.0 + scale) + shift


def _rope(y, tabs_ref, base, half):
    c = tabs_ref[base]
    s1 = tabs_ref[base + 1]
    s2 = tabs_ref[base + 2]
    return y * c + pltpu.roll(y, LANES - half, axis=1) * s1 + pltpu.roll(y, half, axis=1) * s2


def _key_to_f32(key):
    bits = jnp.where(key >= 0, key, key ^ 0x7FFFFFFF)
    return pltpu.bitcast(bits, F32)


def _count_where(sc_ref, n, axis, window, pred):
    slab = COUNT_SLAB_ROWS if axis == 0 else LANES
    assert n % slab == 0
    window = slice(None) if window is None else window
    parts = []
    for j in range(n // slab):
        span = slice(j * slab, (j + 1) * slab)
        sc = sc_ref[span, window] if axis == 0 else sc_ref[window, span]
        parts.append(jnp.where(pred(sc, j * slab), 1.0, 0.0))
    while len(parts) > 1:
        nxt = [parts[a] + parts[a + 1] for a in range(0, len(parts) - 1, 2)]
        if len(parts) % 2:
            nxt.append(parts[-1])
        parts = nxt
    return jnp.sum(parts[0], axis=axis, keepdims=True)


def _tie_cut(sc_ref, n, k, axis, window, thr, cut_ref):
    kf = float(k)
    thr_f = _key_to_f32(thr)
    at_least = _count_where(sc_ref, n, axis, window, lambda sc, base: sc >= thr_f)
    excess = jnp.logical_and(at_least > kf, thr > KEY_NEG_INF)
    cut_ref[...] = jnp.full(cut_ref.shape, n, I32)
    nbits = (n - 1).bit_length()

    @pl.when(jnp.max(jnp.where(excess, 1.0, 0.0)) > 0.5)
    def _():
        need = kf - _count_where(sc_ref, n, axis, window, lambda sc, base: sc > thr_f)

        def step(it, p):
            cand = p + lax.shift_left(jnp.int32(1), nbits - 1 - it)
            cnt = _count_where(
                sc_ref, n, axis, window,
                lambda sc, base: jnp.logical_and(
                    sc == thr_f, base + lax.broadcasted_iota(I32, sc.shape, axis) < cand))
            return jnp.where(cnt < need, cand, p)

        p = lax.fori_loop(0, nbits, step, jnp.zeros(thr.shape, I32))
        cut_ref[...] = jnp.where(excess, p, n)

    return cut_ref[...]


def _kth_largest_key(sc_ref, n, k, axis, window=None, side_work=None, side_trips=0):
    kf = float(k)
    shape = list(sc_ref.shape)
    if window is None:
        window = slice(0, shape[1 - axis])
    shape[1 - axis] = window.stop - window.start
    shape[axis] = 1

    def body(it, prefix):
        cand = prefix + lax.shift_left(jnp.int32(1), 31 - it)
        cand_f = _key_to_f32(cand)
        cnt = _count_where(sc_ref, n, axis, window, lambda sc, base: sc >= cand_f)
        return jnp.where(cnt >= kf, cand, prefix)

    def trip(with_side, t, prefix):
        for e in range(RADIX_UNROLL):
            prefix = body(t * RADIX_UNROLL + e, prefix)
        if with_side:
            side_work(t)
        return prefix

    assert side_trips <= RADIX_TRIPS
    prefix = jnp.full(tuple(shape), INT_MIN, I32)
    if side_trips:
        prefix = lax.fori_loop(0, side_trips, functools.partial(trip, True), prefix)
    return lax.fori_loop(side_trips, RADIX_TRIPS, functools.partial(trip, False), prefix)


def _ada_body(c_ref, w_ref, b_ref, o_ref):
    s = _silu(c_ref[...]).astype(BF16)
    o_ref[0] = _dot(s, w_ref[0].astype(BF16)) + b_ref[0]


def _ada(c_all, ada_w, ada_b):
    rows = c_all.shape[0]
    depth = ada_w.shape[0]
    nblk = 3
    return pl.pallas_call(
        _ada_body,
        grid=(depth, nblk),
        in_specs=[
            pl.BlockSpec((rows, D_MODEL), lambda l, j: (0, 0)),
            pl.BlockSpec((1, D_MODEL, D_MODEL), lambda l, j: (l, 0, j)),
            pl.BlockSpec((1, 1, D_MODEL), lambda l, j: (l, 0, j)),
        ],
        out_specs=pl.BlockSpec((1, rows, D_MODEL), lambda l, j: (l, 0, j)),
        out_shape=jax.ShapeDtypeStruct((depth, rows, 3 * D_MODEL), F32),
        compiler_params=pltpu.CompilerParams(
            dimension_semantics=("arbitrary", "arbitrary"), vmem_limit_bytes=VMEM_LIMIT),
        name="ada_mod",
    )(c_all, ada_w, ada_b.reshape(depth, 1, 3 * D_MODEL))


def _proj_body(x_ref, mod_ref, g_ref, w_ref, qn_ref, kn_ref, tabs_ref,
               q_ref, k_ref, v_ref, qi_ref, kw_ref, ki_ref, gs_ref):
    h = _modulated_norm(x_ref[...], g_ref[...], mod_ref[0]).astype(BF16)
    qn = qn_ref[...]
    kn = kn_ref[...]
    half_h = HEAD_DIM // ROT_FRAC // 2
    half_i = IDX_DIM // ROT_FRAC // 2

    def slabs(off, n_slabs):
        for c0 in range(0, n_slabs, MXU_COLS // LANES):
            z = _dot(h, w_ref[:, off + c0 * LANES:off + c0 * LANES + MXU_COLS])
            for e in range(MXU_COLS // LANES):
                yield c0 + e, z[:, e * LANES:(e + 1) * LANES]

    for hh, z in slabs(OFF_Q, N_HEADS):
        q_ref[:, hh * HEAD_DIM:(hh + 1) * HEAD_DIM] = _rope(_rmsnorm(z, qn), tabs_ref, 0, half_h).astype(BF16)
    tm = x_ref.shape[0]
    for hh, z in slabs(OFF_K, N_KV_HEADS):
        k_ref[pl.ds(hh, tm, stride=N_KV_HEADS), :] = _rope(_rmsnorm(z, kn), tabs_ref, 0, half_h)
    for hh, z in slabs(OFF_V, N_KV_HEADS):
        v_ref[pl.ds(hh, tm, stride=N_KV_HEADS), :] = z
    for c, z in slabs(OFF_QI, IDX_WIDTH // LANES):
        qi_ref[:, c * LANES:(c + 1) * LANES] = _rope(z, tabs_ref, 3, half_i).astype(BF16)
    for c, z in slabs(OFF_G, ATTN_WIDTH // LANES):
        gs_ref[:, c * LANES:(c + 1) * LANES] = _silu(z).astype(BF16)
    kw = _rope(_dot(h, w_ref[:, OFF_KW:OFF_KW + LANES]), tabs_ref, 6, half_i)
    kw_ref[...] = kw
    ki_ref[...] = kw[:, 0:IDX_DIM]


def _proj(x, mod, g, w, qn, kn, tabs, *, tm, rows_per_mod, tab_blocks):
    n = x.shape[0]
    mod_rows = mod.shape[1]
    grid = (n // tm,)
    row = lambda i: (i, 0)
    outs = [
        jax.ShapeDtypeStruct((n, ATTN_WIDTH), BF16),
        jax.ShapeDtypeStruct((n * N_KV_HEADS, HEAD_DIM), F32),
        jax.ShapeDtypeStruct((n * N_KV_HEADS, HEAD_DIM), F32),
        jax.ShapeDtypeStruct((n, IDX_WIDTH), BF16),
        jax.ShapeDtypeStruct((n, LANES), F32),
        jax.ShapeDtypeStruct((n, IDX_DIM), F32),
        jax.ShapeDtypeStruct((n, ATTN_WIDTH), BF16),
    ]
    return pl.pallas_call(
        _proj_body,
        grid=grid,
        in_specs=[
            pl.BlockSpec((tm, D_MODEL), row),
            pl.BlockSpec((1, mod_rows, 3 * D_MODEL), lambda i: (i * tm // rows_per_mod, 0, 0)),
            pl.BlockSpec((1, D_MODEL), lambda i: (0, 0)),
            pl.BlockSpec((D_MODEL, PROJ_WIDTH), lambda i: (0, 0)),
            pl.BlockSpec((1, HEAD_DIM), lambda i: (0, 0)),
            pl.BlockSpec((1, HEAD_DIM), lambda i: (0, 0)),
            pl.BlockSpec((9, tm, LANES), lambda i: (0, i % tab_blocks, 0)),
        ],
        out_specs=[
            pl.BlockSpec((tm, ATTN_WIDTH), row),
            pl.BlockSpec((tm * N_KV_HEADS, HEAD_DIM), row),
            pl.BlockSpec((tm * N_KV_HEADS, HEAD_DIM), row),
            pl.BlockSpec((tm, IDX_WIDTH), row),
            pl.BlockSpec((tm, LANES), row),
            pl.BlockSpec((tm, IDX_DIM), row),
            pl.BlockSpec((tm, ATTN_WIDTH), row),
        ],
        out_shape=outs,
        compiler_params=pltpu.CompilerParams(
            dimension_semantics=("arbitrary",), vmem_limit_bytes=VMEM_LIMIT),
        name="attn_proj",
    )(x, mod, g, w, qn, kn, tabs)


def _rope_tables(pos):
    posf = pos.astype(F32)
    t = pos.shape[0]

    def base(d):
        r = d // ROT_FRAC
        half = r // 2
        inv = jnp.exp(-jnp.log(jnp.asarray(ROPE_THETA, F32)) * jnp.arange(half, dtype=F32) * 2.0 / r)
        ang = posf[:, None] * inv[None, :]
        cos = jnp.cos(ang)
        sin = jnp.sin(ang)
        c = jnp.concatenate([cos, cos, jnp.ones((t, d - r), F32)], axis=1)
        s1 = jnp.concatenate([-sin, jnp.zeros((t, d - half), F32)], axis=1)
        s2 = jnp.concatenate([jnp.zeros((t, half), F32), sin, jnp.zeros((t, d - r), F32)], axis=1)
        return c, s1, s2

    hc, hs1, hs2 = base(HEAD_DIM)
    ic, is1, is2 = base(IDX_DIM)
    wi_scale = N_IDX_HEADS ** -0.5 * IDX_DIM ** -0.5
    pad = LANES - IDX_DIM
    kc = jnp.concatenate([ic, jnp.full((t, N_IDX_HEADS), wi_scale, F32),
                          jnp.zeros((t, pad - N_IDX_HEADS), F32)], axis=1)
    ks1 = jnp.concatenate([is1, jnp.zeros((t, pad), F32)], axis=1)
    ks2 = jnp.concatenate([is2, jnp.zeros((t, pad), F32)], axis=1)
    two = lambda a: jnp.concatenate([a, a], axis=1)
    return jnp.stack([hc, hs1, hs2, two(ic), two(is1), two(is2), kc, ks1, ks2])


QB = 128
SUB_BLOCKS = 1
KEY_BUCKET = 256


def _attn_prompt_body(q_ref, qi_ref, kwb_ref, kws_ref, k_ref, v_ref, gs_ref, x_ref, gate_ref,
                      wout_ref, o_ref, kbf, vbf, kipar, sct_ref, s_ref, cut_ref, *, seq, k_sel):
    i = pl.program_id(1)
    assert SUB_BLOCKS == 1

    @pl.when(i == 0)
    def _():
        ones_col = jnp.where(lax.broadcasted_iota(I32, (seq, HEAD_DIM), 1) == 0, 1.0, 0.0).astype(BF16)
        for kh in range(N_KV_HEADS):
            kbf[kh] = k_ref[pl.ds(kh, seq, stride=N_KV_HEADS), :].astype(BF16)
            vbf[kh, :, 0:HEAD_DIM] = v_ref[pl.ds(kh, seq, stride=N_KV_HEADS), :].astype(BF16)
            vbf[kh, :, HEAD_DIM:2 * HEAD_DIM] = ones_col
        kw = kws_ref[...]
        lane = lax.broadcasted_iota(I32, kw.shape, 1)
        ke = jnp.where(lane < IDX_DIM, kw, 0.0)
        kipar[0] = ke.astype(BF16)
        kipar[1] = pltpu.roll(ke, IDX_DIM, axis=1).astype(BF16)

    def stacked_q(rows, kh):
        return jnp.concatenate(
            [q_ref[rows, (kh * GROUP + g) * HEAD_DIM:(kh * GROUP + g + 1) * HEAD_DIM] for g in range(GROUP)],
            axis=0)

    def select(sb, nk):
        rows = slice(sb * QB, (sb + 1) * QB)
        first = (i * SUB_BLOCKS + sb) * QB
        col = lax.broadcasted_iota(I32, (QB, nk), 1)
        pos = first + lax.broadcasted_iota(I32, (QB, nk), 0)
        if nk <= k_sel:
            return jnp.where(col <= pos, 0.0, -jnp.inf)
        kwb = kwb_ref[rows, :]
        score = None
        for p in range(N_IDX_HEADS // 2):
            pair = None
            for par in range(2):
                h = 2 * p + par
                d = _dot_nt(qi_ref[rows, p * LANES:(p + 1) * LANES], kipar[par, 0:nk, :])
                term = kwb[:, IDX_DIM + h:IDX_DIM + h + 1] * jnp.maximum(d, 0.0)
                pair = term if pair is None else pair + term
            score = pair if score is None else score + pair
        score = jnp.where(col <= pos, score, -jnp.inf)
        sct_ref[0:nk, rows] = score.T

        def qk_tile(t):
            off = pl.multiple_of(t * MXU_COLS, MXU_COLS)
            for kh in range(N_KV_HEADS):
                s_ref[kh, :, pl.ds(off, MXU_COLS)] = _dot_nt(stacked_q(rows, kh), kbf[kh, pl.ds(off, MXU_COLS), :])

        assert nk % MXU_COLS == 0
        thr = _kth_largest_key(sct_ref, nk, k_sel, 0, window=rows,
                               side_work=qk_tile, side_trips=nk // MXU_COLS)
        cut = _tie_cut(sct_ref, nk, k_sel, 0, rows, thr, cut_ref)
        thr_f = _key_to_f32(thr)
        sc = sct_ref[0:nk, rows]
        key_row = lax.broadcasted_iota(I32, (nk, QB), 0)
        q_pos = first + lax.broadcasted_iota(I32, (nk, QB), 1)
        keep = jnp.logical_or(sc > thr_f, jnp.logical_and(sc == thr_f, key_row <= cut))
        sel_t = jnp.logical_and(key_row <= q_pos, jnp.logical_or(keep, thr <= KEY_NEG_INF))
        return jnp.where(sel_t, 0.0, -jnp.inf).T

    def attend(sb, nk, bias):
        rows = slice(sb * QB, (sb + 1) * QB)
        c = HEAD_DIM ** -0.5 * LOG2E
        heads = [None] * N_HEADS
        for kh in range(N_KV_HEADS):
            if nk <= k_sel:
                s = _dot_nt(stacked_q(rows, kh), kbf[kh, 0:nk, :])
            else:
                s = s_ref[kh, :, 0:nk]
            ps = []
            for g in range(GROUP):
                sg = s[g * QB:(g + 1) * QB] + bias
                ps.append(jnp.exp2((sg - jnp.max(sg, axis=1, keepdims=True)) * c).astype(BF16))
            o = _dot(jnp.concatenate(ps, axis=0), vbf[kh, 0:nk, :])
            for g in range(GROUP):
                og = o[g * QB:(g + 1) * QB]
                heads[kh * GROUP + g] = og[:, 0:HEAD_DIM] / og[:, HEAD_DIM:HEAD_DIM + 1]
        attn = jnp.concatenate(heads, axis=1)
        y = _dot((attn * gs_ref[rows, :].astype(F32)).astype(BF16), wout_ref[...])
        o_ref[rows, :] = x_ref[rows, :] + gate_ref[0] * y

    def block(nk):
        biases = [select(sb, nk) for sb in range(SUB_BLOCKS)]
        for sb in range(SUB_BLOCKS):
            attend(sb, nk, biases[sb])

    steps_per_bucket = KEY_BUCKET // (SUB_BLOCKS * QB)
    for bucket in range(seq // KEY_BUCKET):
        pl.when(i // steps_per_bucket == bucket)(functools.partial(block, (bucket + 1) * KEY_BUCKET))


def _attn_prompt(q, qi, kw, k, v, gs, x, mod, wout, *, batch, seq):
    step = SUB_BLOCKS * QB
    nb = seq // step
    k_sel = min(TOPK_MAX, seq // 4)
    blk = lambda b, i: (b * nb + i, 0)
    whole = lambda b, i: (b, 0)
    body = functools.partial(_attn_prompt_body, seq=seq, k_sel=k_sel)
    return pl.pallas_call(
        body,
        grid=(batch, nb),
        in_specs=[
            pl.BlockSpec((step, ATTN_WIDTH), blk),
            pl.BlockSpec((step, IDX_WIDTH), blk),
            pl.BlockSpec((step, LANES), blk),
            pl.BlockSpec((seq, LANES), whole),
            pl.BlockSpec((seq * N_KV_HEADS, HEAD_DIM), whole),
            pl.BlockSpec((seq * N_KV_HEADS, HEAD_DIM), whole),
            pl.BlockSpec((step, ATTN_WIDTH), blk),
            pl.BlockSpec((step, D_MODEL), blk),
            pl.BlockSpec((1, 1, D_MODEL), lambda b, i: (b, 0, 2)),
            pl.BlockSpec((ATTN_WIDTH, D_MODEL), lambda b, i: (0, 0)),
        ],
        out_specs=pl.BlockSpec((step, D_MODEL), blk),
        out_shape=jax.ShapeDtypeStruct((batch * seq, D_MODEL), F32),
        scratch_shapes=[
            pltpu.VMEM((N_KV_HEADS, seq, HEAD_DIM), BF16),
            pltpu.VMEM((N_KV_HEADS, seq, 2 * HEAD_DIM), BF16),
            pltpu.VMEM((2, seq, LANES), BF16),
            pltpu.VMEM((seq, step), F32),
            pltpu.VMEM((N_KV_HEADS, GROUP * QB, seq), F32),
            pltpu.VMEM((1, QB), I32),
        ],
        compiler_params=pltpu.CompilerParams(
            dimension_semantics=("arbitrary", "arbitrary"), vmem_limit_bytes=VMEM_LIMIT),
        name="attn_prompt",
    )(q, qi, kw, kw, k, v, gs, x, mod, wout)


PAGES_PER_STEP = 16
SCORE_PAGES_PER_STEP = 32
SEQ_PER_STEP = 2


def _sample_tile_score(qir, wcol, kpage_t):
    d = _dot(qir, kpage_t.astype(BF16))
    r = wcol * jnp.maximum(d, 0.0)
    acc = r[0:SUBLANES]
    for j in range(1, r.shape[0] // SUBLANES):
        acc = acc + r[j * SUBLANES:(j + 1) * SUBLANES]
    half = SUBLANES // 2
    return acc[0:half] + acc[half:SUBLANES]


def _sample_score_body(pt_ref, qir_ref, wcol_ref, *rest, n_chunks, dec_seq):
    per_seq = SCORE_PAGES_PER_STEP
    npg = SEQ_PER_STEP * per_seq
    pages = rest[:npg]
    knew_ref = rest[npg]
    o_ref = rest[npg + 1]
    c = pl.program_id(1)

    @pl.when(c < n_chunks)
    def _():
        for s in range(SEQ_PER_STEP):
            kcat = jnp.concatenate([pages[s * per_seq + j][0] for j in range(per_seq)], axis=1)
            o_ref[s * dec_seq:(s + 1) * dec_seq, :] = _sample_tile_score(qir_ref[s], wcol_ref[s], kcat)

    @pl.when(c == n_chunks)
    def _():
        o_ref[...] = jnp.zeros(o_ref.shape, F32)
        for s in range(SEQ_PER_STEP):
            o_ref[s * dec_seq:(s + 1) * dec_seq, 0:PAGE_SIZE] = _sample_tile_score(
                qir_ref[s], wcol_ref[s], knew_ref[s])


def _sample_scores(page_table, qir, wcol, cache_ik, knew, *, dec_batch, dec_seq, n_pages):
    per_seq = SCORE_PAGES_PER_STEP
    assert n_pages % per_seq == 0 and per_seq % PAGES_PER_STEP == 0
    n_chunks = n_pages // per_seq
    width = (n_chunks + 1) * per_seq * PAGE_SIZE
    rows = IDX_DIM

    def page_map(s, j):
        def f(bp, c, pt):
            cc = jnp.minimum(c, n_chunks - 1)
            return (pt[bp * SEQ_PER_STEP + s, cc * per_seq + j], 0, 0)
        return f

    in_specs = [
        pl.BlockSpec((SEQ_PER_STEP, rows, IDX_DIM), lambda bp, c, pt: (bp, 0, 0)),
        pl.BlockSpec((SEQ_PER_STEP, rows, 1), lambda bp, c, pt: (bp, 0, 0)),
    ]
    for s in range(SEQ_PER_STEP):
        for j in range(per_seq):
            in_specs.append(pl.BlockSpec((1, IDX_DIM, PAGE_SIZE), page_map(s, j)))
    in_specs.append(pl.BlockSpec((SEQ_PER_STEP, IDX_DIM, PAGE_SIZE), lambda bp, c, pt: (bp, 0, 0)))
    body = functools.partial(_sample_score_body, n_chunks=n_chunks, dec_seq=dec_seq)
    return pl.pallas_call(
        body,
        grid_spec=pltpu.PrefetchScalarGridSpec(
            num_scalar_prefetch=1,
            grid=(dec_batch // SEQ_PER_STEP, n_chunks + 1),
            in_specs=in_specs,
            out_specs=pl.BlockSpec((SEQ_PER_STEP * dec_seq, per_seq * PAGE_SIZE),
                                   lambda bp, c, pt: (bp, c)),
        ),
        out_shape=jax.ShapeDtypeStruct((dec_batch * dec_seq, width), F32),
        compiler_params=pltpu.CompilerParams(
            dimension_semantics=("arbitrary", "arbitrary"), vmem_limit_bytes=VMEM_LIMIT),
        name="sample_scores",
    )(page_table, qir, wcol, *([cache_ik] * (SEQ_PER_STEP * per_seq)), knew)


SELECT_ROWS = 32


def _sample_select_body(sc_ref, sel_ref, key_ref, cut_ref, *, past, dec_seq, k_sel):
    shape = sc_ref.shape
    col = lax.broadcasted_iota(I32, shape, 1)
    t = lax.broadcasted_iota(I32, shape, 0) % dec_seq
    adm = (col - past) <= t
    key_ref[...] = jnp.where(adm, sc_ref[...], -jnp.inf)
    thr = _kth_largest_key(key_ref, shape[1], k_sel, 1)
    cut = _tie_cut(key_ref, shape[1], k_sel, 1, None, thr, cut_ref)
    thr_f = _key_to_f32(thr)
    sc = key_ref[...]
    keep = jnp.logical_or(sc > thr_f, jnp.logical_and(sc == thr_f, col <= cut))
    picked = jnp.logical_or(keep, thr <= KEY_NEG_INF)
    sel_ref[...] = jnp.where(jnp.logical_and(picked, adm), 1.0, 0.0)


def _sample_select(scores, *, past, dec_seq):
    k_sel = min(TOPK_MAX, (past + dec_seq) // 4)
    body = functools.partial(_sample_select_body, past=past, dec_seq=dec_seq, k_sel=k_sel)
    rows, width = scores.shape
    assert rows % SELECT_ROWS == 0 and SELECT_ROWS % dec_seq == 0
    return pl.pallas_call(
        body,
        grid=(rows // SELECT_ROWS,),
        in_specs=[pl.BlockSpec((SELECT_ROWS, width), lambda i: (i, 0))],
        out_specs=pl.BlockSpec((SELECT_ROWS, width), lambda i: (i, 0)),
        out_shape=jax.ShapeDtypeStruct(scores.shape, F32),
        scratch_shapes=[pltpu.VMEM((SELECT_ROWS, width), F32), pltpu.VMEM((SELECT_ROWS, 1), I32)],
        compiler_params=pltpu.CompilerParams(
            dimension_semantics=("arbitrary",), vmem_limit_bytes=VMEM_LIMIT),
        name="sample_select",
    )(scores)


def _sample_attn_body(pt_ref, q_ref, sel_ref, *rest, n_chunks, dec_seq):
    npg = SEQ_PER_STEP * PAGES_PER_STEP
    kpages = rest[:npg]
    vpages = rest[npg:2 * npg]
    knew_ref, vnew_ref, o_ref, m_ref, l_ref, acc_ref = rest[2 * npg:]
    c = pl.program_id(1)
    scale = HEAD_DIM ** -0.5
    rows_per_head = GROUP * dec_seq
    rows_per_head_log2 = rows_per_head.bit_length() - 1
    assert rows_per_head == 1 << rows_per_head_log2 and N_KV_HEADS & (N_KV_HEADS - 1) == 0

    @pl.when(c == 0)
    def _():
        m_ref[...] = jnp.full(m_ref.shape, -jnp.inf, F32)
        l_ref[...] = jnp.zeros(l_ref.shape, F32)
        acc_ref[...] = jnp.zeros(acc_ref.shape, F32)

    def update(tiles, finish):
        m_old = [m_ref[s] for s in range(SEQ_PER_STEP)]
        l_old = [l_ref[s] for s in range(SEQ_PER_STEP)]
        a_old = [acc_ref[s] for s in range(SEQ_PER_STEP)]
        m_out, l_out, a_out = [], [], []
        for s, (kt, vt, sel) in enumerate(tiles):
            sc = _dot_nt(q_ref[s], kt) * scale
            key_head = jnp.bitwise_and(lax.broadcasted_iota(I32, sc.shape, 1), N_KV_HEADS - 1)
            row_head = jnp.right_shift(lax.broadcasted_iota(I32, sc.shape, 0), rows_per_head_log2)
            picked = jnp.concatenate([sel] * (N_KV_HEADS * GROUP), axis=0) > 0.5
            sc = jnp.where(jnp.logical_and(picked, key_head == row_head), sc, -jnp.inf)
            m_new = jnp.maximum(m_old[s], jnp.max(sc, axis=1, keepdims=True))
            m_safe = jnp.where(m_new == -jnp.inf, 0.0, m_new)
            alpha = jnp.exp(m_old[s] - m_safe)
            p = jnp.exp(sc - m_safe)
            m_out.append(m_new)
            l_out.append(alpha * l_old[s] + jnp.sum(p, axis=1, keepdims=True))
            a_out.append(alpha * a_old[s] + _dot(p.astype(BF16), vt))
        for s in range(SEQ_PER_STEP):
            if finish:
                o_ref[s] = a_out[s] / l_out[s]
            else:
                m_ref[s] = m_out[s]
                l_ref[s] = l_out[s]
                acc_ref[s] = a_out[s]

    def page_rows(page_refs, s):
        return jnp.concatenate(
            [page_refs[s * PAGES_PER_STEP + j][0].astype(BF16) for j in range(PAGES_PER_STEP)], axis=0)

    @pl.when(c < n_chunks)
    def _():
        update([(page_rows(kpages, s), page_rows(vpages, s), sel_ref[s * dec_seq:(s + 1) * dec_seq, :])
                for s in range(SEQ_PER_STEP)], False)

    @pl.when(c == n_chunks)
    def _():
        new_rows = PAGE_SIZE * N_KV_HEADS
        update([(knew_ref[s].astype(BF16), vnew_ref[s].astype(BF16),
                 sel_ref[s * dec_seq:(s + 1) * dec_seq, 0:new_rows]) for s in range(SEQ_PER_STEP)], True)


def _sample_attn(page_table, qs, sel, cache_k2, cache_v2, knew, vnew, *, dec_batch, dec_seq, n_pages):
    n_chunks = n_pages // PAGES_PER_STEP
    rows = N_KV_HEADS * GROUP * dec_seq
    page_rows = PAGE_SIZE * N_KV_HEADS

    def page_map(s, j):
        def f(bp, c, pt):
            cc = jnp.minimum(c, n_chunks - 1)
            return (pt[bp * SEQ_PER_STEP + s, cc * PAGES_PER_STEP + j], 0, 0)
        return f

    page_specs = []
    for s in range(SEQ_PER_STEP):
        for j in range(PAGES_PER_STEP):
            page_specs.append(pl.BlockSpec((1, page_rows, HEAD_DIM), page_map(s, j)))
    new_spec = pl.BlockSpec((SEQ_PER_STEP, page_rows, HEAD_DIM), lambda bp, c, pt: (bp, 0, 0))
    in_specs = [
        pl.BlockSpec((SEQ_PER_STEP, rows, HEAD_DIM), lambda bp, c, pt: (bp, 0, 0)),
        pl.BlockSpec((SEQ_PER_STEP * dec_seq, PAGES_PER_STEP * page_rows), lambda bp, c, pt: (bp, c)),
    ] + page_specs + page_specs + [new_spec, new_spec]
    npg = SEQ_PER_STEP * PAGES_PER_STEP
    body = functools.partial(_sample_attn_body, n_chunks=n_chunks, dec_seq=dec_seq)
    return pl.pallas_call(
        body,
        grid_spec=pltpu.PrefetchScalarGridSpec(
            num_scalar_prefetch=1,
            grid=(dec_batch // SEQ_PER_STEP, n_chunks + 1),
            in_specs=in_specs,
            out_specs=pl.BlockSpec((SEQ_PER_STEP, rows, HEAD_DIM), lambda bp, c, pt: (bp, 0, 0)),
            scratch_shapes=[
                pltpu.VMEM((SEQ_PER_STEP, rows, 1), F32),
                pltpu.VMEM((SEQ_PER_STEP, rows, 1), F32),
                pltpu.VMEM((SEQ_PER_STEP, rows, HEAD_DIM), F32),
            ],
        ),
        out_shape=jax.ShapeDtypeStruct((dec_batch, rows, HEAD_DIM), F32),
        compiler_params=pltpu.CompilerParams(
            dimension_semantics=("arbitrary", "arbitrary"), vmem_limit_bytes=VMEM_LIMIT),
        name="sample_attn",
    )(page_table, qs, sel, *([cache_k2] * npg), *([cache_v2] * npg), knew, vnew)


def _log_sigmoid(x):
    return jnp.minimum(x, 0.0) - jnp.log1p(jnp.exp(-jnp.abs(x)))


def _lru_gates(xc, wa_ref, ba_ref, wx_ref, bx_ref, lam_ref):
    xcb = xc.astype(BF16)
    ra = []
    ia = []
    for n in range(N_LRU_BLOCKS):
        blk = xcb[:, n * LRU_BLOCK:(n + 1) * LRU_BLOCK]
        ra.append(_dot(blk, wa_ref[n]))
        ia.append(_dot(blk, wx_ref[n]))
    r = _sigmoid(jnp.concatenate(ra, axis=1) + ba_ref[...])
    ig = _sigmoid(jnp.concatenate(ia, axis=1) + bx_ref[...])
    log_a = (LRU_C * r) * _log_sigmoid(lam_ref[...])
    a = jnp.exp(log_a)
    b = jnp.sqrt(-jnp.tanh(log_a) * (a * a + 1.0)) * (ig * xc)
    return a, b


TL = 512


def _lru_prompt_body(x_ref, mod_ref, g_ref, win_ref, cw_ref, cb_ref, wa_ref, ba_ref, wx_ref, bx_ref,
                     lam_ref, wout_ref, y_ref, conv_ref, hl_ref, xpad, a_scr, b_scr, h_scr, hcar):
    i = pl.program_id(1)

    @pl.when(i == 0)
    def _():
        xpad[0:SUBLANES] = jnp.zeros((SUBLANES, LRU_WIDTH), F32)
        hcar[...] = jnp.zeros(hcar.shape, F32)

    x = x_ref[...]
    mod = mod_ref[0]
    h = _modulated_norm(x, g_ref[...], mod).astype(BF16)
    xb = _dot(h, win_ref[:, 0:LRU_WIDTH])
    gg = _dot(h, win_ref[:, LRU_WIDTH:2 * LRU_WIDTH])
    xpad[SUBLANES:SUBLANES + TL] = xb
    xc = cw_ref[CONV_W - 1:CONV_W] * xb + cb_ref[...]
    for j in range(CONV_W - 1):
        off = SUBLANES - (CONV_W - 1) + j
        xc = xc + cw_ref[j:j + 1] * xpad[off:off + TL]
    tail = xpad[TL + SUBLANES - (CONV_W - 1):TL + SUBLANES]
    conv_ref[0] = tail
    xpad[SUBLANES - (CONV_W - 1):SUBLANES] = tail

    a, b = _lru_gates(xc, wa_ref, ba_ref, wx_ref, bx_ref, lam_ref)
    a_scr[...] = a
    b_scr[...] = b
    row = lax.broadcasted_iota(I32, (SUBLANES, LRU_WIDTH), 0)

    def group(j, hprev):
        r0 = pl.multiple_of(j * SUBLANES, SUBLANES)
        aa = a_scr[pl.ds(r0, SUBLANES), :]
        bb = b_scr[pl.ds(r0, SUBLANES), :]
        d = 1
        while d < SUBLANES:
            a_sh = pltpu.roll(aa, d, axis=0)
            b_sh = pltpu.roll(bb, d, axis=0)
            m = row >= d
            bb = jnp.where(m, aa * b_sh + bb, bb)
            aa = jnp.where(m, aa * a_sh, aa)
            d *= 2
        hh = aa * hprev + bb
        h_scr[pl.ds(r0, SUBLANES), :] = hh
        return jnp.broadcast_to(hh[SUBLANES - 1:SUBLANES, :], (SUBLANES, LRU_WIDTH))

    hlast = lax.fori_loop(0, TL // SUBLANES, group, hcar[...], unroll=2)
    hcar[...] = hlast
    hl_ref[0] = hlast[0:1]
    hs = h_scr[...]
    out = _dot((hs * _silu(gg)).astype(BF16), wout_ref[...])
    y_ref[...] = x + mod[:, 2 * D_MODEL:3 * D_MODEL] * out


def _lru_prompt(x, mod, g, win, cw, cb, wa, ba, wx, bx, lam, wout, *, batch, seq):
    nb = seq // TL
    blk = lambda b, i: (b * nb + i, 0)
    const2 = lambda b, i: (0, 0)
    const3 = lambda b, i: (0, 0, 0)
    per_b = lambda b, i: (b, 0, 0)
    return pl.pallas_call(
        _lru_prompt_body,
        grid=(batch, nb),
        in_specs=[
            pl.BlockSpec((TL, D_MODEL), blk),
            pl.BlockSpec((1, 1, 3 * D_MODEL), per_b),
            pl.BlockSpec((1, D_MODEL), const2),
            pl.BlockSpec((D_MODEL, 2 * LRU_WIDTH), const2),
            pl.BlockSpec((CONV_W, LRU_WIDTH), const2),
            pl.BlockSpec((1, LRU_WIDTH), const2),
            pl.BlockSpec((N_LRU_BLOCKS, LRU_BLOCK, LRU_BLOCK), const3),
            pl.BlockSpec((1, LRU_WIDTH), const2),
            pl.BlockSpec((N_LRU_BLOCKS, LRU_BLOCK, LRU_BLOCK), const3),
            pl.BlockSpec((1, LRU_WIDTH), const2),
            pl.BlockSpec((1, LRU_WIDTH), const2),
            pl.BlockSpec((LRU_WIDTH, D_MODEL), const2),
        ],
        out_specs=[
            pl.BlockSpec((TL, D_MODEL), blk),
            pl.BlockSpec((1, CONV_W - 1, LRU_WIDTH), per_b),
            pl.BlockSpec((1, 1, LRU_WIDTH), per_b),
        ],
        out_shape=[
            jax.ShapeDtypeStruct((batch * seq, D_MODEL), F32),
            jax.ShapeDtypeStruct((batch, CONV_W - 1, LRU_WIDTH), F32),
            jax.ShapeDtypeStruct((batch, 1, LRU_WIDTH), F32),
        ],
        scratch_shapes=[
            pltpu.VMEM((TL + SUBLANES, LRU_WIDTH), F32),
            pltpu.VMEM((TL, LRU_WIDTH), F32),
            pltpu.VMEM((TL, LRU_WIDTH), F32),
            pltpu.VMEM((TL, LRU_WIDTH), F32),
            pltpu.VMEM((SUBLANES, LRU_WIDTH), F32),
        ],
        compiler_params=pltpu.CompilerParams(
            dimension_semantics=("arbitrary", "arbitrary"), vmem_limit_bytes=VMEM_LIMIT),
        name="lru_prompt",
    )(x, mod, g, win, cw, cb, wa, ba, wx, bx, lam, wout)


def _lru_sample_body(o_ref, gs_ref, x_ref, mod0_ref, wout0_ref, mod_ref, g_ref, win_ref, cw_ref, cb_ref,
                     wa_ref, ba_ref, wx_ref, bx_ref, lam_ref, wout_ref, sc_ref, sh_ref,
                     y_ref, conv_ref, hl_ref, *, nb, nt):
    x0 = x_ref[...]
    y0 = _dot((o_ref[...] * gs_ref[...].astype(F32)).astype(BF16), wout0_ref[...])
    x = x0 + mod0_ref[:, 2 * D_MODEL:3 * D_MODEL] * y0
    mod = mod_ref[...]
    h = _modulated_norm(x, g_ref[...], mod).astype(BF16)
    xb = _dot(h, win_ref[:, 0:LRU_WIDTH])
    gg = _dot(h, win_ref[:, LRU_WIDTH:2 * LRU_WIDTH])
    slabs = [sc_ref[j] for j in range(CONV_W - 1)] + [xb[t * nb:(t + 1) * nb] for t in range(nt)]
    xcs = []
    for t in range(nt):
        acc = cb_ref[...] + cw_ref[0:1] * slabs[t]
        for j in range(1, CONV_W):
            acc = acc + cw_ref[j:j + 1] * slabs[t + j]
        xcs.append(acc)
    for j in range(CONV_W - 1):
        conv_ref[j] = slabs[nt + j]
    xc = jnp.concatenate(xcs, axis=0)
    a, b = _lru_gates(xc, wa_ref, ba_ref, wx_ref, bx_ref, lam_ref)
    hprev = sh_ref[...]
    hs = []
    for t in range(nt):
        hprev = a[t * nb:(t + 1) * nb] * hprev + b[t * nb:(t + 1) * nb]
        hs.append(hprev)
    hl_ref[...] = hprev
    out = _dot((jnp.concatenate(hs, axis=0) * _silu(gg)).astype(BF16), wout_ref[...])
    y_ref[...] = x + mod[:, 2 * D_MODEL:3 * D_MODEL] * out


def _lru_sample(o, gs, x, mod0, wout0, mod, g, win, cw, cb, wa, ba, wx, bx, lam, wout, sc, sh, *, nb, nt):
    args = (o, gs, x, mod0, wout0, mod, g, win, cw, cb, wa, ba, wx, bx, lam, wout, sc, sh)

    def full(a):
        nd = a.ndim
        return pl.BlockSpec(a.shape, lambda i, nd=nd: (0,) * nd)

    out_shape = [
        jax.ShapeDtypeStruct((nt * nb, D_MODEL), F32),
        jax.ShapeDtypeStruct((CONV_W - 1, nb, LRU_WIDTH), F32),
        jax.ShapeDtypeStruct((nb, LRU_WIDTH), F32),
    ]
    body = functools.partial(_lru_sample_body, nb=nb, nt=nt)
    return pl.pallas_call(
        body,
        grid=(1,),
        in_specs=[full(a) for a in args],
        out_specs=[pl.BlockSpec(s.shape, lambda i, nd=len(s.shape): (0,) * nd) for s in out_shape],
        out_shape=out_shape,
        compiler_params=pltpu.CompilerParams(
            dimension_semantics=("arbitrary",), vmem_limit_bytes=VMEM_LIMIT),
        name="lru_sample",
    )(*args)


def kernel(x_prompt, x_sample, cache_k, cache_v, cache_idx_k, state_conv, state_h, page_table, c_prompt, c_sample, norm_g, ada_w, ada_b, attn_w_in, attn_q_norm, attn_k_norm, attn_w_out, lru_w_in, lru_conv_w, lru_conv_b, lru_w_a, lru_b_a, lru_w_x, lru_b_x, lru_lam, lru_w_out):
    B, S, _ = x_prompt.shape
    Bd, T, _ = x_sample.shape
    n_pages = page_table.shape[1]
    past = n_pages * PAGE_SIZE
    n_pool = cache_k.shape[1]
    assert S % QB == 0 and S % TL == 0 and n_pages % PAGES_PER_STEP == 0
    assert Bd % SEQ_PER_STEP == 0 and SEQ_PER_STEP * T == SUBLANES

    mod = _ada(jnp.concatenate([c_prompt, c_sample], axis=0), ada_w, ada_b)
    mod_p = [mod[l, :B].reshape(B, 1, 3 * D_MODEL) for l in range(2)]
    mod_s = [jnp.repeat(mod[l, B:], T, axis=0) for l in range(2)]

    w = attn_w_in[0]
    o_k, o_v, o_qi, o_ki, o_wi, o_g = 1024, 1280, 1536, 2560, 2624, 2640
    wcat = jnp.concatenate(
        [w[:, :o_ki], w[:, o_g:], w[:, o_ki:o_g], jnp.zeros((D_MODEL, PROJ_WIDTH - w.shape[1]), F32)],
        axis=1).astype(BF16)
    g0 = norm_g[0].reshape(1, D_MODEL)
    qn = attn_q_norm[0].reshape(1, HEAD_DIM)
    kn = attn_k_norm[0].reshape(1, HEAD_DIM)
    wout0 = attn_w_out[0].astype(BF16)

    tm = 512
    tabs_p = _rope_tables(jnp.arange(S))
    xp = x_prompt.reshape(B * S, D_MODEL)
    q, k, v, qi, kw, ki, gs = _proj(xp, mod_p[0], g0, wcat, qn, kn, tabs_p,
                                    tm=tm, rows_per_mod=S, tab_blocks=S // tm)
    x1p = _attn_prompt(q, qi, kw, k, v, gs, xp, mod_p[0], wout0, batch=B, seq=S)

    pos_s = past + jnp.arange(T)
    tabs_s = jnp.tile(_rope_tables(pos_s), (1, Bd, 1))
    xs = x_sample.reshape(Bd * T, D_MODEL)
    q2, k2, v2, qi2, kw2, ki2, gs2 = _proj(xs, mod_s[0].reshape(1, Bd * T, 3 * D_MODEL), g0, wcat, qn, kn,
                                           tabs_s, tm=Bd * T, rows_per_mod=Bd * T, tab_blocks=1)
    qir = qi2.reshape(Bd, T, N_IDX_HEADS, IDX_DIM).transpose(0, 2, 1, 3).reshape(Bd, N_IDX_HEADS * T, IDX_DIM)
    wcol = kw2[:, IDX_DIM:IDX_DIM + N_IDX_HEADS].reshape(Bd, T, N_IDX_HEADS).transpose(0, 2, 1)
    wcol = wcol.reshape(Bd, N_IDX_HEADS * T, 1)
    pad_rows = lambda a: jnp.pad(a, ((0, 0), (0, PAGE_SIZE - T)) + ((0, 0),) * (a.ndim - 2))
    ki_new = pad_rows(ki2.reshape(Bd, T, IDX_DIM)).transpose(0, 2, 1)
    cache_ik_t = cache_idx_k.reshape(n_pool, PAGE_SIZE, IDX_DIM).transpose(0, 2, 1)
    scores = _sample_scores(page_table, qir, wcol, cache_ik_t, ki_new,
                            dec_batch=Bd, dec_seq=T, n_pages=n_pages)
    sel = _sample_select(scores, past=past, dec_seq=T)
    qs = q2.reshape(Bd, T, N_KV_HEADS, GROUP, HEAD_DIM).transpose(0, 2, 3, 1, 4)
    qs = qs.reshape(Bd, N_KV_HEADS * GROUP * T, HEAD_DIM)
    new_pad = ((0, 0), (0, (PAGE_SIZE - T) * N_KV_HEADS), (0, 0))
    k_new = jnp.pad(k2.reshape(Bd, T * N_KV_HEADS, HEAD_DIM), new_pad)
    v_new = jnp.pad(v2.reshape(Bd, T * N_KV_HEADS, HEAD_DIM), new_pad)
    sel = jnp.repeat(sel, N_KV_HEADS, axis=1)
    o2 = _sample_attn(page_table, qs, sel,
                      cache_k.reshape(n_pool, PAGE_SIZE * N_KV_HEADS, HEAD_DIM),
                      cache_v.reshape(n_pool, PAGE_SIZE * N_KV_HEADS, HEAD_DIM),
                      k_new, v_new, dec_batch=Bd, dec_seq=T, n_pages=n_pages)
    o2 = o2.reshape(Bd, N_KV_HEADS, GROUP, T, HEAD_DIM).transpose(3, 0, 1, 2, 4).reshape(T * Bd, ATTN_WIDTH)

    g1 = norm_g[1].reshape(1, D_MODEL)
    win = lru_w_in[0].astype(BF16)
    cw = lru_conv_w[0]
    cb = lru_conv_b[0].reshape(1, LRU_WIDTH)
    wa = lru_w_a[0].astype(BF16)
    wx = lru_w_x[0].astype(BF16)
    ba = lru_b_a[0].reshape(1, LRU_WIDTH)
    bx = lru_b_x[0].reshape(1, LRU_WIDTH)
    lam = lru_lam[0].reshape(1, LRU_WIDTH)
    wout1 = lru_w_out[0].astype(BF16)

    yp, conv_p, h_p = _lru_prompt(x1p, mod_p[1], g1, win, cw, cb, wa, ba, wx, bx, lam, wout1, batch=B, seq=S)

    tmaj = lambda a: a.reshape(Bd, T, -1).transpose(1, 0, 2).reshape(T * Bd, -1)
    ys, conv_s, h_s = _lru_sample(
        o2, tmaj(gs2), tmaj(xs), tmaj(mod_s[0]), wout0, tmaj(mod_s[1]), g1, win, cw, cb, wa, ba, wx, bx,
        lam, wout1, state_conv[0].transpose(1, 0, 2), state_h[0], nb=Bd, nt=T)

    y_prompt = yp.reshape(B, S, D_MODEL)
    y_sample = ys.reshape(T, Bd, D_MODEL).transpose(1, 0, 2)
    return (y_prompt, y_sample,
            k.reshape(1, B, S, N_KV_HEADS, HEAD_DIM), v.reshape(1, B, S, N_KV_HEADS, HEAD_DIM),
            ki.reshape(1, B, S, IDX_DIM),
            k2.reshape(1, Bd, T, N_KV_HEADS, HEAD_DIM), v2.reshape(1, Bd, T, N_KV_HEADS, HEAD_DIM),
            ki2.reshape(1, Bd, T, IDX_DIM),
            conv_p.reshape(1, B, CONV_W - 1, LRU_WIDTH), h_p.reshape(1, B, LRU_WIDTH),
            conv_s.transpose(1, 0, 2).reshape(1, Bd, CONV_W - 1, LRU_WIDTH), h_s.reshape(1, Bd, LRU_WIDTH))
```

```python
import functools
import math

import jax
import jax.numpy as jnp
from jax import lax
from jax.experimental import pallas as pl
from jax.experimental.pallas import tpu as pltpu

F32 = jnp.float32
BF16 = jnp.bfloat16
I32 = jnp.int32

D_MODEL = 1024
N_HEADS = 8
N_KV_HEADS = 2
HEAD_DIM = 128
GROUP = N_HEADS // N_KV_HEADS
ATTN_WIDTH = N_HEADS * HEAD_DIM
KV_WIDTH = N_KV_HEADS * HEAD_DIM
N_IDX_HEADS = 16
IDX_DIM = 64
IDX_WIDTH = N_IDX_HEADS * IDX_DIM
TOPK_MAX = 256
ROPE_THETA = 500000.0
ROT_FRAC = 4
PAGE_SIZE = 128
LRU_WIDTH = D_MODEL
N_LRU_BLOCKS = 4
LRU_BLOCK = LRU_WIDTH // N_LRU_BLOCKS
CONV_W = 4
LRU_C = 8.0
EPS = 1e-6

LANES = 128
SUBLANES = 8
MXU_COLS = 256
VMEM_LIMIT = 56 * 1024 * 1024

OFF_Q = 0
OFF_K = OFF_Q + ATTN_WIDTH
OFF_V = OFF_K + KV_WIDTH
OFF_QI = OFF_V + KV_WIDTH
OFF_G = OFF_QI + IDX_WIDTH
OFF_KW = OFF_G + ATTN_WIDTH
PROJ_WIDTH = OFF_KW + LANES

INT_MIN = -(2 ** 31)
KEY_NEG_INF = 0x807FFFFF - 2 ** 32
RADIX_UNROLL = 4
RADIX_TRIPS = 32 // RADIX_UNROLL
COUNT_SLAB_ROWS = 64
LOG2E = 1.4426950408889634
NT_DIMS = (((1,), (1,)), ((), ()))


def _dot(a, b):
    return jnp.dot(a, b, preferred_element_type=F32)


def _dot_nt(a, b):
    return lax.dot_general(a, b, NT_DIMS, preferred_element_type=F32)


def _silu(x):
    return x / (1.0 + jnp.exp(-x))


def _sigmoid(x):
    return 1.0 / (1.0 + jnp.exp(-x))


def _rmsnorm(x, g):
    return x * lax.rsqrt(jnp.mean(x * x, axis=-1, keepdims=True) + EPS) * g


def _modulated_norm(x, g, mod):
    shift = mod[:, 0:D_MODEL]
    scale = mod[:, D_MODEL:2 * D_MODEL]
    return _rmsnorm(x, g) * (1.0 + scale) + shift


def _rope(y, tabs_ref, base, half):
    c = tabs_ref[base]
    s1 = tabs_ref[base + 1]
    s2 = tabs_ref[base + 2]
    return y * c + pltpu.roll(y, LANES - half, axis=1) * s1 + pltpu.roll(y, half, axis=1) * s2


def _key_to_f32(key):
    bits = jnp.where(key >= 0, key, key ^ 0x7FFFFFFF)
    return pltpu.bitcast(bits, F32)


def _count_where(sc_ref, n, axis, window, pred):
    slab = COUNT_SLAB_ROWS if axis == 0 else LANES
    assert n % slab == 0
    window = slice(None) if window is None else window
    parts = []
    for j in range(n // slab):
        span = slice(j * slab, (j + 1) * slab)
        sc = sc_ref[span, window] if axis == 0 else sc_ref[window, span]
        parts.append(jnp.where(pred(sc, j * slab), 1.0, 0.0))
    while len(parts) > 1:
        nxt = [parts[a] + parts[a + 1] for a in range(0, len(parts) - 1, 2)]
        if len(parts) % 2:
            nxt.append(parts[-1])
        parts = nxt
    return jnp.sum(parts[0], axis=axis, keepdims=True)


def _tie_cut(sc_ref, n, k, axis, window, thr, at_least, cut_ref):
    kf = float(k)
    thr_f = _key_to_f32(thr)
    excess = jnp.logical_and(at_least > kf, thr > KEY_NEG_INF)
    cut_ref[...] = jnp.full(cut_ref.shape, n, I32)
    nbits = (n - 1).bit_length()

    @pl.when(jnp.max(jnp.where(excess, 1.0, 0.0)) > 0.5)
    def _():
        need = kf - _count_where(sc_ref, n, axis, window, lambda sc, base: sc > thr_f)

        def step(it, p):
            cand = p + lax.shift_left(jnp.int32(1), nbits - 1 - it)
            cnt = _count_where(
                sc_ref, n, axis, window,
                lambda sc, base: jnp.logical_and(
                    sc == thr_f, base + lax.broadcasted_iota(I32, sc.shape, axis) < cand))
            return jnp.where(cnt < need, cand, p)

        p = lax.fori_loop(0, nbits, step, jnp.zeros(thr.shape, I32))
        cut_ref[...] = jnp.where(excess, p, n)

    return cut_ref[...]


def _kth_largest_key(sc_ref, n, k, axis, window=None, side_work=None, side_trips=0):
    kf = float(k)
    shape = list(sc_ref.shape)
    if window is None:
        window = slice(0, shape[1 - axis])
    shape[1 - axis] = window.stop - window.start
    shape[axis] = 1

    def body(it, carry):
        prefix, at_least = carry
        cand = prefix + lax.shift_left(jnp.int32(1), 31 - it)
        cand_f = _key_to_f32(cand)
        cnt = _count_where(sc_ref, n, axis, window, lambda sc, base: sc >= cand_f)
        take = cnt >= kf
        return jnp.where(take, cand, prefix), jnp.where(take, cnt, at_least)

    def trip(with_side, t, carry):
        for e in range(RADIX_UNROLL):
            carry = body(t * RADIX_UNROLL + e, carry)
        if with_side:
            side_work(t)
        return carry

    assert side_trips <= RADIX_TRIPS
    carry = (jnp.full(tuple(shape), INT_MIN, I32), jnp.full(tuple(shape), float(n), F32))
    if side_trips:
        carry = lax.fori_loop(0, side_trips, functools.partial(trip, True), carry)
    return lax.fori_loop(side_trips, RADIX_TRIPS, functools.partial(trip, False), carry)


def _ada_body(c_ref, w_ref, b_ref, o_ref):
    s = _silu(c_ref[...]).astype(BF16)
    o_ref[0] = _dot(s, w_ref[0].astype(BF16)) + b_ref[0]


def _ada(c_all, ada_w, ada_b):
    rows = c_all.shape[0]
    depth = ada_w.shape[0]
    nblk = 3
    return pl.pallas_call(
        _ada_body,
        grid=(depth, nblk),
        in_specs=[
            pl.BlockSpec((rows, D_MODEL), lambda l, j: (0, 0)),
            pl.BlockSpec((1, D_MODEL, D_MODEL), lambda l, j: (l, 0, j)),
            pl.BlockSpec((1, 1, D_MODEL), lambda l, j: (l, 0, j)),
        ],
        out_specs=pl.BlockSpec((1, rows, D_MODEL), lambda l, j: (l, 0, j)),
        out_shape=jax.ShapeDtypeStruct((depth, rows, 3 * D_MODEL), F32),
        compiler_params=pltpu.CompilerParams(
            dimension_semantics=("arbitrary", "arbitrary"), vmem_limit_bytes=VMEM_LIMIT),
        name="ada_mod",
    )(c_all, ada_w, ada_b.reshape(depth, 1, 3 * D_MODEL))


def _proj_body(x_ref, mod_ref, g_ref, w_ref, qn_ref, kn_ref, tabs_ref,
               q_ref, k_ref, v_ref, qi_ref, kw_ref, ki_ref, gs_ref):
    h = _modulated_norm(x_ref[...], g_ref[...], mod_ref[0]).astype(BF16)
    qn = qn_ref[...]
    kn = kn_ref[...]
    half_h = HEAD_DIM // ROT_FRAC // 2
    half_i = IDX_DIM // ROT_FRAC // 2

    def slabs(off, n_slabs):
        for c0 in range(0, n_slabs, MXU_COLS // LANES):
            z = _dot(h, w_ref[:, off + c0 * LANES:off + c0 * LANES + MXU_COLS])
            for e in range(MXU_COLS // LANES):
                yield c0 + e, z[:, e * LANES:(e + 1) * LANES]

    for hh, z in slabs(OFF_Q, N_HEADS):
        q_ref[:, hh * HEAD_DIM:(hh + 1) * HEAD_DIM] = _rope(_rmsnorm(z, qn), tabs_ref, 0, half_h).astype(BF16)
    tm = x_ref.shape[0]
    for hh, z in slabs(OFF_K, N_KV_HEADS):
        k_ref[pl.ds(hh, tm, stride=N_KV_HEADS), :] = _rope(_rmsnorm(z, kn), tabs_ref, 0, half_h)
    for hh, z in slabs(OFF_V, N_KV_HEADS):
        v_ref[pl.ds(hh, tm, stride=N_KV_HEADS), :] = z
    for c, z in slabs(OFF_QI, IDX_WIDTH // LANES):
        qi_ref[:, c * LANES:(c + 1) * LANES] = _rope(z, tabs_ref, 3, half_i).astype(BF16)
    for c, z in slabs(OFF_G, ATTN_WIDTH // LANES):
        gs_ref[:, c * LANES:(c + 1) * LANES] = _silu(z).astype(BF16)
    kw = _rope(_dot(h, w_ref[:, OFF_KW:OFF_KW + LANES]), tabs_ref, 6, half_i)
    kw_ref[...] = kw
    ki_ref[...] = kw[:, 0:IDX_DIM]


def _proj(x, mod, g, w, qn, kn, tabs, *, tm, rows_per_mod, tab_blocks):
    n = x.shape[0]
    mod_rows = mod.shape[1]
    grid = (n // tm,)
    row = lambda i: (i, 0)
    outs = [
        jax.ShapeDtypeStruct((n, ATTN_WIDTH), BF16),
        jax.ShapeDtypeStruct((n * N_KV_HEADS, HEAD_DIM), F32),
        jax.ShapeDtypeStruct((n * N_KV_HEADS, HEAD_DIM), F32),
        jax.ShapeDtypeStruct((n, IDX_WIDTH), BF16),
        jax.ShapeDtypeStruct((n, LANES), F32),
        jax.ShapeDtypeStruct((n, IDX_DIM), F32),
        jax.ShapeDtypeStruct((n, ATTN_WIDTH), BF16),
    ]
    return pl.pallas_call(
        _proj_body,
        grid=grid,
        in_specs=[
            pl.BlockSpec((tm, D_MODEL), row),
            pl.BlockSpec((1, mod_rows, 3 * D_MODEL), lambda i: (i * tm // rows_per_mod, 0, 0)),
            pl.BlockSpec((1, D_MODEL), lambda i: (0, 0)),
            pl.BlockSpec((D_MODEL, PROJ_WIDTH), lambda i: (0, 0)),
            pl.BlockSpec((1, HEAD_DIM), lambda i: (0, 0)),
            pl.BlockSpec((1, HEAD_DIM), lambda i: (0, 0)),
            pl.BlockSpec((9, tm, LANES), lambda i: (0, i % tab_blocks, 0)),
        ],
        out_specs=[
            pl.BlockSpec((tm, ATTN_WIDTH), row),
            pl.BlockSpec((tm * N_KV_HEADS, HEAD_DIM), row),
            pl.BlockSpec((tm * N_KV_HEADS, HEAD_DIM), row),
            pl.BlockSpec((tm, IDX_WIDTH), row),
            pl.BlockSpec((tm, LANES), row),
            pl.BlockSpec((tm, IDX_DIM), row),
            pl.BlockSpec((tm, ATTN_WIDTH), row),
        ],
        out_shape=outs,
        compiler_params=pltpu.CompilerParams(
            dimension_semantics=("arbitrary",), vmem_limit_bytes=VMEM_LIMIT),
        name="attn_proj",
    )(x, mod, g, w, qn, kn, tabs)


def _rope_tables(pos):
    posf = pos.astype(F32)
    t = pos.shape[0]

    def base(d):
        r = d // ROT_FRAC
        half = r // 2
        inv = jnp.exp(-jnp.log(jnp.asarray(ROPE_THETA, F32)) * jnp.arange(half, dtype=F32) * 2.0 / r)
        ang = posf[:, None] * inv[None, :]
        cos = jnp.cos(ang)
        sin = jnp.sin(ang)
        c = jnp.concatenate([cos, cos, jnp.ones((t, d - r), F32)], axis=1)
        s1 = jnp.concatenate([-sin, jnp.zeros((t, d - half), F32)], axis=1)
        s2 = jnp.concatenate([jnp.zeros((t, half), F32), sin, jnp.zeros((t, d - r), F32)], axis=1)
        return c, s1, s2

    hc, hs1, hs2 = base(HEAD_DIM)
    ic, is1, is2 = base(IDX_DIM)
    wi_scale = N_IDX_HEADS ** -0.5 * IDX_DIM ** -0.5
    pad = LANES - IDX_DIM
    kc = jnp.concatenate([ic, jnp.full((t, N_IDX_HEADS), wi_scale, F32),
                          jnp.zeros((t, pad - N_IDX_HEADS), F32)], axis=1)
    ks1 = jnp.concatenate([is1, jnp.zeros((t, pad), F32)], axis=1)
    ks2 = jnp.concatenate([is2, jnp.zeros((t, pad), F32)], axis=1)
    two = lambda a: jnp.concatenate([a, a], axis=1)
    return jnp.stack([hc, hs1, hs2, two(ic), two(is1), two(is2), kc, ks1, ks2])


QB = 128
SUB_BLOCKS = 1
KEY_BUCKET = 256


def _attn_prompt_body(q_ref, qi_ref, kwb_ref, kws_ref, k_ref, v_ref, gs_ref, x_ref, gate_ref,
                      wout_ref, o_ref, kbf, vbf, kipar, sct_ref, s_ref, cut_ref, *, seq, k_sel):
    i = pl.program_id(1)
    assert SUB_BLOCKS == 1

    @pl.when(i == 0)
    def _():
        ones_col = jnp.where(lax.broadcasted_iota(I32, (seq, HEAD_DIM), 1) == 0, 1.0, 0.0).astype(BF16)
        for kh in range(N_KV_HEADS):
            kbf[kh] = k_ref[pl.ds(kh, seq, stride=N_KV_HEADS), :].astype(BF16)
            vbf[kh, :, 0:HEAD_DIM] = v_ref[pl.ds(kh, seq, stride=N_KV_HEADS), :].astype(BF16)
            vbf[kh, :, HEAD_DIM:2 * HEAD_DIM] = ones_col
        kw = kws_ref[...]
        lane = lax.broadcasted_iota(I32, kw.shape, 1)
        ke = jnp.where(lane < IDX_DIM, kw, 0.0)
        kipar[0] = ke.astype(BF16)
        kipar[1] = pltpu.roll(ke, IDX_DIM, axis=1).astype(BF16)

    def stacked_q(rows, kh):
        return jnp.concatenate(
            [q_ref[rows, (kh * GROUP + g) * HEAD_DIM:(kh * GROUP + g + 1) * HEAD_DIM] for g in range(GROUP)],
            axis=0)

    def select(sb, nk):
        rows = slice(sb * QB, (sb + 1) * QB)
        first = (i * SUB_BLOCKS + sb) * QB
        col = lax.broadcasted_iota(I32, (QB, nk), 1)
        pos = first + lax.broadcasted_iota(I32, (QB, nk), 0)
        if nk <= k_sel:
            return jnp.where(col <= pos, 0.0, -jnp.inf)
        kwb = kwb_ref[rows, :]
        score = None
        for p in range(N_IDX_HEADS // 2):
            pair = None
            for par in range(2):
                h = 2 * p + par
                d = _dot_nt(qi_ref[rows, p * LANES:(p + 1) * LANES], kipar[par, 0:nk, :])
                term = kwb[:, IDX_DIM + h:IDX_DIM + h + 1] * jnp.maximum(d, 0.0)
                pair = term if pair is None else pair + term
            score = pair if score is None else score + pair
        score = jnp.where(col <= pos, score, -jnp.inf)
        sct_ref[0:nk, rows] = score.T

        def qk_tile(t):
            off = pl.multiple_of(t * MXU_COLS, MXU_COLS)
            for kh in range(N_KV_HEADS):
                s_ref[kh, :, pl.ds(off, MXU_COLS)] = _dot_nt(stacked_q(rows, kh), kbf[kh, pl.ds(off, MXU_COLS), :])

        assert nk % MXU_COLS == 0
        thr, at_least = _kth_largest_key(sct_ref, nk, k_sel, 0, window=rows,
                                         side_work=qk_tile, side_trips=nk // MXU_COLS)
        cut = _tie_cut(sct_ref, nk, k_sel, 0, rows, thr, at_least, cut_ref)
        thr_f = _key_to_f32(thr)
        sc = sct_ref[0:nk, rows]
        key_row = lax.broadcasted_iota(I32, (nk, QB), 0)
        q_pos = first + lax.broadcasted_iota(I32, (nk, QB), 1)
        keep = jnp.logical_or(sc > thr_f, jnp.logical_and(sc == thr_f, key_row <= cut))
        sel_t = jnp.logical_and(key_row <= q_pos, jnp.logical_or(keep, thr <= KEY_NEG_INF))
        return jnp.where(sel_t, 0.0, -jnp.inf).T

    def attend(sb, nk, bias):
        rows = slice(sb * QB, (sb + 1) * QB)
        c = HEAD_DIM ** -0.5 * LOG2E
        heads = [None] * N_HEADS
        for kh in range(N_KV_HEADS):
            if nk <= k_sel:
                s = _dot_nt(stacked_q(rows, kh), kbf[kh, 0:nk, :])
            else:
                s = s_ref[kh, :, 0:nk]
            ps = []
            for g in range(GROUP):
                sg = s[g * QB:(g + 1) * QB] + bias
                ps.append(jnp.exp2((sg - jnp.max(sg, axis=1, keepdims=True)) * c).astype(BF16))
            o = _dot(jnp.concatenate(ps, axis=0), vbf[kh, 0:nk, :])
            for g in range(GROUP):
                og = o[g * QB:(g + 1) * QB]
                heads[kh * GROUP + g] = og[:, 0:HEAD_DIM] / og[:, HEAD_DIM:HEAD_DIM + 1]
        attn = jnp.concatenate(heads, axis=1)
        y = _dot((attn * gs_ref[rows, :].astype(F32)).astype(BF16), wout_ref[...])
        o_ref[rows, :] = x_ref[rows, :] + gate_ref[0] * y

    def block(nk):
        biases = [select(sb, nk) for sb in range(SUB_BLOCKS)]
        for sb in range(SUB_BLOCKS):
            attend(sb, nk, biases[sb])

    steps_per_bucket = KEY_BUCKET // (SUB_BLOCKS * QB)
    for bucket in range(seq // KEY_BUCKET):
        pl.when(i // steps_per_bucket == bucket)(functools.partial(block, (bucket + 1) * KEY_BUCKET))


def _attn_prompt(q, qi, kw, k, v, gs, x, mod, wout, *, batch, seq):
    step = SUB_BLOCKS * QB
    nb = seq // step
    k_sel = min(TOPK_MAX, seq // 4)
    blk = lambda b, i: (b * nb + i, 0)
    whole = lambda b, i: (b, 0)
    body = functools.partial(_attn_prompt_body, seq=seq, k_sel=k_sel)
    return pl.pallas_call(
        body,
        grid=(batch, nb),
        in_specs=[
            pl.BlockSpec((step, ATTN_WIDTH), blk),
            pl.BlockSpec((step, IDX_WIDTH), blk),
            pl.BlockSpec((step, LANES), blk),
            pl.BlockSpec((seq, LANES), whole),
            pl.BlockSpec((seq * N_KV_HEADS, HEAD_DIM), whole),
            pl.BlockSpec((seq * N_KV_HEADS, HEAD_DIM), whole),
            pl.BlockSpec((step, ATTN_WIDTH), blk),
            pl.BlockSpec((step, D_MODEL), blk),
            pl.BlockSpec((1, 1, D_MODEL), lambda b, i: (b, 0, 2)),
            pl.BlockSpec((ATTN_WIDTH, D_MODEL), lambda b, i: (0, 0)),
        ],
        out_specs=pl.BlockSpec((step, D_MODEL), blk),
        out_shape=jax.ShapeDtypeStruct((batch * seq, D_MODEL), F32),
        scratch_shapes=[
            pltpu.VMEM((N_KV_HEADS, seq, HEAD_DIM), BF16),
            pltpu.VMEM((N_KV_HEADS, seq, 2 * HEAD_DIM), BF16),
            pltpu.VMEM((2, seq, LANES), BF16),
            pltpu.VMEM((seq, step), F32),
            pltpu.VMEM((N_KV_HEADS, GROUP * QB, seq), F32),
            pltpu.VMEM((1, QB), I32),
        ],
        compiler_params=pltpu.CompilerParams(
            dimension_semantics=("arbitrary", "arbitrary"), vmem_limit_bytes=VMEM_LIMIT),
        name="attn_prompt",
    )(q, qi, kw, kw, k, v, gs, x, mod, wout)


PAGES_PER_STEP = 16
SCORE_PAGES_PER_STEP = 32
SEQ_PER_STEP = 2


def _sample_tile_score(qir, wcol, kpage_t):
    d = _dot(qir, kpage_t.astype(BF16))
    r = wcol * jnp.maximum(d, 0.0)
    acc = r[0:SUBLANES]
    for j in range(1, r.shape[0] // SUBLANES):
        acc = acc + r[j * SUBLANES:(j + 1) * SUBLANES]
    half = SUBLANES // 2
    return acc[0:half] + acc[half:SUBLANES]


def _sample_score_body(pt_ref, qir_ref, wcol_ref, *rest, n_chunks, dec_seq):
    per_seq = SCORE_PAGES_PER_STEP
    npg = SEQ_PER_STEP * per_seq
    pages = rest[:npg]
    knew_ref = rest[npg]
    o_ref = rest[npg + 1]
    c = pl.program_id(1)

    @pl.when(c < n_chunks)
    def _():
        for s in range(SEQ_PER_STEP):
            kcat = jnp.concatenate([pages[s * per_seq + j][0] for j in range(per_seq)], axis=1)
            o_ref[s * dec_seq:(s + 1) * dec_seq, :] = _sample_tile_score(qir_ref[s], wcol_ref[s], kcat)

    @pl.when(c == n_chunks)
    def _():
        o_ref[...] = jnp.zeros(o_ref.shape, F32)
        for s in range(SEQ_PER_STEP):
            o_ref[s * dec_seq:(s + 1) * dec_seq, 0:PAGE_SIZE] = _sample_tile_score(
                qir_ref[s], wcol_ref[s], knew_ref[s])


def _sample_scores(page_table, qir, wcol, cache_ik, knew, *, dec_batch, dec_seq, n_pages):
    per_seq = SCORE_PAGES_PER_STEP
    assert n_pages % per_seq == 0 and per_seq % PAGES_PER_STEP == 0
    n_chunks = n_pages // per_seq
    width = (n_chunks + 1) * per_seq * PAGE_SIZE
    rows = IDX_DIM

    def page_map(s, j):
        def f(bp, c, pt):
            cc = jnp.minimum(c, n_chunks - 1)
            return (pt[bp * SEQ_PER_STEP + s, cc * per_seq + j], 0, 0)
        return f

    in_specs = [
        pl.BlockSpec((SEQ_PER_STEP, rows, IDX_DIM), lambda bp, c, pt: (bp, 0, 0)),
        pl.BlockSpec((SEQ_PER_STEP, rows, 1), lambda bp, c, pt: (bp, 0, 0)),
    ]
    for s in range(SEQ_PER_STEP):
        for j in range(per_seq):
            in_specs.append(pl.BlockSpec((1, IDX_DIM, PAGE_SIZE), page_map(s, j)))
    in_specs.append(pl.BlockSpec((SEQ_PER_STEP, IDX_DIM, PAGE_SIZE), lambda bp, c, pt: (bp, 0, 0)))
    body = functools.partial(_sample_score_body, n_chunks=n_chunks, dec_seq=dec_seq)
    return pl.pallas_call(
        body,
        grid_spec=pltpu.PrefetchScalarGridSpec(
            num_scalar_prefetch=1,
            grid=(dec_batch // SEQ_PER_STEP, n_chunks + 1),
            in_specs=in_specs,
            out_specs=pl.BlockSpec((SEQ_PER_STEP * dec_seq, per_seq * PAGE_SIZE),
                                   lambda bp, c, pt: (bp, c)),
        ),
        out_shape=jax.ShapeDtypeStruct((dec_batch * dec_seq, width), F32),
        compiler_params=pltpu.CompilerParams(
            dimension_semantics=("arbitrary", "arbitrary"), vmem_limit_bytes=VMEM_LIMIT),
        name="sample_scores",
    )(page_table, qir, wcol, *([cache_ik] * (SEQ_PER_STEP * per_seq)), knew)


SELECT_ROWS = 32


def _sample_select_body(sc_ref, sel_ref, key_ref, cut_ref, *, past, dec_seq, k_sel):
    shape = sc_ref.shape
    col = lax.broadcasted_iota(I32, shape, 1)
    t = lax.broadcasted_iota(I32, shape, 0) % dec_seq
    adm = (col - past) <= t
    key_ref[...] = jnp.where(adm, sc_ref[...], -jnp.inf)
    thr, at_least = _kth_largest_key(key_ref, shape[1], k_sel, 1)
    cut = _tie_cut(key_ref, shape[1], k_sel, 1, None, thr, at_least, cut_ref)
    thr_f = _key_to_f32(thr)
    sc = key_ref[...]
    keep = jnp.logical_or(sc > thr_f, jnp.logical_and(sc == thr_f, col <= cut))
    picked = jnp.logical_or(keep, thr <= KEY_NEG_INF)
    sel_ref[...] = jnp.where(jnp.logical_and(picked, adm), 1.0, 0.0)


def _sample_select(scores, *, past, dec_seq):
    k_sel = min(TOPK_MAX, (past + dec_seq) // 4)
    body = functools.partial(_sample_select_body, past=past, dec_seq=dec_seq, k_sel=k_sel)
    rows, width = scores.shape
    assert rows % SELECT_ROWS == 0 and SELECT_ROWS % dec_seq == 0
    return pl.pallas_call(
        body,
        grid=(rows // SELECT_ROWS,),
        in_specs=[pl.BlockSpec((SELECT_ROWS, width), lambda i: (i, 0))],
        out_specs=pl.BlockSpec((SELECT_ROWS, width), lambda i: (i, 0)),
        out_shape=jax.ShapeDtypeStruct(scores.shape, F32),
        scratch_shapes=[pltpu.VMEM((SELECT_ROWS, width), F32), pltpu.VMEM((SELECT_ROWS, 1), I32)],
        compiler_params=pltpu.CompilerParams(
            dimension_semantics=("arbitrary",), vmem_limit_bytes=VMEM_LIMIT),
        name="sample_select",
    )(scores)


def _sample_attn_body(pt_ref, q_ref, sel_ref, *rest, n_chunks, dec_seq):
    npg = SEQ_PER_STEP * PAGES_PER_STEP
    kpages = rest[:npg]
    vpages = rest[npg:2 * npg]
    knew_ref, vnew_ref, o_ref, m_ref, l_ref, acc_ref = rest[2 * npg:]
    c = pl.program_id(1)
    scale = HEAD_DIM ** -0.5
    rows_per_head = GROUP * dec_seq
    rows_per_head_log2 = rows_per_head.bit_length() - 1
    assert rows_per_head == 1 << rows_per_head_log2 and N_KV_HEADS & (N_KV_HEADS - 1) == 0

    @pl.when(c == 0)
    def _():
        m_ref[...] = jnp.full(m_ref.shape, -jnp.inf, F32)
        l_ref[...] = jnp.zeros(l_ref.shape, F32)
        acc_ref[...] = jnp.zeros(acc_ref.shape, F32)

    def update(tiles, finish):
        m_old = [m_ref[s] for s in range(SEQ_PER_STEP)]
        l_old = [l_ref[s] for s in range(SEQ_PER_STEP)]
        a_old = [acc_ref[s] for s in range(SEQ_PER_STEP)]
        m_out, l_out, a_out = [], [], []
        for s, (kt, vt, sel) in enumerate(tiles):
            sc = _dot_nt(q_ref[s], kt) * scale
            key_head = jnp.bitwise_and(lax.broadcasted_iota(I32, sc.shape, 1), N_KV_HEADS - 1)
            row_head = jnp.right_shift(lax.broadcasted_iota(I32, sc.shape, 0), rows_per_head_log2)
            picked = jnp.concatenate([sel] * (N_KV_HEADS * GROUP), axis=0) > 0.5
            sc = jnp.where(jnp.logical_and(picked, key_head == row_head), sc, -jnp.inf)
            m_new = jnp.maximum(m_old[s], jnp.max(sc, axis=1, keepdims=True))
            m_safe = jnp.where(m_new == -jnp.inf, 0.0, m_new)
            alpha = jnp.exp(m_old[s] - m_safe)
            p = jnp.exp(sc - m_safe)
            m_out.append(m_new)
            l_out.append(alpha * l_old[s] + jnp.sum(p, axis=1, keepdims=True))
            a_out.append(alpha * a_old[s] + _dot(p.astype(BF16), vt))
        for s in range(SEQ_PER_STEP):
            if finish:
                o_ref[s] = a_out[s] / l_out[s]
            else:
                m_ref[s] = m_out[s]
                l_ref[s] = l_out[s]
                acc_ref[s] = a_out[s]

    def page_rows(page_refs, s):
        return jnp.concatenate(
            [page_refs[s * PAGES_PER_STEP + j][0].astype(BF16) for j in range(PAGES_PER_STEP)], axis=0)

    @pl.when(c < n_chunks)
    def _():
        update([(page_rows(kpages, s), page_rows(vpages, s), sel_ref[s * dec_seq:(s + 1) * dec_seq, :])
                for s in range(SEQ_PER_STEP)], False)

    @pl.when(c == n_chunks)
    def _():
        new_rows = PAGE_SIZE * N_KV_HEADS
        update([(knew_ref[s].astype(BF16), vnew_ref[s].astype(BF16),
                 sel_ref[s * dec_seq:(s + 1) * dec_seq, 0:new_rows]) for s in range(SEQ_PER_STEP)], True)


def _sample_attn(page_table, qs, sel, cache_k2, cache_v2, knew, vnew, *, dec_batch, dec_seq, n_pages):
    n_chunks = n_pages // PAGES_PER_STEP
    rows = N_KV_HEADS * GROUP * dec_seq
    page_rows = PAGE_SIZE * N_KV_HEADS

    def page_map(s, j):
        def f(bp, c, pt):
            cc = jnp.minimum(c, n_chunks - 1)
            return (pt[bp * SEQ_PER_STEP + s, cc * PAGES_PER_STEP + j], 0, 0)
        return f

    page_specs = []
    for s in range(SEQ_PER_STEP):
        for j in range(PAGES_PER_STEP):
            page_specs.append(pl.BlockSpec((1, page_rows, HEAD_DIM), page_map(s, j)))
    new_spec = pl.BlockSpec((SEQ_PER_STEP, page_rows, HEAD_DIM), lambda bp, c, pt: (bp, 0, 0))
    in_specs = [
        pl.BlockSpec((SEQ_PER_STEP, rows, HEAD_DIM), lambda bp, c, pt: (bp, 0, 0)),
        pl.BlockSpec((SEQ_PER_STEP * dec_seq, PAGES_PER_STEP * page_rows), lambda bp, c, pt: (bp, c)),
    ] + page_specs + page_specs + [new_spec, new_spec]
    npg = SEQ_PER_STEP * PAGES_PER_STEP
    body = functools.partial(_sample_attn_body, n_chunks=n_chunks, dec_seq=dec_seq)
    return pl.pallas_call(
        body,
        grid_spec=pltpu.PrefetchScalarGridSpec(
            num_scalar_prefetch=1,
            grid=(dec_batch // SEQ_PER_STEP, n_chunks + 1),
            in_specs=in_specs,
            out_specs=pl.BlockSpec((SEQ_PER_STEP, rows, HEAD_DIM), lambda bp, c, pt: (bp, 0, 0)),
            scratch_shapes=[
                pltpu.VMEM((SEQ_PER_STEP, rows, 1), F32),
                pltpu.VMEM((SEQ_PER_STEP, rows, 1), F32),
                pltpu.VMEM((SEQ_PER_STEP, rows, HEAD_DIM), F32),
            ],
        ),
        out_shape=jax.ShapeDtypeStruct((dec_batch, rows, HEAD_DIM), F32),
        compiler_params=pltpu.CompilerParams(
            dimension_semantics=("arbitrary", "arbitrary"), vmem_limit_bytes=VMEM_LIMIT),
        name="sample_attn",
    )(page_table, qs, sel, *([cache_k2] * npg), *([cache_v2] * npg), knew, vnew)


def _log_sigmoid(x):
    return jnp.minimum(x, 0.0) - jnp.log1p(jnp.exp(-jnp.abs(x)))


def _lru_gates(xc, wa_ref, ba_ref, wx_ref, bx_ref, lam_ref):
    xcb = xc.astype(BF16)
    ra = []
    ia = []
    for n in range(N_LRU_BLOCKS):
        blk = xcb[:, n * LRU_BLOCK:(n + 1) * LRU_BLOCK]
        ra.append(_dot(blk, wa_ref[n]))
        ia.append(_dot(blk, wx_ref[n]))
    r = _sigmoid(jnp.concatenate(ra, axis=1) + ba_ref[...])
    ig = _sigmoid(jnp.concatenate(ia, axis=1) + bx_ref[...])
    log_a = (LRU_C * r) * _log_sigmoid(lam_ref[...])
    a = jnp.exp(log_a)
    b = jnp.sqrt(-jnp.tanh(log_a) * (a * a + 1.0)) * (ig * xc)
    return a, b


TL = 512


def _lru_prompt_body(x_ref, mod_ref, g_ref, win_ref, cw_ref, cb_ref, wa_ref, ba_ref, wx_ref, bx_ref,
                     lam_ref, wout_ref, y_ref, conv_ref, hl_ref, xpad, a_scr, b_scr, h_scr, hcar):
    i = pl.program_id(1)

    @pl.when(i == 0)
    def _():
        xpad[0:SUBLANES] = jnp.zeros((SUBLANES, LRU_WIDTH), F32)
        hcar[...] = jnp.zeros(hcar.shape, F32)

    x = x_ref[...]
    mod = mod_ref[0]
    h = _modulated_norm(x, g_ref[...], mod).astype(BF16)
    xb = _dot(h, win_ref[:, 0:LRU_WIDTH])
    gg = _dot(h, win_ref[:, LRU_WIDTH:2 * LRU_WIDTH])
    xpad[SUBLANES:SUBLANES + TL] = xb
    xc = cw_ref[CONV_W - 1:CONV_W] * xb + cb_ref[...]
    for j in range(CONV_W - 1):
        off = SUBLANES - (CONV_W - 1) + j
        xc = xc + cw_ref[j:j + 1] * xpad[off:off + TL]
    tail = xpad[TL + SUBLANES - (CONV_W - 1):TL + SUBLANES]
    conv_ref[0] = tail
    xpad[SUBLANES - (CONV_W - 1):SUBLANES] = tail

    a, b = _lru_gates(xc, wa_ref, ba_ref, wx_ref, bx_ref, lam_ref)
    a_scr[...] = a
    b_scr[...] = b
    row = lax.broadcasted_iota(I32, (SUBLANES, LRU_WIDTH), 0)

    def group(j, hprev):
        r0 = pl.multiple_of(j * SUBLANES, SUBLANES)
        aa = a_scr[pl.ds(r0, SUBLANES), :]
        bb = b_scr[pl.ds(r0, SUBLANES), :]
        d = 1
        while d < SUBLANES:
            a_sh = pltpu.roll(aa, d, axis=0)
            b_sh = pltpu.roll(bb, d, axis=0)
            m = row >= d
            bb = jnp.where(m, aa * b_sh + bb, bb)
            aa = jnp.where(m, aa * a_sh, aa)
            d *= 2
        hh = aa * hprev + bb
        h_scr[pl.ds(r0, SUBLANES), :] = hh
        return jnp.broadcast_to(hh[SUBLANES - 1:SUBLANES, :], (SUBLANES, LRU_WIDTH))

    hlast = lax.fori_loop(0, TL // SUBLANES, group, hcar[...], unroll=2)
    hcar[...] = hlast
    hl_ref[0] = hlast[0:1]
    hs = h_scr[...]
    out = _dot((hs * _silu(gg)).astype(BF16), wout_ref[...])
    y_ref[...] = x + mod[:, 2 * D_MODEL:3 * D_MODEL] * out


def _lru_prompt(x, mod, g, win, cw, cb, wa, ba, wx, bx, lam, wout, *, batch, seq):
    nb = seq // TL
    blk = lambda b, i: (b * nb + i, 0)
    const2 = lambda b, i: (0, 0)
    const3 = lambda b, i: (0, 0, 0)
    per_b = lambda b, i: (b, 0, 0)
    return pl.pallas_call(
        _lru_prompt_body,
        grid=(batch, nb),
        in_specs=[
            pl.BlockSpec((TL, D_MODEL), blk),
            pl.BlockSpec((1, 1, 3 * D_MODEL), per_b),
            pl.BlockSpec((1, D_MODEL), const2),
            pl.BlockSpec((D_MODEL, 2 * LRU_WIDTH), const2),
            pl.BlockSpec((CONV_W, LRU_WIDTH), const2),
            pl.BlockSpec((1, LRU_WIDTH), const2),
            pl.BlockSpec((N_LRU_BLOCKS, LRU_BLOCK, LRU_BLOCK), const3),
            pl.BlockSpec((1, LRU_WIDTH), const2),
            pl.BlockSpec((N_LRU_BLOCKS, LRU_BLOCK, LRU_BLOCK), const3),
            pl.BlockSpec((1, LRU_WIDTH), const2),
            pl.BlockSpec((1, LRU_WIDTH), const2),
            pl.BlockSpec((LRU_WIDTH, D_MODEL), const2),
        ],
        out_specs=[
            pl.BlockSpec((TL, D_MODEL), blk),
            pl.BlockSpec((1, CONV_W - 1, LRU_WIDTH), per_b),
            pl.BlockSpec((1, 1, LRU_WIDTH), per_b),
        ],
        out_shape=[
            jax.ShapeDtypeStruct((batch * seq, D_MODEL), F32),
            jax.ShapeDtypeStruct((batch, CONV_W - 1, LRU_WIDTH), F32),
            jax.ShapeDtypeStruct((batch, 1, LRU_WIDTH), F32),
        ],
        scratch_shapes=[
            pltpu.VMEM((TL + SUBLANES, LRU_WIDTH), F32),
            pltpu.VMEM((TL, LRU_WIDTH), F32),
            pltpu.VMEM((TL, LRU_WIDTH), F32),
            pltpu.VMEM((TL, LRU_WIDTH), F32),
            pltpu.VMEM((SUBLANES, LRU_WIDTH), F32),
        ],
        compiler_params=pltpu.CompilerParams(
            dimension_semantics=("arbitrary", "arbitrary"), vmem_limit_bytes=VMEM_LIMIT),
        name="lru_prompt",
    )(x, mod, g, win, cw, cb, wa, ba, wx, bx, lam, wout)


def _lru_sample_body(o_ref, gs_ref, x_ref, mod0_ref, wout0_ref, mod_ref, g_ref, win_ref, cw_ref, cb_ref,
                     wa_ref, ba_ref, wx_ref, bx_ref, lam_ref, wout_ref, sc_ref, sh_ref,
                     y_ref, conv_ref, hl_ref, *, nb, nt):
    x0 = x_ref[...]
    y0 = _dot((o_ref[...] * gs_ref[...].astype(F32)).astype(BF16), wout0_ref[...])
    x = x0 + mod0_ref[:, 2 * D_MODEL:3 * D_MODEL] * y0
    mod = mod_ref[...]
    h = _modulated_norm(x, g_ref[...], mod).astype(BF16)
    xb = _dot(h, win_ref[:, 0:LRU_WIDTH])
    gg = _dot(h, win_ref[:, LRU_WIDTH:2 * LRU_WIDTH])
    slabs = [sc_ref[j] for j in range(CONV_W - 1)] + [xb[t * nb:(t + 1) * nb] for t in range(nt)]
    xcs = []
    for t in range(nt):
        acc = cb_ref[...] + cw_ref[0:1] * slabs[t]
        for j in range(1, CONV_W):
            acc = acc + cw_ref[j:j + 1] * slabs[t + j]
        xcs.append(acc)
    for j in range(CONV_W - 1):
        conv_ref[j] = slabs[nt + j]
    xc = jnp.concatenate(xcs, axis=0)
    a, b = _lru_gates(xc, wa_ref, ba_ref, wx_ref, bx_ref, lam_ref)
    hprev = sh_ref[...]
    hs = []
    for t in range(nt):
        hprev = a[t * nb:(t + 1) * nb] * hprev + b[t * nb:(t + 1) * nb]
        hs.append(hprev)
    hl_ref[...] = hprev
    out = _dot((jnp.concatenate(hs, axis=0) * _silu(gg)).astype(BF16), wout_ref[...])
    y_ref[...] = x + mod[:, 2 * D_MODEL:3 * D_MODEL] * out


def _lru_sample(o, gs, x, mod0, wout0, mod, g, win, cw, cb, wa, ba, wx, bx, lam, wout, sc, sh, *, nb, nt):
    args = (o, gs, x, mod0, wout0, mod, g, win, cw, cb, wa, ba, wx, bx, lam, wout, sc, sh)

    def full(a):
        nd = a.ndim
        return pl.BlockSpec(a.shape, lambda i, nd=nd: (0,) * nd)

    out_shape = [
        jax.ShapeDtypeStruct((nt * nb, D_MODEL), F32),
        jax.ShapeDtypeStruct((CONV_W - 1, nb, LRU_WIDTH), F32),
        jax.ShapeDtypeStruct((nb, LRU_WIDTH), F32),
    ]
    body = functools.partial(_lru_sample_body, nb=nb, nt=nt)
    return pl.pallas_call(
        body,
        grid=(1,),
        in_specs=[full(a) for a in args],
        out_specs=[pl.BlockSpec(s.shape, lambda i, nd=len(s.shape): (0,) * nd) for s in out_shape],
        out_shape=out_shape,
        compiler_params=pltpu.CompilerParams(
            dimension_semantics=("arbitrary",), vmem_limit_bytes=VMEM_LIMIT),
        name="lru_sample",
    )(*args)


def kernel(x_prompt, x_sample, cache_k, cache_v, cache_idx_k, state_conv, state_h, page_table, c_prompt, c_sample, norm_g, ada_w, ada_b, attn_w_in, attn_q_norm, attn_k_norm, attn_w_out, lru_w_in, lru_conv_w, lru_conv_b, lru_w_a, lru_b_a, lru_w_x, lru_b_x, lru_lam, lru_w_out):
    B, S, _ = x_prompt.shape
    Bd, T, _ = x_sample.shape
    n_pages = page_table.shape[1]
    past = n_pages * PAGE_SIZE
    n_pool = cache_k.shape[1]
    assert S % QB == 0 and S % TL == 0 and n_pages % PAGES_PER_STEP == 0
    assert Bd % SEQ_PER_STEP == 0 and SEQ_PER_STEP * T == SUBLANES

    mod = _ada(jnp.concatenate([c_prompt, c_sample], axis=0), ada_w, ada_b)
    mod_p = [mod[l, :B].reshape(B, 1, 3 * D_MODEL) for l in range(2)]
    mod_s = [jnp.repeat(mod[l, B:], T, axis=0) for l in range(2)]

    w = attn_w_in[0]
    o_k, o_v, o_qi, o_ki, o_wi, o_g = 1024, 1280, 1536, 2560, 2624, 2640
    wcat = jnp.concatenate(
        [w[:, :o_ki], w[:, o_g:], w[:, o_ki:o_g], jnp.zeros((D_MODEL, PROJ_WIDTH - w.shape[1]), F32)],
        axis=1).astype(BF16)
    g0 = norm_g[0].reshape(1, D_MODEL)
    qn = attn_q_norm[0].reshape(1, HEAD_DIM)
    kn = attn_k_norm[0].reshape(1, HEAD_DIM)
    wout0 = attn_w_out[0].astype(BF16)

    tm = 512
    tabs_p = _rope_tables(jnp.arange(S))
    xp = x_prompt.reshape(B * S, D_MODEL)
    q, k, v, qi, kw, ki, gs = _proj(xp, mod_p[0], g0, wcat, qn, kn, tabs_p,
                                    tm=tm, rows_per_mod=S, tab_blocks=S // tm)
    x1p = _attn_prompt(q, qi, kw, k, v, gs, xp, mod_p[0], wout0, batch=B, seq=S)

    pos_s = past + jnp.arange(T)
    tabs_s = jnp.tile(_rope_tables(pos_s), (1, Bd, 1))
    xs = x_sample.reshape(Bd * T, D_MODEL)
    q2, k2, v2, qi2, kw2, ki2, gs2 = _proj(xs, mod_s[0].reshape(1, Bd * T, 3 * D_MODEL), g0, wcat, qn, kn,
                                           tabs_s, tm=Bd * T, rows_per_mod=Bd * T, tab_blocks=1)
    qir = qi2.reshape(Bd, T, N_IDX_HEADS, IDX_DIM).transpose(0, 2, 1, 3).reshape(Bd, N_IDX_HEADS * T, IDX_DIM)
    wcol = kw2[:, IDX_DIM:IDX_DIM + N_IDX_HEADS].reshape(Bd, T, N_IDX_HEADS).transpose(0, 2, 1)
    wcol = wcol.reshape(Bd, N_IDX_HEADS * T, 1)
    pad_rows = lambda a: jnp.pad(a, ((0, 0), (0, PAGE_SIZE - T)) + ((0, 0),) * (a.ndim - 2))
    ki_new = pad_rows(ki2.reshape(Bd, T, IDX_DIM)).transpose(0, 2, 1)
    cache_ik_t = cache_idx_k.reshape(n_pool, PAGE_SIZE, IDX_DIM).transpose(0, 2, 1)
    scores = _sample_scores(page_table, qir, wcol, cache_ik_t, ki_new,
                            dec_batch=Bd, dec_seq=T, n_pages=n_pages)
    sel = _sample_select(scores, past=past, dec_seq=T)
    qs = q2.reshape(Bd, T, N_KV_HEADS, GROUP, HEAD_DIM).transpose(0, 2, 3, 1, 4)
    qs = qs.reshape(Bd, N_KV_HEADS * GROUP * T, HEAD_DIM)
    new_pad = ((0, 0), (0, (PAGE_SIZE - T) * N_KV_HEADS), (0, 0))
    k_new = jnp.pad(k2.reshape(Bd, T * N_KV_HEADS, HEAD_DIM), new_pad)
    v_new = jnp.pad(v2.reshape(Bd, T * N_KV_HEADS, HEAD_DIM), new_pad)
    sel = jnp.repeat(sel, N_KV_HEADS, axis=1)
    o2 = _sample_attn(page_table, qs, sel,
                      cache_k.reshape(n_pool, PAGE_SIZE * N_KV_HEADS, HEAD_DIM),
                      cache_v.reshape(n_pool, PAGE_SIZE * N_KV_HEADS, HEAD_DIM),
                      k_new, v_new, dec_batch=Bd, dec_seq=T, n_pages=n_pages)
    o2 = o2.reshape(Bd, N_KV_HEADS, GROUP, T, HEAD_DIM).transpose(3, 0, 1, 2, 4).reshape(T * Bd, ATTN_WIDTH)

    g1 = norm_g[1].reshape(1, D_MODEL)
    win = lru_w_in[0].astype(BF16)
    cw = lru_conv_w[0]
    cb = lru_conv_b[0].reshape(1, LRU_WIDTH)
    wa = lru_w_a[0].astype(BF16)
    wx = lru_w_x[0].astype(BF16)
    ba = lru_b_a[0].reshape(1, LRU_WIDTH)
    bx = lru_b_x[0].reshape(1, LRU_WIDTH)
    lam = lru_lam[0].reshape(1, LRU_WIDTH)
    wout1 = lru_w_out[0].astype(BF16)

    yp, conv_p, h_p = _lru_prompt(x1p, mod_p[1], g1, win, cw, cb, wa, ba, wx, bx, lam, wout1, batch=B, seq=S)

    tmaj = lambda a: a.reshape(Bd, T, -1).transpose(1, 0, 2).reshape(T * Bd, -1)
    ys, conv_s, h_s = _lru_sample(
        o2, tmaj(gs2), tmaj(xs), tmaj(mod_s[0]), wout0, tmaj(mod_s[1]), g1, win, cw, cb, wa, ba, wx, bx,
        lam, wout1, state_conv[0].transpose(1, 0, 2), state_h[0], nb=Bd, nt=T)

    y_prompt = yp.reshape(B, S, D_MODEL)
    y_sample = ys.reshape(T, Bd, D_MODEL).transpose(1, 0, 2)
    return (y_prompt, y_sample,
            k.reshape(1, B, S, N_KV_HEADS, HEAD_DIM), v.reshape(1, B, S, N_KV_HEADS, HEAD_DIM),
            ki.reshape(1, B, S, IDX_DIM),
            k2.reshape(1, Bd, T, N_KV_HEADS, HEAD_DIM), v2.reshape(1, Bd, T, N_KV_HEADS, HEAD_DIM),
            ki2.reshape(1, Bd, T, IDX_DIM),
            conv_p.reshape(1, B, CONV_W - 1, LRU_WIDTH), h_p.reshape(1, B, LRU_WIDTH),
            conv_s.transpose(1, 0, 2).reshape(1, Bd, CONV_W - 1, LRU_WIDTH), h_s.reshape(1, Bd, LRU_WIDTH))
```
